```python
import math
import jax
import jax.numpy as jnp
from jax import lax
import numpy as np

D_MODEL = 2048
BATCH = 2
SEQ = 4096
DEPTH = 4
DEC_BATCH = 8
DEC_SEQ = 1
PAST_LEN = 16384
PAGE_SIZE = 128

N_MIXERS = 3
N_HEADS = 16
HEAD_DIM = D_MODEL // N_HEADS
N_KV_HEADS = 4
GROUP = N_HEADS // N_KV_HEADS
N_IDX_HEADS = 16
IDX_DIM = 64
TOPK_MAX = 256
ROPE_THETA = 500000.0
QBLK = 128
SSM_GROUP = 16
N_SSM_GROUPS = D_MODEL // SSM_GROUP
SSM_STATE = 64
D_FF = -(-(8 * D_MODEL) // (3 * 256)) * 256
EPS = 1e-6
FOX_BIAS_INIT = 4.0
N_DSA = (DEPTH + 2) // 3
N_FOX = (DEPTH + 1) // 3
N_SSM = DEPTH // 3
DSA_IN = N_HEADS * HEAD_DIM + 2 * N_KV_HEADS * HEAD_DIM + N_IDX_HEADS * IDX_DIM + IDX_DIM + N_IDX_HEADS
FOX_IN = N_HEADS * HEAD_DIM + 2 * N_KV_HEADS * HEAD_DIM + N_HEADS

kernel_name = 'hybrid_dsa_fox_s5_decode_step'


def rms_norm(x, g):
    x32 = x.astype(jnp.float32)
    y = x32 * lax.rsqrt(jnp.mean(x32 * x32, axis=-1, keepdims=True) + EPS) * g.astype(jnp.float32)
    return y.astype(x.dtype)


def rope(x, pos):
    rot = x.shape[-1] // 4
    half = rot // 2
    inv = ROPE_THETA ** (-jnp.arange(half, dtype=jnp.float32) / half)
    ang = pos.astype(jnp.float32)[:, None] * inv[None, :]
    cos = jnp.cos(ang)[None, :, None, :]
    sin = jnp.sin(ang)[None, :, None, :]
    x32 = x.astype(jnp.float32)
    x1 = x32[..., :half]
    x2 = x32[..., half:rot]
    out = jnp.concatenate([x1 * cos - x2 * sin, x2 * cos + x1 * sin, x32[..., rot:]], axis=-1)
    return out.astype(x.dtype)


def to_blocks(a):
    b, t = a.shape[:2]
    return a.reshape((b, t // QBLK, QBLK) + a.shape[2:]).swapaxes(0, 1)


def from_blocks(a):
    nb, b, qb = a.shape[:3]
    return a.swapaxes(0, 1).reshape((b, nb * qb) + a.shape[3:])


def gather_rows(a, idx):
    return jax.vmap(lambda ab, ib: ab[ib])(a, idx)


def swiglu(h, w_gu, w_down):
    g, u = jnp.split(h @ w_gu, 2, axis=-1)
    return (jax.nn.silu(g) * u) @ w_down


def dsa_project(h, pos, w_in, q_gain, k_gain, ik_gain):
    bn, t, _ = h.shape
    sizes = [N_HEADS * HEAD_DIM, N_KV_HEADS * HEAD_DIM, N_KV_HEADS * HEAD_DIM, N_IDX_HEADS * IDX_DIM, IDX_DIM]
    cuts = np.cumsum(sizes).tolist()
    q, k, v, iq, ik, iw = jnp.split(h @ w_in, cuts, axis=-1)
    q = rope(rms_norm(q.reshape(bn, t, N_HEADS, HEAD_DIM), q_gain), pos)
    k = rope(rms_norm(k.reshape(bn, t, N_KV_HEADS, HEAD_DIM), k_gain), pos)
    v = v.reshape(bn, t, N_KV_HEADS, HEAD_DIM)
    iq = rope(iq.reshape(bn, t, N_IDX_HEADS, IDX_DIM), pos)
    ik = rope(rms_norm(ik, ik_gain)[:, :, None, :], pos)[:, :, 0]
    iw = iw.astype(jnp.float32) * (N_IDX_HEADS ** -0.5 * IDX_DIM ** -0.5)
    return q, k, v, iq, ik, iw


def index_scores(iq, iw, ik):
    r = jax.nn.relu(jnp.einsum('btnd,bsd->btns', iq, ik).astype(jnp.float32))
    return jnp.einsum('btn,btns->bts', iw, r)


def sparse_attend(q, ksel, vsel, valid):
    bn, tq = q.shape[:2]
    qg = q.reshape(bn, tq, N_KV_HEADS, GROUP, HEAD_DIM)
    s = jnp.einsum('btjgd,btkjd->btjgk', qg, ksel).astype(jnp.float32) * (HEAD_DIM ** -0.5)
    s = jnp.where(valid[:, :, None, None, :], s, -jnp.inf)
    p = jax.nn.softmax(s, axis=-1).astype(vsel.dtype)
    o = jnp.einsum('btjgk,btkjd->btjgd', p, vsel)
    return o.reshape(bn, tq, N_HEADS * HEAD_DIM)


def dsa_prompt(h, w_in, w_out, q_gain, k_gain, ik_gain):
    bn, t, _ = h.shape
    pos = jnp.arange(t)
    q, k, v, iq, ik, iw = dsa_project(h, pos, w_in, q_gain, k_gain, ik_gain)
    n_sel = min(TOPK_MAX, t // 4)

    def block(args):
        qb, iqb, iwb, pb = args
        sc = index_scores(iqb, iwb, ik)
        sc = jnp.where(pos[None, None, :] <= pb[None, :, None], sc, -jnp.inf)
        _, idx = lax.top_k(sc, n_sel)
        valid = idx <= pb[None, :, None]
        return sparse_attend(qb, gather_rows(k, idx), gather_rows(v, idx), valid)

    o = lax.map(block, (to_blocks(q), to_blocks(iq), to_blocks(iw), pos.reshape(-1, QBLK)))
    return from_blocks(o) @ w_out, k, v, ik


def dsa_sample(h, pool_k, pool_v, pool_ik, page_table, w_in, w_out, q_gain, k_gain, ik_gain):
    bn, t, _ = h.shape
    pos = PAST_LEN + jnp.arange(t)
    q, k, v, iq, ik, iw = dsa_project(h, pos, w_in, q_gain, k_gain, ik_gain)
    n_keys = PAST_LEN + t
    ik_past = pool_ik[page_table].reshape(bn, PAST_LEN, IDX_DIM).astype(ik.dtype)
    ik_all = jnp.concatenate([ik_past, ik], axis=1)
    sc = index_scores(iq, iw, ik_all)
    sc = jnp.where(jnp.arange(n_keys)[None, None, :] <= pos[None, :, None], sc, -jnp.inf)
    n_sel = min(TOPK_MAX, n_keys // 4)
    _, idx = lax.top_k(sc, n_sel)
    in_past = idx < PAST_LEN
    pidx = jnp.minimum(idx, PAST_LEN - 1)
    phys = gather_rows(page_table, pidx // PAGE_SIZE)
    off = pidx % PAGE_SIZE
    nidx = jnp.clip(idx - PAST_LEN, 0, t - 1)

    def select(pool, new):
        return jnp.where(in_past[..., None, None], pool[phys, off].astype(new.dtype), gather_rows(new, nidx))

    valid = idx <= pos[None, :, None]
    o = sparse_attend(q, select(pool_k, k), select(pool_v, v), valid)
    return o @ w_out, k, v, ik


def fox_project(h, w_in, b_f, q_gain, k_gain):
    bn, t, _ = h.shape
    cuts = np.cumsum([N_HEADS * HEAD_DIM, N_KV_HEADS * HEAD_DIM, N_KV_HEADS * HEAD_DIM]).tolist()
    q, k, v, fl = jnp.split(h @ w_in, cuts, axis=-1)
    q = rms_norm(q.reshape(bn, t, N_HEADS, HEAD_DIM), q_gain)
    k = rms_norm(k.reshape(bn, t, N_KV_HEADS, HEAD_DIM), k_gain)
    v = v.reshape(bn, t, N_KV_HEADS, HEAD_DIM)
    logf = jax.nn.log_sigmoid(fl.astype(jnp.float32) + b_f.astype(jnp.float32))
    return q, k, v, logf


def fox_attend(q, k, v, cq, ck, qpos, kpos):
    bn, tq = q.shape[:2]
    s_len = k.shape[1]
    qg = q.reshape(bn, tq, N_KV_HEADS, GROUP, HEAD_DIM)
    s = jnp.einsum('btjgd,bsjd->bjgts', qg, k).astype(jnp.float32) * (HEAD_DIM ** -0.5)
    cq_ = cq.reshape(bn, tq, N_KV_HEADS, GROUP).transpose(0, 2, 3, 1)[..., None]
    ck_ = ck.reshape(bn, s_len, N_KV_HEADS, GROUP).transpose(0, 2, 3, 1)[:, :, :, None, :]
    s = s + (cq_ - ck_)
    s = jnp.where(kpos[None, :] <= qpos[:, None], s, -jnp.inf)
    p = jax.nn.softmax(s, axis=-1).astype(v.dtype)
    o = jnp.einsum('bjgts,bsjd->btjgd', p, v)
    return o.reshape(bn, tq, N_HEADS * HEAD_DIM)


def fox_prompt(h, w_in, b_f, w_out, q_gain, k_gain):
    bn, t, _ = h.shape
    q, k, v, logf = fox_project(h, w_in, b_f, q_gain, k_gain)
    c = jnp.cumsum(logf, axis=1)
    pos = jnp.arange(t)

    def block(args):
        qb, cb, pb = args
        return fox_attend(qb, k, v, cb, c, pb, pos)

    o = lax.map(block, (to_blocks(q), to_blocks(c), pos.reshape(-1, QBLK)))
    return from_blocks(o) @ w_out, k, v, logf


def fox_sample(h, pool_k, pool_v, pool_lf, page_table, w_in, b_f, w_out, q_gain, k_gain):
    bn, t, _ = h.shape
    q, k, v, logf = fox_project(h, w_in, b_f, q_gain, k_gain)
    k_all = jnp.concatenate([pool_k[page_table].reshape(bn, PAST_LEN, N_KV_HEADS, HEAD_DIM).astype(k.dtype), k], axis=1)
    v_all = jnp.concatenate([pool_v[page_table].reshape(bn, PAST_LEN, N_KV_HEADS, HEAD_DIM).astype(v.dtype), v], axis=1)
    lf_all = jnp.concatenate([pool_lf[page_table].reshape(bn, PAST_LEN, N_HEADS).astype(jnp.float32), logf], axis=1)
    c = jnp.cumsum(lf_all, axis=1)
    pos = PAST_LEN + jnp.arange(t)
    o = fox_attend(q, k_all, v_all, c[:, PAST_LEN:], c, pos, jnp.arange(PAST_LEN + t))
    return o @ w_out, k, v, logf


def s5_mix(u, h0_re, h0_im, a_re, a_im, log_dt, b_re, b_im, c_re, c_im, d_skip, w_glu):
    f32 = jnp.float32
    bn, t, _ = u.shape
    u32 = u.astype(f32)
    ug = u32.reshape(bn, t, N_SSM_GROUPS, SSM_GROUP)
    lam = lax.complex(a_re.astype(f32), a_im.astype(f32))
    dt = jnp.exp(log_dt.astype(f32))[:, None]
    a_bar = jnp.exp(lam * dt)
    b_scale = (a_bar - 1.0) / lam
    bu = lax.complex(jnp.einsum('gnc,btgc->btgn', b_re.astype(f32), ug),
                     jnp.einsum('gnc,btgc->btgn', b_im.astype(f32), ug)) * b_scale
    h0 = lax.complex(h0_re.astype(f32), h0_im.astype(f32))
    bu = bu.at[:, 0].add(a_bar * h0)

    def combine(l, r):
        return (l[0] * r[0], r[0] * l[1] + r[1])

    _, hs = lax.associative_scan(combine, (jnp.broadcast_to(a_bar, bu.shape), bu), axis=1)
    y = (jnp.einsum('gcn,btgn->btgc', c_re.astype(f32), jnp.real(hs))
         - jnp.einsum('gcn,btgn->btgc', c_im.astype(f32), jnp.imag(hs)))
    y = y.reshape(bn, t, D_MODEL) + d_skip.astype(f32) * u32
    g = jax.nn.gelu(y).astype(u.dtype)
    ab = g @ w_glu
    out = ab[..., :D_MODEL] * jax.nn.sigmoid(ab[..., D_MODEL:])
    h_last = hs[:, -1]
    return out.astype(u.dtype), jnp.real(h_last), jnp.imag(h_last)


def setup_inputs(seed: int = 0) -> dict:
    key = jax.random.key(seed)
    ks = iter(jax.random.split(key, 48))
    f32 = jnp.float32

    def nrm(shape, scale):
        return jax.random.normal(next(ks), shape, f32) * scale

    n_pages = PAST_LEN // PAGE_SIZE
    n_used = DEC_BATCH * n_pages
    n_pool = n_used + n_used // 4
    perm = jax.random.permutation(next(ks), n_pool)
    page_table = perm[:n_used].reshape(DEC_BATCH, n_pages).astype(jnp.int32)
    n_idx = jnp.arange(SSM_STATE, dtype=f32)
    return {
        'x_prompt': nrm((BATCH, SEQ, D_MODEL), 1.0),
        'x_sample': nrm((DEC_BATCH, DEC_SEQ, D_MODEL), 1.0),
        'cache_dsa_k': nrm((N_DSA, n_pool, PAGE_SIZE, N_KV_HEADS, HEAD_DIM), 1.0),
        'cache_dsa_v': nrm((N_DSA, n_pool, PAGE_SIZE, N_KV_HEADS, HEAD_DIM), 1.0),
        'cache_dsa_idx_k': nrm((N_DSA, n_pool, PAGE_SIZE, IDX_DIM), 1.0),
        'cache_fox_k': nrm((N_FOX, n_pool, PAGE_SIZE, N_KV_HEADS, HEAD_DIM), 1.0),
        'cache_fox_v': nrm((N_FOX, n_pool, PAGE_SIZE, N_KV_HEADS, HEAD_DIM), 1.0),
        'cache_fox_logf': jax.nn.log_sigmoid(FOX_BIAS_INIT + nrm((N_FOX, n_pool, PAGE_SIZE, N_HEADS), 1.0)),
        'state_ssm_re': nrm((N_SSM, DEC_BATCH, N_SSM_GROUPS, SSM_STATE), 1.0),
        'state_ssm_im': nrm((N_SSM, DEC_BATCH, N_SSM_GROUPS, SSM_STATE), 1.0),
        'page_table': page_table,
        'norm_mix': 1.0 + nrm((DEPTH, D_MODEL), 0.02),
        'norm_ffn': 1.0 + nrm((DEPTH, D_MODEL), 0.02),
        'dsa_w_in': nrm((N_DSA, D_MODEL, DSA_IN), D_MODEL ** -0.5),
        'dsa_w_out': nrm((N_DSA, N_HEADS * HEAD_DIM, D_MODEL), (N_HEADS * HEAD_DIM) ** -0.5),
        'dsa_q_gain': 1.0 + nrm((N_DSA, HEAD_DIM), 0.02),
        'dsa_k_gain': 1.0 + nrm((N_DSA, HEAD_DIM), 0.02),
        'dsa_ik_gain': 1.0 + nrm((N_DSA, IDX_DIM), 0.02),
        'fox_w_in': nrm((N_FOX, D_MODEL, FOX_IN), D_MODEL ** -0.5),
        'fox_b_f': FOX_BIAS_INIT + nrm((N_FOX, N_HEADS), 0.1),
        'fox_w_out': nrm((N_FOX, N_HEADS * HEAD_DIM, D_MODEL), (N_HEADS * HEAD_DIM) ** -0.5),
        'fox_q_gain': 1.0 + nrm((N_FOX, HEAD_DIM), 0.02),
        'fox_k_gain': 1.0 + nrm((N_FOX, HEAD_DIM), 0.02),
        'ssm_a_re': -0.5 * jnp.exp(nrm((N_SSM, N_SSM_GROUPS, SSM_STATE), 0.01)),
        'ssm_a_im': jnp.pi * n_idx + nrm((N_SSM, N_SSM_GROUPS, SSM_STATE), 0.01),
        'ssm_log_dt': jax.random.uniform(next(ks), (N_SSM, N_SSM_GROUPS), f32, math.log(1e-3), math.log(1e-1)),
        'ssm_b_re': nrm((N_SSM, N_SSM_GROUPS, SSM_STATE, SSM_GROUP), (2 * SSM_GROUP) ** -0.5),
        'ssm_b_im': nrm((N_SSM, N_SSM_GROUPS, SSM_STATE, SSM_GROUP), (2 * SSM_GROUP) ** -0.5),
        'ssm_c_re': nrm((N_SSM, N_SSM_GROUPS, SSM_GROUP, SSM_STATE), (2 * SSM_STATE) ** -0.5),
        'ssm_c_im': nrm((N_SSM, N_SSM_GROUPS, SSM_GROUP, SSM_STATE), (2 * SSM_STATE) ** -0.5),
        'ssm_d': nrm((N_SSM, D_MODEL), 0.5),
        'ssm_w_glu': nrm((N_SSM, D_MODEL, 2 * D_MODEL), D_MODEL ** -0.5),
        'ffn_w_gu': nrm((DEPTH, D_MODEL, 2 * D_FF), D_MODEL ** -0.5),
        'ffn_w_down': nrm((DEPTH, D_FF, D_MODEL), D_FF ** -0.5),
    }


def reference(x_prompt, x_sample, cache_dsa_k, cache_dsa_v, cache_dsa_idx_k, cache_fox_k, cache_fox_v,
              cache_fox_logf, state_ssm_re, state_ssm_im, page_table, norm_mix, norm_ffn,
              dsa_w_in, dsa_w_out, dsa_q_gain, dsa_k_gain, dsa_ik_gain,
              fox_w_in, fox_b_f, fox_w_out, fox_q_gain, fox_k_gain,
              ssm_a_re, ssm_a_im, ssm_log_dt, ssm_b_re, ssm_b_im, ssm_c_re, ssm_c_im, ssm_d, ssm_w_glu,
              ffn_w_gu, ffn_w_down):
    xp, xs = x_prompt, x_sample
    dsa_kp, dsa_vp, dsa_ikp, dsa_ks, dsa_vs, dsa_iks = [], [], [], [], [], []
    fox_kp, fox_vp, fox_lfp, fox_ks, fox_vs, fox_lfs = [], [], [], [], [], []
    ssm_rep, ssm_imp, ssm_res, ssm_ims = [], [], [], []
    for i in range(DEPTH):
        kind, j = i % N_MIXERS, i // N_MIXERS
        hp = rms_norm(xp, norm_mix[i])
        hs = rms_norm(xs, norm_mix[i])
        if kind == 0:
            mp, kp, vp, ikp = dsa_prompt(hp, dsa_w_in[j], dsa_w_out[j], dsa_q_gain[j], dsa_k_gain[j], dsa_ik_gain[j])
            ms, ks_, vs_, iks = dsa_sample(hs, cache_dsa_k[j], cache_dsa_v[j], cache_dsa_idx_k[j], page_table,
                                           dsa_w_in[j], dsa_w_out[j], dsa_q_gain[j], dsa_k_gain[j], dsa_ik_gain[j])
            dsa_kp.append(kp); dsa_vp.append(vp); dsa_ikp.append(ikp)
            dsa_ks.append(ks_); dsa_vs.append(vs_); dsa_iks.append(iks)
        elif kind == 1:
            mp, kp, vp, lfp = fox_prompt(hp, fox_w_in[j], fox_b_f[j], fox_w_out[j], fox_q_gain[j], fox_k_gain[j])
            ms, ks_, vs_, lfs = fox_sample(hs, cache_fox_k[j], cache_fox_v[j], cache_fox_logf[j], page_table,
                                           fox_w_in[j], fox_b_f[j], fox_w_out[j], fox_q_gain[j], fox_k_gain[j])
            fox_kp.append(kp); fox_vp.append(vp); fox_lfp.append(lfp)
            fox_ks.append(ks_); fox_vs.append(vs_); fox_lfs.append(lfs)
        else:
            zeros = jnp.zeros((xp.shape[0], N_SSM_GROUPS, SSM_STATE), jnp.float32)
            ssm_w = (ssm_a_re[j], ssm_a_im[j], ssm_log_dt[j], ssm_b_re[j], ssm_b_im[j],
                     ssm_c_re[j], ssm_c_im[j], ssm_d[j], ssm_w_glu[j])
            mp, rp, ip = s5_mix(hp, zeros, zeros, *ssm_w)
            ms, rs, is_ = s5_mix(hs, state_ssm_re[j], state_ssm_im[j], *ssm_w)
            ssm_rep.append(rp); ssm_imp.append(ip); ssm_res.append(rs); ssm_ims.append(is_)
        xp = xp + mp
        xs = xs + ms
        xp = xp + swiglu(rms_norm(xp, norm_ffn[i]), ffn_w_gu[i], ffn_w_down[i])
        xs = xs + swiglu(rms_norm(xs, norm_ffn[i]), ffn_w_gu[i], ffn_w_down[i])
    return (xp, xs,
            jnp.stack(dsa_kp), jnp.stack(dsa_vp), jnp.stack(dsa_ikp),
            jnp.stack(dsa_ks), jnp.stack(dsa_vs), jnp.stack(dsa_iks),
            jnp.stack(fox_kp), jnp.stack(fox_vp), jnp.stack(fox_lfp),
            jnp.stack(fox_ks), jnp.stack(fox_vs), jnp.stack(fox_lfs),
            jnp.stack(ssm_rep), jnp.stack(ssm_imp), jnp.stack(ssm_res), jnp.stack(ssm_ims))
```

```python
import functools
import math

import jax
import jax.numpy as jnp
import numpy as np
from jax import lax
from jax.experimental import pallas as pl
from jax.experimental.pallas import tpu as pltpu

F32 = jnp.float32
BF16 = jnp.bfloat16
I32 = jnp.int32

D_MODEL = 2048
N_HEADS = 16
HEAD_DIM = 128
N_KV = 4
GROUP = N_HEADS // N_KV
N_IDX_HEADS = 16
IDX_DIM = 64
TOPK_MAX = 256
ROPE_THETA = 500000.0
PAGE = 128
SSM_GROUP = 16
N_SSM_GROUPS = D_MODEL // SSM_GROUP
SSM_STATE = 64
SSM_CHUNK = 16
EPS = 1e-6
N_MIXERS = 3

LANES = 128
NEG = -1e30
INT_MIN = -2 ** 31
HI = lax.Precision.HIGHEST
VMEM_LIMIT = 56 * 1024 * 1024

NT = (((1,), (1,)), ((), ()))


def _params(sem):
    return pltpu.CompilerParams(dimension_semantics=sem, vmem_limit_bytes=VMEM_LIMIT)


def _rms(x, gain):
    return x * lax.rsqrt(jnp.mean(x * x, axis=-1, keepdims=True) + EPS) * gain


def _proj_kernel(x_ref, g_ref, w_ref, o_ref, h_ref):
    @pl.when(pl.program_id(1) == 0)
    def _():
        h_ref[...] = _rms(x_ref[...], g_ref[...]).astype(BF16)

    o_ref[...] = jnp.dot(h_ref[...], w_ref[...], preferred_element_type=F32)


def _pick_tn(n, cap=1536):
    k = -(-n // cap)
    return -(-n // (k * LANES)) * LANES


def norm_proj(x, gain, w, tm):
    m, d = x.shape
    n = w.shape[1]
    tn = _pick_tn(n)
    return pl.pallas_call(
        _proj_kernel,
        grid=(m // tm, pl.cdiv(n, tn)),
        in_specs=[pl.BlockSpec((tm, d), lambda i, j: (i, 0)),
                  pl.BlockSpec((1, d), lambda i, j: (0, 0)),
                  pl.BlockSpec((d, tn), lambda i, j: (0, j))],
        out_specs=pl.BlockSpec((tm, tn), lambda i, j: (i, j)),
        out_shape=jax.ShapeDtypeStruct((m, n), F32),
        scratch_shapes=[pltpu.VMEM((tm, d), BF16)],
        compiler_params=_params(("parallel", "arbitrary")),
        name="norm_proj",
    )(x, gain.reshape(1, d), w)


def _out_res_kernel(x_ref, o_ref, w_ref, y_ref):
    y_ref[...] = x_ref[...] + jnp.dot(o_ref[...].astype(BF16), w_ref[...], preferred_element_type=F32)


def out_res(x, o, w, tm):
    m, d = x.shape
    k = w.shape[0]
    return pl.pallas_call(
        _out_res_kernel,
        grid=(m // tm,),
        in_specs=[pl.BlockSpec((tm, d), lambda i: (i, 0)),
                  pl.BlockSpec((tm, k), lambda i: (i, 0)),
                  pl.BlockSpec((k, d), lambda i: (0, 0))],
        out_specs=pl.BlockSpec((tm, d), lambda i: (i, 0)),
        out_shape=jax.ShapeDtypeStruct((m, d), F32),
        compiler_params=_params(("parallel",)),
        name="out_res",
    )(x, o, w)


def _ffn_kernel(x_ref, g_ref, wg_ref, wu_ref, wd_ref, y_ref, h_ref, acc_ref):
    f = pl.program_id(1)

    @pl.when(f == 0)
    def _():
        h_ref[...] = _rms(x_ref[...], g_ref[...]).astype(BF16)
        acc_ref[...] = jnp.zeros_like(acc_ref)

    h = h_ref[...]
    g = jnp.dot(h, wg_ref[...], preferred_element_type=F32)
    u = jnp.dot(h, wu_ref[...], preferred_element_type=F32)
    a = (g * jax.nn.sigmoid(g) * u).astype(BF16)
    acc_ref[...] += jnp.dot(a, wd_ref[...], preferred_element_type=F32)

    @pl.when(f == pl.num_programs(1) - 1)
    def _():
        y_ref[...] = x_ref[...] + acc_ref[...]


def ffn_res(x, gain, w_gu, w_down, tm, tf=512):
    m, d = x.shape
    dff = w_down.shape[0]
    nf = dff // tf
    return pl.pallas_call(
        _ffn_kernel,
        grid=(m // tm, nf),
        in_specs=[pl.BlockSpec((tm, d), lambda i, f: (i, 0)),
                  pl.BlockSpec((1, d), lambda i, f: (0, 0)),
                  pl.BlockSpec((d, tf), lambda i, f: (0, f)),
                  pl.BlockSpec((d, tf), lambda i, f: (0, f + nf)),
                  pl.BlockSpec((tf, d), lambda i, f: (f, 0))],
        out_specs=pl.BlockSpec((tm, d), lambda i, f: (i, 0)),
        out_shape=jax.ShapeDtypeStruct((m, d), F32),
        scratch_shapes=[pltpu.VMEM((tm, d), BF16), pltpu.VMEM((tm, d), F32)],
        compiler_params=_params(("parallel", "arbitrary")),
        name="ffn_res",
    )(x, gain.reshape(1, d), w_gu, w_gu, w_down)


def _gelu_tanh(y):
    return 0.5 * y * (1.0 + jnp.tanh(math.sqrt(2.0 / math.pi) * (y + 0.044715 * (y * y * y))))


def _glu_kernel(x_ref, y_ref, wa_ref, wb_ref, o_ref, g_ref):
    @pl.when(pl.program_id(1) == 0)
    def _():
        g_ref[...] = _gelu_tanh(y_ref[...]).astype(BF16)

    g = g_ref[...]
    a = jnp.dot(g, wa_ref[...], preferred_element_type=F32)
    b = jnp.dot(g, wb_ref[...], preferred_element_type=F32)
    o_ref[...] = x_ref[...] + a * jax.nn.sigmoid(b)


def glu_res(x, y, w_glu, tm, tn=512):
    m, d = x.shape
    nj = d // tn
    return pl.pallas_call(
        _glu_kernel,
        grid=(m // tm, nj),
        in_specs=[pl.BlockSpec((tm, tn), lambda i, j: (i, j)),
                  pl.BlockSpec((tm, d), lambda i, j: (i, 0)),
                  pl.BlockSpec((d, tn), lambda i, j: (0, j)),
                  pl.BlockSpec((d, tn), lambda i, j: (0, j + nj))],
        out_specs=pl.BlockSpec((tm, tn), lambda i, j: (i, j)),
        out_shape=jax.ShapeDtypeStruct((m, d), F32),
        scratch_shapes=[pltpu.VMEM((tm, d), BF16)],
        compiler_params=_params(("parallel", "arbitrary")),
        name="glu_res",
    )(x, y, w_glu, w_glu)


def _rope_tables(pos, width, period):
    rot = period // 4
    half = rot // 2
    inv = ROPE_THETA ** (-jnp.arange(half, dtype=F32) / half)
    ang = pos.astype(F32)[:, None] * inv[None, :]
    cos, sin = jnp.cos(ang), jnp.sin(ang)
    n = pos.shape[0]
    ones = jnp.ones((n, period - rot), F32)
    zeros = jnp.zeros((n, period - rot), F32)
    zh = jnp.zeros((n, half), F32)
    c = jnp.concatenate([cos, cos, ones], axis=1)
    s_up = jnp.concatenate([zh, sin, zeros], axis=1)
    s_dn = jnp.concatenate([-sin, zh, zeros], axis=1)
    reps = width // period
    return tuple(jnp.tile(a, (1, reps)) for a in (c, s_up, s_dn))


def _rope(x, cos, s_up, s_dn, half):
    w = x.shape[-1]
    return x * cos + pltpu.roll(x, half, 1) * s_up + pltpu.roll(x, w - half, 1) * s_dn


def _dsa_post_kernel(p_ref, qg_ref, kg_ref, ikg_ref, c1_ref, u1_ref, d1_ref, c2_ref, u2_ref, d2_ref,
                     q_ref, k_ref, v_ref, ik_ref, kb_ref, vb_ref, ikb_ref, iq_ref, iw_ref):
    c1, u1, d1 = c1_ref[...], u1_ref[...], d1_ref[...]
    c2, u2, d2 = c2_ref[...], u2_ref[...], d2_ref[...]
    qg, kg = qg_ref[...], kg_ref[...]
    scale = HEAD_DIM ** -0.5
    for h in range(N_HEADS):
        x = p_ref[:, h * HEAD_DIM:(h + 1) * HEAD_DIM]
        y = _rope(_rms(x, qg), c1, u1, d1, HEAD_DIM // 8)
        q_ref[:, h * HEAD_DIM:(h + 1) * HEAD_DIM] = (y * scale).astype(BF16)
    k0 = N_HEADS * HEAD_DIM
    v0 = k0 + N_KV * HEAD_DIM
    for h in range(N_KV):
        x = p_ref[:, k0 + h * HEAD_DIM:k0 + (h + 1) * HEAD_DIM]
        y = _rope(_rms(x, kg), c1, u1, d1, HEAD_DIM // 8)
        k_ref[:, h * HEAD_DIM:(h + 1) * HEAD_DIM] = y
        kb_ref[:, h * HEAD_DIM:(h + 1) * HEAD_DIM] = y.astype(BF16)
    v = p_ref[:, v0:v0 + N_KV * HEAD_DIM]
    v_ref[...] = v
    vb_ref[...] = v.astype(BF16)
    i0 = v0 + N_KV * HEAD_DIM
    for h in range(N_IDX_HEADS * IDX_DIM // LANES):
        x = p_ref[:, i0 + h * LANES:i0 + (h + 1) * LANES]
        iq_ref[:, h * LANES:(h + 1) * LANES] = _rope(x, c2, u2, d2, IDX_DIM // 8).astype(BF16)
    j0 = i0 + N_IDX_HEADS * IDX_DIM
    xk = _rms(p_ref[:, j0:j0 + IDX_DIM], ikg_ref[...])
    xk2 = jnp.concatenate([xk, xk], axis=1)
    ik = _rope(xk2, c2, u2, d2, IDX_DIM // 8)[:, :IDX_DIM]
    ik_ref[...] = ik
    ikb_ref[...] = ik.astype(BF16)
    iw_ref[...] = p_ref[:, j0 + IDX_DIM:j0 + IDX_DIM + N_IDX_HEADS] * (N_IDX_HEADS ** -0.5 * IDX_DIM ** -0.5)


def dsa_post(proj, pos, q_gain, k_gain, ik_gain, tm):
    m, n = proj.shape
    npos = pos.shape[0]
    t1 = _rope_tables(pos, HEAD_DIM, HEAD_DIM)
    t2 = _rope_tables(pos, LANES, IDX_DIM)
    nb = npos // tm
    row = lambda i: (i, 0)
    tab = lambda i: (i % nb, 0)
    const = lambda i: (0, 0)
    nkv = N_KV * HEAD_DIM
    outs = [((m, N_HEADS * HEAD_DIM), BF16), ((m, nkv), F32), ((m, nkv), F32), ((m, IDX_DIM), F32),
            ((m, nkv), BF16), ((m, nkv), BF16), ((m, IDX_DIM), BF16),
            ((m, N_IDX_HEADS * IDX_DIM), BF16), ((m, N_IDX_HEADS), F32)]
    return pl.pallas_call(
        _dsa_post_kernel,
        grid=(m // tm,),
        in_specs=[pl.BlockSpec((tm, n), row),
                  pl.BlockSpec((1, HEAD_DIM), const), pl.BlockSpec((1, HEAD_DIM), const),
                  pl.BlockSpec((1, IDX_DIM), const)]
                 + [pl.BlockSpec((tm, LANES), tab)] * 6,
        out_specs=[pl.BlockSpec((tm, s[1]), row) for s, _ in outs],
        out_shape=[jax.ShapeDtypeStruct(s, dt) for s, dt in outs],
        compiler_params=_params(("parallel",)),
        name="dsa_post",
    )(proj, q_gain.reshape(1, -1), k_gain.reshape(1, -1), ik_gain.reshape(1, -1), *t1, *t2)


def _log_sigmoid(x):
    return jnp.minimum(x, 0.0) - jnp.log1p(jnp.exp(-jnp.abs(x)))


def _fox_post_kernel(p_ref, qg_ref, kg_ref, bf_ref, q_ref, k_ref, v_ref, kb_ref, vb_ref, lf_ref):
    qg, kg = qg_ref[...], kg_ref[...]
    scale = HEAD_DIM ** -0.5
    for h in range(N_HEADS):
        x = p_ref[:, h * HEAD_DIM:(h + 1) * HEAD_DIM]
        q_ref[:, h * HEAD_DIM:(h + 1) * HEAD_DIM] = (_rms(x, qg) * scale).astype(BF16)
    k0 = N_HEADS * HEAD_DIM
    v0 = k0 + N_KV * HEAD_DIM
    for h in range(N_KV):
        y = _rms(p_ref[:, k0 + h * HEAD_DIM:k0 + (h + 1) * HEAD_DIM], kg)
        k_ref[:, h * HEAD_DIM:(h + 1) * HEAD_DIM] = y
        kb_ref[:, h * HEAD_DIM:(h + 1) * HEAD_DIM] = y.astype(BF16)
    v = p_ref[:, v0:v0 + N_KV * HEAD_DIM]
    v_ref[...] = v
    vb_ref[...] = v.astype(BF16)
    f0 = v0 + N_KV * HEAD_DIM
    lf_ref[...] = _log_sigmoid(p_ref[:, f0:f0 + N_HEADS] + bf_ref[...])


def fox_post(proj, b_f, q_gain, k_gain, tm):
    m, n = proj.shape
    row = lambda i: (i, 0)
    const = lambda i: (0, 0)
    nkv = N_KV * HEAD_DIM
    outs = [((m, N_HEADS * HEAD_DIM), BF16), ((m, nkv), F32), ((m, nkv), F32),
            ((m, nkv), BF16), ((m, nkv), BF16), ((m, N_HEADS), F32)]
    return pl.pallas_call(
        _fox_post_kernel,
        grid=(m // tm,),
        in_specs=[pl.BlockSpec((tm, n), row), pl.BlockSpec((1, HEAD_DIM), const),
                  pl.BlockSpec((1, HEAD_DIM), const), pl.BlockSpec((1, N_HEADS), const)],
        out_specs=[pl.BlockSpec((tm, s[1]), row) for s, _ in outs],
        out_shape=[jax.ShapeDtypeStruct(s, dt) for s, dt in outs],
        compiler_params=_params(("parallel",)),
        name="fox_post",
    )(proj, q_gain.reshape(1, -1), k_gain.reshape(1, -1), b_f.reshape(1, -1))


def _softmax_step(s, vc, m_run, l_run, acc):
    m_new = jnp.maximum(m_run, jnp.max(s, axis=-1, keepdims=True))
    alpha = jnp.exp(m_run - m_new)
    p = jnp.exp(s - m_new)
    l_new = alpha * l_run + jnp.sum(p, axis=-1, keepdims=True)
    acc_new = alpha * acc + jnp.dot(p.astype(BF16), vc, preferred_element_type=F32)
    return m_new, l_new, acc_new


def _group_queries(q_ref, j):
    return jnp.concatenate(
        [q_ref[0, :, (GROUP * j + g) * HEAD_DIM:(GROUP * j + g + 1) * HEAD_DIM] for g in range(GROUP)], axis=0)


def _store_group(o_ref, j, o, tq):
    for g in range(GROUP):
        o_ref[0, :, (GROUP * j + g) * HEAD_DIM:(GROUP * j + g + 1) * HEAD_DIM] = (
            o[g * tq:(g + 1) * tq].astype(o_ref.dtype))


def _cumsum_kernel(lf_ref, c_ref):
    t = lf_ref.shape[2]
    r = lax.broadcasted_iota(I32, (LANES, LANES), 0)
    c = lax.broadcasted_iota(I32, (LANES, LANES), 1)
    upper = (r <= c).astype(F32)
    carry = jnp.zeros((lf_ref.shape[1], 1), F32)
    for b in range(t // LANES):
        blk = lf_ref[0, :, b * LANES:(b + 1) * LANES]
        cs = jnp.dot(blk, upper, precision=HI, preferred_element_type=F32) + carry
        c_ref[0, :, b * LANES:(b + 1) * LANES] = cs
        carry = cs[:, LANES - 1:LANES]


def cumsum_lanes(lf_t):
    b, h, t = lf_t.shape
    return pl.pallas_call(
        _cumsum_kernel,
        grid=(b,),
        in_specs=[pl.BlockSpec((1, h, t), lambda i: (i, 0, 0))],
        out_specs=pl.BlockSpec((1, h, t), lambda i: (i, 0, 0)),
        out_shape=jax.ShapeDtypeStruct((b, h, t), F32),
        compiler_params=_params(("parallel",)),
        name="cumsum_lanes",
    )(lf_t)


def _fox_kernel(q_ref, k_ref, v_ref, c_ref, o_ref, *, tq):
    i = pl.program_id(1)
    rows = GROUP * tq
    tri = (lax.broadcasted_iota(I32, (tq, tq), 1) <= lax.broadcasted_iota(I32, (tq, tq), 0))
    tri_bias = jnp.where(tri, 0.0, NEG)
    for j in range(N_KV):
        qj = _group_queries(q_ref, j)

        def logits(c, qj=qj, j=j):
            start = pl.multiple_of(c * tq, tq)
            kc = k_ref[0, pl.ds(start, tq), j * HEAD_DIM:(j + 1) * HEAD_DIM]
            vc = v_ref[0, pl.ds(start, tq), j * HEAD_DIM:(j + 1) * HEAD_DIM]
            s = lax.dot_general(qj, kc, NT, preferred_element_type=F32)
            ck = c_ref[0, c, GROUP * j:GROUP * (j + 1), :]
            return s.reshape(GROUP, tq, tq) - ck[:, None, :], vc

        def body(c, carry):
            s, vc = logits(c)
            return _softmax_step(s.reshape(rows, tq), vc, *carry)

        init = (jnp.full((rows, 1), NEG, F32), jnp.zeros((rows, 1), F32), jnp.zeros((rows, HEAD_DIM), F32))
        carry = lax.fori_loop(0, i, body, init)
        s, vc = logits(i)
        s = s + tri_bias[None]
        _, l_run, acc = _softmax_step(s.reshape(rows, tq), vc, *carry)
        _store_group(o_ref, j, acc / l_run, tq)


def fox_attention(q, kb, vb, c_t, tq=128):
    b, t, _ = q.shape
    nkv = N_KV * HEAD_DIM
    c_blk = c_t.reshape(b, N_HEADS, t // tq, tq).transpose(0, 2, 1, 3)
    return pl.pallas_call(
        functools.partial(_fox_kernel, tq=tq),
        grid=(b, t // tq),
        in_specs=[pl.BlockSpec((1, tq, N_HEADS * HEAD_DIM), lambda bi, i: (bi, i, 0)),
                  pl.BlockSpec((1, t, nkv), lambda bi, i: (bi, 0, 0)),
                  pl.BlockSpec((1, t, nkv), lambda bi, i: (bi, 0, 0)),
                  pl.BlockSpec((1, t // tq, N_HEADS, tq), lambda bi, i: (bi, 0, 0, 0))],
        out_specs=pl.BlockSpec((1, tq, N_HEADS * HEAD_DIM), lambda bi, i: (bi, i, 0)),
        out_shape=jax.ShapeDtypeStruct((b, t, N_HEADS * HEAD_DIM), BF16),
        compiler_params=_params(("parallel", "arbitrary")),
        name="fox_attention",
    )(q, kb, vb, c_blk)


def _order_key(score):
    score = jnp.where(score == 0.0, 0.0, score)
    bits = lax.bitcast_convert_type(score, I32)
    return jnp.where(bits < 0, bits ^ 0x7FFFFFFF, bits)


def _select_threshold(count_ge, nsel, shape):
    def bit_body(bi, tau_b):
        bit = jnp.left_shift(jnp.int32(1), 31 - bi)
        cand_b = tau_b | bit
        cnt = count_ge(cand_b ^ INT_MIN)
        return jnp.where(cnt >= nsel, cand_b, tau_b)

    tau_b = lax.fori_loop(0, 32, bit_body, jnp.zeros(shape, I32))
    return tau_b ^ INT_MIN


def _select_tie_index(count_eq_below, need, nbits, shape):
    def bit_body(bi, m):
        cand = m | jnp.left_shift(jnp.int32(1), nbits - 1 - bi)
        return jnp.where(count_eq_below(cand) < need, cand, m)

    return lax.fori_loop(0, nbits, bit_body, jnp.zeros(shape, I32))


def _dsa_kernel(iq_ref, iw_ref, ik_ref, q_ref, k_ref, v_ref, o_ref, keys_ref, bias_ref, *, tq, kc, nsel, t_total):
    i = pl.program_id(1)
    nch = (i * tq + tq + kc - 1) // kc
    nl = kc // LANES
    w = iw_ref[0]
    iq = iq_ref[0, 0]
    qpos = i * tq + lax.broadcasted_iota(I32, (tq, kc), 0)
    kiota = lax.broadcasted_iota(I32, (tq, kc), 1)

    def score_body(c, _):
        ikc = ik_ref[0, pl.ds(pl.multiple_of(c * kc, kc), kc), :]
        y = lax.dot_general(iq, ikc, NT, preferred_element_type=F32)
        acc = jnp.zeros((tq, kc), F32)
        for h in range(N_IDX_HEADS):
            acc = acc + w[:, h:h + 1] * jnp.maximum(y[h * tq:(h + 1) * tq], 0.0)
        key = jnp.where(kiota + c * kc <= qpos, _order_key(acc), INT_MIN)
        keys_ref[c] = key
        return 0

    lax.fori_loop(0, nch, score_body, 0)

    def lane_fold(m):
        out = m[:, 0:LANES]
        for l in range(1, nl):
            out = out + m[:, l * LANES:(l + 1) * LANES]
        return out

    def count(pred):
        def body(c, acc):
            return acc + lane_fold(jnp.where(pred(keys_ref[c], c), 1.0, 0.0))
        acc = lax.fori_loop(0, nch, body, jnp.zeros((tq, LANES), F32))
        return jnp.sum(acc, axis=1, keepdims=True)

    tau = _select_threshold(lambda cand: count(lambda kb, c: kb >= cand), float(nsel), (tq, 1))
    n_ge = count(lambda kb, c: kb >= tau)
    n_gt = count(lambda kb, c: kb > tau)
    need = float(nsel) - n_gt
    nbits = max(1, int(t_total - 1).bit_length())

    def tie_search():
        return _select_tie_index(
            lambda cand: count(lambda kb, c: (kb == tau) & (kiota + c * kc < cand)), need, nbits, (tq, 1))

    m_star = lax.cond(jnp.max(n_ge) > float(nsel), tie_search, lambda: jnp.full((tq, 1), t_total, I32))

    def bias_body(c, _):
        kb = keys_ref[c]
        idx = kiota + c * kc
        sel = ((kb > tau) | ((kb == tau) & (idx <= m_star))) & (idx <= qpos)
        bias_ref[c] = jnp.where(sel, 0.0, NEG)
        return 0

    lax.fori_loop(0, nch, bias_body, 0)

    rows = GROUP * tq
    for j in range(N_KV):
        qj = _group_queries(q_ref, j)

        def body(c, carry, qj=qj, j=j):
            start = pl.multiple_of(c * kc, kc)
            kcb = k_ref[0, pl.ds(start, kc), j * HEAD_DIM:(j + 1) * HEAD_DIM]
            vcb = v_ref[0, pl.ds(start, kc), j * HEAD_DIM:(j + 1) * HEAD_DIM]
            s = lax.dot_general(qj, kcb, NT, preferred_element_type=F32)
            s = s.reshape(GROUP, tq, kc) + bias_ref[c][None]
            return _softmax_step(s.reshape(rows, kc), vcb, *carry)

        init = (jnp.full((rows, 1), NEG, F32), jnp.zeros((rows, 1), F32), jnp.zeros((rows, HEAD_DIM), F32))
        _, l_run, acc = lax.fori_loop(0, nch, body, init)
        _store_group(o_ref, j, acc / l_run, tq)


def dsa_attention(iq_s, iw, ikb, q, kb, vb, tq=128, kc=256):
    b, t, _ = q.shape
    nsel = min(TOPK_MAX, t // 4)
    nkv = N_KV * HEAD_DIM
    nch = t // kc
    kern = functools.partial(_dsa_kernel, tq=tq, kc=kc, nsel=nsel, t_total=t)
    return pl.pallas_call(
        kern,
        grid=(b, t // tq),
        in_specs=[pl.BlockSpec((1, 1, N_IDX_HEADS * tq, IDX_DIM), lambda bi, i: (bi, i, 0, 0)),
                  pl.BlockSpec((1, tq, N_IDX_HEADS), lambda bi, i: (bi, i, 0)),
                  pl.BlockSpec((1, t, IDX_DIM), lambda bi, i: (bi, 0, 0)),
                  pl.BlockSpec((1, tq, N_HEADS * HEAD_DIM), lambda bi, i: (bi, i, 0)),
                  pl.BlockSpec((1, t, nkv), lambda bi, i: (bi, 0, 0)),
                  pl.BlockSpec((1, t, nkv), lambda bi, i: (bi, 0, 0))],
        out_specs=pl.BlockSpec((1, tq, N_HEADS * HEAD_DIM), lambda bi, i: (bi, i, 0)),
        out_shape=jax.ShapeDtypeStruct((b, t, N_HEADS * HEAD_DIM), BF16),
        scratch_shapes=[pltpu.VMEM((nch, tq, kc), I32), pltpu.VMEM((nch, tq, kc), F32)],
        compiler_params=_params(("parallel", "arbitrary")),
        name="dsa_attention",
    )(iq_s, iw, ikb, q, kb, vb)


def _dsa_sample_index_kernel(pt_ref, iq_ref, iw_ref, ik_ref, ikn_ref, bias_ref, biasn_ref, keys_ref, *, nsel):
    p = pl.program_id(1)
    npages = pl.num_programs(1)
    iq = iq_ref[0].astype(BF16)
    w = iw_ref[0]
    y = lax.dot_general(iq, ik_ref[0].astype(BF16), NT, preferred_element_type=F32)
    sc = jnp.sum(w * jnp.maximum(y, 0.0), axis=0, keepdims=True)
    keys_ref[pl.ds(p, 1), :] = _order_key(sc)

    @pl.when(p == npages - 1)
    def _():
        prod = iq.astype(F32) * ikn_ref[0].astype(BF16).astype(F32)
        yn = jnp.sum(prod, axis=1, keepdims=True)
        kn = _order_key(jnp.sum(w * jnp.maximum(yn, 0.0), axis=0, keepdims=True))
        keys = keys_ref[...]
        idx = (lax.broadcasted_iota(I32, keys.shape, 0) * PAGE + lax.broadcasted_iota(I32, keys.shape, 1))

        def total(x):
            return jnp.sum(jnp.sum(x, axis=1, keepdims=True), axis=0, keepdims=True)

        def count_ge(cand):
            return total(jnp.where(keys >= cand, 1.0, 0.0)) + jnp.where(kn >= cand, 1.0, 0.0)

        tau = _select_threshold(count_ge, float(nsel), (1, 1))
        n_gt = total(jnp.where(keys > tau, 1.0, 0.0)) + jnp.where(kn > tau, 1.0, 0.0)
        need = float(nsel) - n_gt
        nbits = int(keys.shape[0] * PAGE).bit_length()
        m_star = _select_tie_index(
            lambda cand: total(jnp.where((keys == tau) & (idx < cand), 1.0, 0.0)), need, nbits, (1, 1))
        n_eq = total(jnp.where(keys == tau, 1.0, 0.0))
        sel = (keys > tau) | ((keys == tau) & (idx <= m_star))
        bias_ref[0] = jnp.where(sel, 0.0, NEG)
        sel_n = (kn > tau) | ((kn == tau) & (n_eq < need))
        biasn_ref[0] = jnp.broadcast_to(jnp.where(sel_n, 0.0, NEG), (1, LANES))


def dsa_sample_index(page_table, iq, iw, pool_ik, ik_new):
    b, npages = page_table.shape
    nsel = min(TOPK_MAX, (npages * PAGE + 1) // 4)
    grid_spec = pltpu.PrefetchScalarGridSpec(
        num_scalar_prefetch=1,
        grid=(b, npages),
        in_specs=[pl.BlockSpec((1, N_IDX_HEADS, IDX_DIM), lambda bi, p, pt: (bi, 0, 0)),
                  pl.BlockSpec((1, N_IDX_HEADS, 1), lambda bi, p, pt: (bi, 0, 0)),
                  pl.BlockSpec((1, PAGE, IDX_DIM), lambda bi, p, pt: (pt[bi, p], 0, 0)),
                  pl.BlockSpec((1, 1, IDX_DIM), lambda bi, p, pt: (bi, 0, 0))],
        out_specs=[pl.BlockSpec((1, npages, PAGE), lambda bi, p, pt: (bi, 0, 0)),
                   pl.BlockSpec((1, 1, LANES), lambda bi, p, pt: (bi, 0, 0))],
        scratch_shapes=[pltpu.VMEM((npages, PAGE), I32)])
    return pl.pallas_call(
        functools.partial(_dsa_sample_index_kernel, nsel=nsel),
        grid_spec=grid_spec,
        out_shape=[jax.ShapeDtypeStruct((b, npages, PAGE), F32), jax.ShapeDtypeStruct((b, 1, LANES), F32)],
        compiler_params=_params(("parallel", "arbitrary")),
        name="dsa_sample_index",
    )(page_table, iq, iw, pool_ik, ik_new)


def _fox_sample_bias_kernel(pt_ref, lf_ref, lfn_ref, bias_ref, carry_ref):
    p = pl.program_id(1)

    @pl.when(p == 0)
    def _():
        carry_ref[...] = lfn_ref[0]

    r = lax.broadcasted_iota(I32, (PAGE, PAGE), 0)
    c = lax.broadcasted_iota(I32, (PAGE, PAGE), 1)
    after = (r > c).astype(F32)
    lf = lf_ref[0]
    carry = carry_ref[...]
    bias_ref[0, 0] = jnp.dot(lf, after, precision=HI, preferred_element_type=F32) + carry
    carry_ref[...] = carry + jnp.sum(lf, axis=1, keepdims=True)


def fox_sample_bias(page_table, pool_lf_t, lf_new):
    b, npages = page_table.shape
    h = pool_lf_t.shape[1]
    grid_spec = pltpu.PrefetchScalarGridSpec(
        num_scalar_prefetch=1,
        grid=(b, npages),
        in_specs=[pl.BlockSpec((1, h, PAGE), lambda bi, p, pt: (pt[bi, npages - 1 - p], 0, 0)),
                  pl.BlockSpec((1, h, 1), lambda bi, p, pt: (bi, 0, 0))],
        out_specs=pl.BlockSpec((1, 1, h, PAGE), lambda bi, p, pt: (bi, npages - 1 - p, 0, 0)),
        scratch_shapes=[pltpu.VMEM((h, 1), F32)])
    return pl.pallas_call(
        _fox_sample_bias_kernel,
        grid_spec=grid_spec,
        out_shape=jax.ShapeDtypeStruct((b, npages, h, PAGE), F32),
        compiler_params=_params(("parallel", "arbitrary")),
        name="fox_sample_bias",
    )(page_table, pool_lf_t, lf_new)


def _decode_kernel(pt_ref, q_ref, k_ref, v_ref, b_ref, kn_ref, vn_ref, bn_ref, o_ref, m_ref, l_ref, acc_ref):
    p = pl.program_id(1)
    npages = pl.num_programs(1)

    @pl.when(p == 0)
    def _():
        m_ref[...] = jnp.full_like(m_ref, NEG)
        l_ref[...] = jnp.zeros_like(l_ref)
        acc_ref[...] = jnp.zeros_like(acc_ref)

    q = q_ref[0]
    s = lax.dot_general(q, k_ref[0].astype(BF16), NT, preferred_element_type=F32) + b_ref[0, 0]
    m_new, l_new, acc_new = _softmax_step(s, v_ref[0].astype(BF16), m_ref[...], l_ref[...], acc_ref[...])
    m_ref[...], l_ref[...], acc_ref[...] = m_new, l_new, acc_new

    @pl.when(p == npages - 1)
    def _():
        kn = kn_ref[0].astype(BF16).astype(F32)
        s_n = jnp.sum(q.astype(F32) * kn, axis=1, keepdims=True) + bn_ref[0]
        m_run, l_run, acc = m_ref[...], l_ref[...], acc_ref[...]
        m_fin = jnp.maximum(m_run, s_n)
        alpha = jnp.exp(m_run - m_fin)
        p_n = jnp.exp(s_n - m_fin)
        l_fin = alpha * l_run + p_n
        acc = alpha * acc + p_n.astype(BF16).astype(F32) * vn_ref[0].astype(BF16).astype(F32)
        head_kv = lax.broadcasted_iota(I32, (N_HEADS, HEAD_DIM), 0) // GROUP
        o = jnp.zeros((N_HEADS, HEAD_DIM), F32)
        for j in range(N_KV):
            o = o + jnp.where(head_kv == j, acc[:, j * HEAD_DIM:(j + 1) * HEAD_DIM], 0.0)
        o_ref[0] = o / l_fin


def paged_decode_attention(page_table, q_bd, pool_k, pool_v, bias, k_new, v_new, bias_new):
    b, npages = page_table.shape
    nkv = N_KV * HEAD_DIM
    per_b = lambda bi, p, pt: (bi, 0, 0)
    grid_spec = pltpu.PrefetchScalarGridSpec(
        num_scalar_prefetch=1,
        grid=(b, npages),
        in_specs=[pl.BlockSpec((1, N_HEADS, nkv), per_b),
                  pl.BlockSpec((1, PAGE, nkv), lambda bi, p, pt: (pt[bi, p], 0, 0)),
                  pl.BlockSpec((1, PAGE, nkv), lambda bi, p, pt: (pt[bi, p], 0, 0)),
                  pl.BlockSpec((1, 1, N_HEADS, PAGE), lambda bi, p, pt: (bi, p, 0, 0)),
                  pl.BlockSpec((1, 1, nkv), per_b),
                  pl.BlockSpec((1, 1, nkv), per_b),
                  pl.BlockSpec((1, N_HEADS, 1), per_b)],
        out_specs=pl.BlockSpec((1, N_HEADS, HEAD_DIM), per_b),
        scratch_shapes=[pltpu.VMEM((N_HEADS, 1), F32), pltpu.VMEM((N_HEADS, 1), F32),
                        pltpu.VMEM((N_HEADS, nkv), F32)])
    return pl.pallas_call(
        _decode_kernel,
        grid_spec=grid_spec,
        out_shape=jax.ShapeDtypeStruct((b, N_HEADS, HEAD_DIM), F32),
        compiler_params=_params(("parallel", "arbitrary")),
        name="paged_decode_attention",
    )(page_table, q_bd, pool_k, pool_v, bias, k_new, v_new, bias_new)


def _block_diag_queries(q):
    b = q.shape[0]
    qh = q.reshape(b, N_HEADS, 1, HEAD_DIM)
    own = (jnp.arange(N_HEADS)[:, None] // GROUP == jnp.arange(N_KV)[None, :])[None, :, :, None]
    return jnp.where(own, qh, jnp.zeros_like(qh)).reshape(b, N_HEADS, N_KV * HEAD_DIM)


def _ssm_tables(a_re, a_im, log_dt, b_re, b_im, c_re, c_im, d_skip):
    lam = lax.complex(a_re, a_im)
    dt = jnp.exp(log_dt)[:, None]
    a_bar = jnp.exp(lam * dt)
    bs = lax.complex(b_re, b_im) * ((a_bar - 1.0) / lam)[:, :, None]
    cc = lax.complex(c_re, c_im)
    L = SSM_CHUNK
    pows = [jnp.ones_like(a_bar)]
    for _ in range(L):
        pows.append(pows[-1] * a_bar)
    pw = jnp.stack(pows)
    ca = cc[None] * pw[:L, :, None, :]
    kd = (jnp.einsum('dgcn,gne->dgce', jnp.real(ca), jnp.real(bs), precision=HI)
          - jnp.einsum('dgcn,gne->dgce', jnp.imag(ca), jnp.imag(bs), precision=HI))
    g = a_re.shape[0]
    cdim = SSM_GROUP
    dmat = jnp.eye(cdim, dtype=F32)[None] * d_skip.reshape(g, cdim)[:, :, None]
    kd = kd.at[0].add(dmat)
    ii = jnp.arange(L)
    diff = ii[None, :] - ii[:, None]
    kt = kd[jnp.clip(diff, 0, L - 1)]
    kt = jnp.where((diff >= 0)[:, :, None, None, None], kt, 0.0)
    m = kt.transpose(2, 0, 4, 1, 3).reshape(g, L * cdim, L * cdim)
    pj = pw[L - 1 - ii][:, :, None, :] * bs.transpose(0, 2, 1)[None]
    pj = pj.transpose(1, 0, 2, 3).reshape(g, L * cdim, -1)
    p = jnp.concatenate([jnp.real(pj), jnp.imag(pj)], axis=-1)
    cp = cc[None] * pw[1:L + 1][:, :, None, :]
    cp = cp.transpose(1, 3, 0, 2).reshape(g, -1, L * cdim)
    q = jnp.concatenate([jnp.real(cp), -jnp.imag(cp)], axis=1)
    al = pw[L]
    return m, p, q, jnp.real(al), jnp.imag(al), a_bar, bs, cc


def _ssm_kernel(u_ref, m_ref, p_ref, q_ref, ar_ref, ai_ref, y_ref, hf_ref, *, nb):
    u = u_ref[0]
    r = u.shape[0]
    nk = r // nb
    n = SSM_STATE
    h = jnp.dot(u, p_ref[0], precision=HI, preferred_element_type=F32)
    row = lax.broadcasted_iota(I32, (r, 2 * n), 0) % nk
    lane = lax.broadcasted_iota(I32, (1, 2 * n), 1)
    dr = jnp.concatenate([ar_ref[0], ar_ref[0]], axis=1)
    di = jnp.concatenate([-ai_ref[0], ai_ref[0]], axis=1)
    shift = 1
    while shift < nk:
        hs = jnp.where(row >= shift, pltpu.roll(h, shift, 0), 0.0)
        h = h + dr * hs + di * pltpu.roll(hs, n, 1)
        re, im = dr, jnp.where(lane < n, -di, di)
        re2, im2 = re * re - im * im, 2.0 * re * im
        dr, di = re2, jnp.where(lane < n, -im2, im2)
        shift *= 2
    hprev = jnp.where(row >= 1, pltpu.roll(h, 1, 0), 0.0)
    y_ref[0] = (jnp.dot(u, m_ref[0], precision=HI, preferred_element_type=F32)
                + jnp.dot(hprev, q_ref[0], precision=HI, preferred_element_type=F32))
    for b in range(nb):
        hf_ref[0, b:b + 1, :] = h[(b + 1) * nk - 1:(b + 1) * nk, :]


def ssm_prompt(u_g, m, p, q, al_re, al_im, nb):
    g, r, w = u_g.shape
    n2 = p.shape[2]
    gmap = lambda i: (i, 0, 0)
    return pl.pallas_call(
        functools.partial(_ssm_kernel, nb=nb),
        grid=(g,),
        in_specs=[pl.BlockSpec((1, r, w), gmap), pl.BlockSpec((1, w, w), gmap), pl.BlockSpec((1, w, n2), gmap),
                  pl.BlockSpec((1, n2, w), gmap), pl.BlockSpec((1, 1, n2 // 2), gmap),
                  pl.BlockSpec((1, 1, n2 // 2), gmap)],
        out_specs=[pl.BlockSpec((1, r, w), gmap), pl.BlockSpec((1, nb, n2), gmap)],
        out_shape=[jax.ShapeDtypeStruct((g, r, w), F32), jax.ShapeDtypeStruct((g, nb, n2), F32)],
        compiler_params=_params(("parallel",)),
        name="ssm_prompt",
    )(u_g, m, p, q, al_re.reshape(g, 1, -1), al_im.reshape(g, 1, -1))


def _ssm_step_kernel(u_ref, hr_ref, hi_ref, ar_ref, ai_ref, br_ref, bi_ref, cr_ref, ci_ref, d_ref,
                     y_ref, or_ref, oi_ref):
    u = u_ref[0]
    bur = jnp.zeros(hr_ref.shape[1:], F32)
    bui = jnp.zeros(hr_ref.shape[1:], F32)
    for c in range(SSM_GROUP):
        uc = u[:, c:c + 1]
        bur = bur + br_ref[c] * uc
        bui = bui + bi_ref[c] * uc
    ar, ai = ar_ref[...], ai_ref[...]
    h0r, h0i = hr_ref[0], hi_ref[0]
    hr = ar * h0r - ai * h0i + bur
    hi = ar * h0i + ai * h0r + bui
    or_ref[0] = hr
    oi_ref[0] = hi
    lane = lax.broadcasted_iota(I32, u.shape, 1)
    y = d_ref[...] * u
    for c in range(SSM_GROUP):
        col = jnp.sum(cr_ref[c] * hr - ci_ref[c] * hi, axis=1, keepdims=True)
        y = y + jnp.where(lane == c, col, 0.0)
    y_ref[0] = y


def ssm_step(u, h_re, h_im, a_bar, bs, cc, d_skip):
    b, g, c = u.shape
    n = h_re.shape[2]
    per_b = lambda i: (i, 0, 0)
    c3 = lambda i: (0, 0, 0)
    c2 = lambda i: (0, 0)
    bsr = jnp.real(bs).transpose(2, 0, 1)
    bsi = jnp.imag(bs).transpose(2, 0, 1)
    ccr = jnp.real(cc).transpose(1, 0, 2)
    cci = jnp.imag(cc).transpose(1, 0, 2)
    return pl.pallas_call(
        _ssm_step_kernel,
        grid=(b,),
        in_specs=[pl.BlockSpec((1, g, c), per_b), pl.BlockSpec((1, g, n), per_b), pl.BlockSpec((1, g, n), per_b),
                  pl.BlockSpec((g, n), c2), pl.BlockSpec((g, n), c2),
                  pl.BlockSpec((c, g, n), c3), pl.BlockSpec((c, g, n), c3),
                  pl.BlockSpec((c, g, n), c3), pl.BlockSpec((c, g, n), c3),
                  pl.BlockSpec((g, c), c2)],
        out_specs=[pl.BlockSpec((1, g, c), per_b), pl.BlockSpec((1, g, n), per_b), pl.BlockSpec((1, g, n), per_b)],
        out_shape=[jax.ShapeDtypeStruct((b, g, c), F32), jax.ShapeDtypeStruct((b, g, n), F32),
                   jax.ShapeDtypeStruct((b, g, n), F32)],
        compiler_params=_params(("parallel",)),
        name="ssm_step",
    )(u, h_re, h_im, jnp.real(a_bar), jnp.imag(a_bar), bsr, bsi, ccr, cci, d_skip.reshape(g, c))


def _dsa_layer(xp, xs, gain, pool_k, pool_v, pool_ik, page_table, w_in, w_out, q_gain, k_gain, ik_gain,
               bp, t, past_len, tm):
    mp, ms = xp.shape[0], xs.shape[0]
    nkv = N_KV * HEAD_DIM
    proj = norm_proj(xp, gain, w_in, tm)
    q, k, v, ik, kb, vb, ikb, iq, iw = dsa_post(proj, jnp.arange(t), q_gain, k_gain, ik_gain, min(tm, 256))
    tq = 128
    nq = t // tq
    iq_s = iq.reshape(bp, nq, tq, N_IDX_HEADS, IDX_DIM).transpose(0, 1, 3, 2, 4).reshape(
        bp, nq, N_IDX_HEADS * tq, IDX_DIM)
    o = dsa_attention(iq_s, iw.reshape(bp, t, -1), ikb.reshape(bp, t, -1), q.reshape(bp, t, -1),
                      kb.reshape(bp, t, -1), vb.reshape(bp, t, -1), tq=tq, kc=min(256, t))
    xp = out_res(xp, o.reshape(mp, -1), w_out, tm)
    projs = norm_proj(xs, gain, w_in, ms)
    pos_s = jnp.full((ms,), past_len, I32)
    qs, ks, vs, iks, _, _, _, iqs, iws = dsa_post(projs, pos_s, q_gain, k_gain, ik_gain, ms)
    bias, bias_n = dsa_sample_index(page_table, iqs.reshape(ms, N_IDX_HEADS, IDX_DIM),
                                    iws.reshape(ms, N_IDX_HEADS, 1), pool_ik, iks.reshape(ms, 1, IDX_DIM))
    npages = page_table.shape[1]
    bias_h = jnp.broadcast_to(bias[:, :, None, :], (ms, npages, N_HEADS, PAGE))
    bias_nh = jnp.broadcast_to(bias_n[:, :, :1], (ms, N_HEADS, 1))
    os_ = paged_decode_attention(page_table, _block_diag_queries(qs), pool_k.reshape(-1, PAGE, nkv),
                                 pool_v.reshape(-1, PAGE, nkv), bias_h, ks.reshape(ms, 1, nkv),
                                 vs.reshape(ms, 1, nkv), bias_nh)
    xs = out_res(xs, os_.reshape(ms, -1), w_out, ms)
    return xp, xs, (k, v, ik), (ks, vs, iks)


def _fox_layer(xp, xs, gain, pool_k, pool_v, pool_lf, page_table, w_in, b_f, w_out, q_gain, k_gain, bp, t, tm):
    mp, ms = xp.shape[0], xs.shape[0]
    nkv = N_KV * HEAD_DIM
    proj = norm_proj(xp, gain, w_in, tm)
    q, k, v, kb, vb, lf = fox_post(proj, b_f, q_gain, k_gain, min(tm, 256))
    c_t = cumsum_lanes(lf.reshape(bp, t, N_HEADS).transpose(0, 2, 1))
    o = fox_attention(q.reshape(bp, t, -1), kb.reshape(bp, t, -1), vb.reshape(bp, t, -1), c_t)
    xp = out_res(xp, o.reshape(mp, -1), w_out, tm)
    projs = norm_proj(xs, gain, w_in, ms)
    qs, ks, vs, _, _, lfs = fox_post(projs, b_f, q_gain, k_gain, ms)
    bias = fox_sample_bias(page_table, pool_lf.transpose(0, 2, 1), lfs.reshape(ms, N_HEADS, 1))
    os_ = paged_decode_attention(page_table, _block_diag_queries(qs), pool_k.reshape(-1, PAGE, nkv),
                                 pool_v.reshape(-1, PAGE, nkv), bias, ks.reshape(ms, 1, nkv),
                                 vs.reshape(ms, 1, nkv), jnp.zeros((ms, N_HEADS, 1), F32))
    xs = out_res(xs, os_.reshape(ms, -1), w_out, ms)
    return xp, xs, (k, v, lf), (ks, vs, lfs)


def _rmsnorm_kernel(x_ref, g_ref, o_ref):
    o_ref[...] = _rms(x_ref[...], g_ref[...])


def rmsnorm(x, gain, tm):
    m, d = x.shape
    return pl.pallas_call(
        _rmsnorm_kernel,
        grid=(m // tm,),
        in_specs=[pl.BlockSpec((tm, d), lambda i: (i, 0)), pl.BlockSpec((1, d), lambda i: (0, 0))],
        out_specs=pl.BlockSpec((tm, d), lambda i: (i, 0)),
        out_shape=jax.ShapeDtypeStruct((m, d), F32),
        compiler_params=_params(("parallel",)),
        name="rmsnorm",
    )(x, gain.reshape(1, d))


def _ssm_layer(xp, xs, gain, h0_re, h0_im, a_re, a_im, log_dt, b_re, b_im, c_re, c_im, d_skip, w_glu, bp, t, tm):
    mp, ms = xp.shape[0], xs.shape[0]
    g, n, cdim, L = N_SSM_GROUPS, SSM_STATE, SSM_GROUP, SSM_CHUNK
    m, p, q, al_re, al_im, a_bar, bs, cc = _ssm_tables(a_re, a_im, log_dt, b_re, b_im, c_re, c_im, d_skip)
    u = rmsnorm(xp, gain, tm)
    u_g = u.reshape(mp // L, L, g, cdim).transpose(2, 0, 1, 3).reshape(g, mp // L, L * cdim)
    y_g, hf = ssm_prompt(u_g, m, p, q, al_re, al_im, bp)
    y = y_g.reshape(g, mp // L, L, cdim).transpose(1, 2, 0, 3).reshape(mp, g * cdim)
    xp = glu_res(xp, y, w_glu, tm)
    hf = hf.transpose(1, 0, 2)
    us = rmsnorm(xs, gain, ms)
    ys, hs_re, hs_im = ssm_step(us.reshape(ms, g, cdim), h0_re, h0_im, a_bar, bs, cc, d_skip)
    xs = glu_res(xs, ys.reshape(ms, g * cdim), w_glu, ms)
    return xp, xs, (hf[..., :n], hf[..., n:]), (hs_re, hs_im)


def kernel(x_prompt, x_sample, cache_dsa_k, cache_dsa_v, cache_dsa_idx_k, cache_fox_k, cache_fox_v, cache_fox_logf, state_ssm_re, state_ssm_im, page_table, norm_mix, norm_ffn, dsa_w_in, dsa_w_out, dsa_q_gain, dsa_k_gain, dsa_ik_gain, fox_w_in, fox_b_f, fox_w_out, fox_q_gain, fox_k_gain, ssm_a_re, ssm_a_im, ssm_log_dt, ssm_b_re, ssm_b_im, ssm_c_re, ssm_c_im, ssm_d, ssm_w_glu, ffn_w_gu, ffn_w_down):
    bp, t, d = x_prompt.shape
    bs_, ts, _ = x_sample.shape
    assert ts == 1, "the sample group carries one new position per sequence"
    depth = norm_mix.shape[0]
    past_len = page_table.shape[1] * PAGE
    tm = min(512, bp * t)
    xp = x_prompt.reshape(bp * t, d)
    xs = x_sample.reshape(bs_ * ts, d)
    bf = lambda w: w.astype(BF16)
    dsa_p, dsa_s, fox_p, fox_s, ssm_p, ssm_s = [], [], [], [], [], []
    for i in range(depth):
        kind, j = i % N_MIXERS, i // N_MIXERS
        if kind == 0:
            xp, xs, outp, outs = _dsa_layer(
                xp, xs, norm_mix[i], cache_dsa_k[j], cache_dsa_v[j], cache_dsa_idx_k[j], page_table,
                bf(dsa_w_in[j]), bf(dsa_w_out[j]), dsa_q_gain[j], dsa_k_gain[j], dsa_ik_gain[j], bp, t, past_len, tm)
            dsa_p.append(outp); dsa_s.append(outs)
        elif kind == 1:
            xp, xs, outp, outs = _fox_layer(
                xp, xs, norm_mix[i], cache_fox_k[j], cache_fox_v[j], cache_fox_logf[j], page_table,
                bf(fox_w_in[j]), fox_b_f[j], bf(fox_w_out[j]), fox_q_gain[j], fox_k_gain[j], bp, t, tm)
            fox_p.append(outp); fox_s.append(outs)
        else:
            xp, xs, outp, outs = _ssm_layer(
                xp, xs, norm_mix[i], state_ssm_re[j], state_ssm_im[j], ssm_a_re[j], ssm_a_im[j], ssm_log_dt[j],
                ssm_b_re[j], ssm_b_im[j], ssm_c_re[j], ssm_c_im[j], ssm_d[j], bf(ssm_w_glu[j]), bp, t, tm)
            ssm_p.append(outp); ssm_s.append(outs)
        w_gu, w_down = bf(ffn_w_gu[i]), bf(ffn_w_down[i])
        xp = ffn_res(xp, norm_ffn[i], w_gu, w_down, tm)
        xs = ffn_res(xs, norm_ffn[i], w_gu, w_down, bs_ * ts)

    def stack(items, idx, shape):
        return jnp.stack([it[idx].reshape(shape) for it in items])

    kv_p, kv_s = (bp, t, N_KV, HEAD_DIM), (bs_, ts, N_KV, HEAD_DIM)
    st_p, st_s = (bp, N_SSM_GROUPS, SSM_STATE), (bs_, N_SSM_GROUPS, SSM_STATE)
    return (xp.reshape(bp, t, d), xs.reshape(bs_, ts, d),
            stack(dsa_p, 0, kv_p), stack(dsa_p, 1, kv_p), stack(dsa_p, 2, (bp, t, IDX_DIM)),
            stack(dsa_s, 0, kv_s), stack(dsa_s, 1, kv_s), stack(dsa_s, 2, (bs_, ts, IDX_DIM)),
            stack(fox_p, 0, kv_p), stack(fox_p, 1, kv_p), stack(fox_p, 2, (bp, t, N_HEADS)),
            stack(fox_s, 0, kv_s), stack(fox_s, 1, kv_s), stack(fox_s, 2, (bs_, ts, N_HEADS)),
            stack(ssm_p, 0, st_p), stack(ssm_p, 1, st_p), stack(ssm_s, 0, st_s), stack(ssm_s, 1, st_s))
```

```python
import functools
import math

import jax
import jax.numpy as jnp
from jax import lax
from jax.experimental import pallas as pl
from jax.experimental.pallas import tpu as pltpu

F32 = jnp.float32
BF16 = jnp.bfloat16
I32 = jnp.int32

D_MODEL = 2048
N_HEADS = 16
HEAD_DIM = 128
N_KV = 4
GROUP = N_HEADS // N_KV
N_IDX_HEADS = 16
IDX_DIM = 64
TOPK_MAX = 256
ROPE_THETA = 500000.0
PAGE = 128
SSM_GROUP = 16
N_SSM_GROUPS = D_MODEL // SSM_GROUP
SSM_STATE = 64
SSM_CHUNK = 16
EPS = 1e-6
N_MIXERS = 3

LANES = 128
SUBLANES = 8
TQ = 128
KC = 256
DECODE_PAGES = 8
LOG2E = math.log2(math.e)
NEG = -1e30
BISECT_MAX = 64
HI = lax.Precision.HIGHEST
VMEM_LIMIT = 56 * 1024 * 1024

NT = (((1,), (1,)), ((), ()))


def _params(sem):
    return pltpu.CompilerParams(dimension_semantics=sem, vmem_limit_bytes=VMEM_LIMIT)


def _rms(x, gain):
    return x * lax.rsqrt(jnp.mean(x * x, axis=-1, keepdims=True) + EPS) * gain


def _proj_kernel(x_ref, g_ref, w_ref, o_ref, h_ref):
    @pl.when(pl.program_id(1) == 0)
    def _():
        h_ref[...] = _rms(x_ref[...], g_ref[...]).astype(BF16)

    o_ref[...] = jnp.dot(h_ref[...], w_ref[...], preferred_element_type=F32)


def _pick_tn(n, cap=1536):
    k = -(-n // cap)
    return -(-n // (k * LANES)) * LANES


def norm_proj(x, gain, w, tm):
    m, d = x.shape
    n = w.shape[1]
    tn = _pick_tn(n)
    return pl.pallas_call(
        _proj_kernel,
        grid=(m // tm, pl.cdiv(n, tn)),
        in_specs=[pl.BlockSpec((tm, d), lambda i, j: (i, 0)),
                  pl.BlockSpec((1, d), lambda i, j: (0, 0)),
                  pl.BlockSpec((d, tn), lambda i, j: (0, j))],
        out_specs=pl.BlockSpec((tm, tn), lambda i, j: (i, j)),
        out_shape=jax.ShapeDtypeStruct((m, n), F32),
        scratch_shapes=[pltpu.VMEM((tm, d), BF16)],
        compiler_params=_params(("parallel", "arbitrary")),
        name="norm_proj",
    )(x, gain.reshape(1, d), w)


def _out_res_kernel(x_ref, o_ref, w_ref, y_ref):
    y_ref[...] = x_ref[...] + jnp.dot(o_ref[...].astype(BF16), w_ref[...], preferred_element_type=F32)


def out_res(x, o, w, tm):
    m, d = x.shape
    k = w.shape[0]
    return pl.pallas_call(
        _out_res_kernel,
        grid=(m // tm,),
        in_specs=[pl.BlockSpec((tm, d), lambda i: (i, 0)),
                  pl.BlockSpec((tm, k), lambda i: (i, 0)),
                  pl.BlockSpec((k, d), lambda i: (0, 0))],
        out_specs=pl.BlockSpec((tm, d), lambda i: (i, 0)),
        out_shape=jax.ShapeDtypeStruct((m, d), F32),
        compiler_params=_params(("parallel",)),
        name="out_res",
    )(x, o, w)


def _ffn_kernel(x_ref, g_ref, wg_ref, wu_ref, wd_ref, y_ref, h_ref, acc_ref):
    f = pl.program_id(1)

    @pl.when(f == 0)
    def _():
        h_ref[...] = _rms(x_ref[...], g_ref[...]).astype(BF16)
        acc_ref[...] = jnp.zeros_like(acc_ref)

    h = h_ref[...]
    g = jnp.dot(h, wg_ref[...], preferred_element_type=F32)
    u = jnp.dot(h, wu_ref[...], preferred_element_type=F32)
    a = (g * jax.nn.sigmoid(g) * u).astype(BF16)
    acc_ref[...] += jnp.dot(a, wd_ref[...], preferred_element_type=F32)

    @pl.when(f == pl.num_programs(1) - 1)
    def _():
        y_ref[...] = x_ref[...] + acc_ref[...]


def ffn_res(x, gain, w_gu, w_down, tm, tf=512):
    m, d = x.shape
    dff = w_down.shape[0]
    nf = dff // tf
    return pl.pallas_call(
        _ffn_kernel,
        grid=(m // tm, nf),
        in_specs=[pl.BlockSpec((tm, d), lambda i, f: (i, 0)),
                  pl.BlockSpec((1, d), lambda i, f: (0, 0)),
                  pl.BlockSpec((d, tf), lambda i, f: (0, f)),
                  pl.BlockSpec((d, tf), lambda i, f: (0, f + nf)),
                  pl.BlockSpec((tf, d), lambda i, f: (f, 0))],
        out_specs=pl.BlockSpec((tm, d), lambda i, f: (i, 0)),
        out_shape=jax.ShapeDtypeStruct((m, d), F32),
        scratch_shapes=[pltpu.VMEM((tm, d), BF16), pltpu.VMEM((tm, d), F32)],
        compiler_params=_params(("parallel", "arbitrary")),
        name="ffn_res",
    )(x, gain.reshape(1, d), w_gu, w_gu, w_down)


def _gelu_tanh(y):
    return 0.5 * y * (1.0 + jnp.tanh(math.sqrt(2.0 / math.pi) * (y + 0.044715 * (y * y * y))))


def _glu_kernel(x_ref, y_ref, wa_ref, wb_ref, o_ref, g_ref):
    @pl.when(pl.program_id(1) == 0)
    def _():
        g_ref[...] = _gelu_tanh(y_ref[...]).astype(BF16)

    g = g_ref[...]
    a = jnp.dot(g, wa_ref[...], preferred_element_type=F32)
    b = jnp.dot(g, wb_ref[...], preferred_element_type=F32)
    o_ref[...] = x_ref[...] + a * jax.nn.sigmoid(b)


def glu_res(x, y, w_glu, tm, tn=512):
    m, d = x.shape
    nj = d // tn
    return pl.pallas_call(
        _glu_kernel,
        grid=(m // tm, nj),
        in_specs=[pl.BlockSpec((tm, tn), lambda i, j: (i, j)),
                  pl.BlockSpec((tm, d), lambda i, j: (i, 0)),
                  pl.BlockSpec((d, tn), lambda i, j: (0, j)),
                  pl.BlockSpec((d, tn), lambda i, j: (0, j + nj))],
        out_specs=pl.BlockSpec((tm, tn), lambda i, j: (i, j)),
        out_shape=jax.ShapeDtypeStruct((m, d), F32),
        scratch_shapes=[pltpu.VMEM((tm, d), BF16)],
        compiler_params=_params(("parallel", "arbitrary")),
        name="glu_res",
    )(x, y, w_glu, w_glu)


def _rmsnorm_kernel(x_ref, g_ref, o_ref):
    o_ref[...] = _rms(x_ref[...], g_ref[...])


def rmsnorm(x, gain, tm):
    m, d = x.shape
    return pl.pallas_call(
        _rmsnorm_kernel,
        grid=(m // tm,),
        in_specs=[pl.BlockSpec((tm, d), lambda i: (i, 0)), pl.BlockSpec((1, d), lambda i: (0, 0))],
        out_specs=pl.BlockSpec((tm, d), lambda i: (i, 0)),
        out_shape=jax.ShapeDtypeStruct((m, d), F32),
        compiler_params=_params(("parallel",)),
        name="rmsnorm",
    )(x, gain.reshape(1, d))


def _rope_tables(pos, width, period):
    rot = period // 4
    half = rot // 2
    inv = ROPE_THETA ** (-jnp.arange(half, dtype=F32) / half)
    ang = pos.astype(F32)[:, None] * inv[None, :]
    cos, sin = jnp.cos(ang), jnp.sin(ang)
    n = pos.shape[0]
    ones = jnp.ones((n, period - rot), F32)
    zeros = jnp.zeros((n, period - rot), F32)
    zh = jnp.zeros((n, half), F32)
    c = jnp.concatenate([cos, cos, ones], axis=1)
    s_up = jnp.concatenate([zh, sin, zeros], axis=1)
    s_dn = jnp.concatenate([-sin, zh, zeros], axis=1)
    reps = width // period
    return tuple(jnp.tile(a, (1, reps)) for a in (c, s_up, s_dn))


def _rope(x, cos, s_up, s_dn, half):
    w = x.shape[-1]
    return x * cos + pltpu.roll(x, half, 1) * s_up + pltpu.roll(x, w - half, 1) * s_dn


def _store_kv_heads(p_ref, k0, v0, k_fn, k_ref, v_ref, kh_ref, vt_ref):
    for h in range(N_KV):
        y = k_fn(p_ref[:, k0 + h * HEAD_DIM:k0 + (h + 1) * HEAD_DIM])
        k_ref[:, h * HEAD_DIM:(h + 1) * HEAD_DIM] = y
        if kh_ref is not None:
            kh_ref[h] = y.astype(BF16)
            vh = p_ref[:, v0 + h * HEAD_DIM:v0 + (h + 1) * HEAD_DIM]
            vt_ref[h, 0] = vh.T.astype(BF16)
    v_ref[...] = p_ref[:, v0:v0 + N_KV * HEAD_DIM]


def _dsa_post_kernel(p_ref, qg_ref, kg_ref, ikg_ref, c1_ref, u1_ref, d1_ref, c2_ref, u2_ref, d2_ref,
                     q_ref, k_ref, v_ref, ik_ref, iw_ref, *mode_refs, prompt):
    c1, u1, d1 = c1_ref[...], u1_ref[...], d1_ref[...]
    c2, u2, d2 = c2_ref[...], u2_ref[...], d2_ref[...]
    qg, kg = qg_ref[...], kg_ref[...]
    scale = HEAD_DIM ** -0.5 * LOG2E
    for h in range(N_HEADS):
        x = p_ref[:, h * HEAD_DIM:(h + 1) * HEAD_DIM]
        y = _rope(_rms(x, qg), c1, u1, d1, HEAD_DIM // 8)
        q_ref[:, h * HEAD_DIM:(h + 1) * HEAD_DIM] = (y * scale).astype(BF16)
    k0 = N_HEADS * HEAD_DIM
    v0 = k0 + N_KV * HEAD_DIM
    kh_ref, vt_ref, ikb_ref, iqt_ref = mode_refs if prompt else (None, None, None, None)
    iq_ref = None if prompt else mode_refs[0]
    _store_kv_heads(p_ref, k0, v0, lambda x: _rope(_rms(x, kg), c1, u1, d1, HEAD_DIM // 8),
                    k_ref, v_ref, kh_ref, vt_ref)
    i0 = v0 + N_KV * HEAD_DIM
    tm = p_ref.shape[0]
    for h in range(N_IDX_HEADS * IDX_DIM // LANES):
        x = _rope(p_ref[:, i0 + h * LANES:i0 + (h + 1) * LANES], c2, u2, d2, IDX_DIM // 8)
        if prompt:
            for qb in range(tm // TQ):
                xt = x[qb * TQ:(qb + 1) * TQ].T.astype(BF16)
                iqt_ref[qb, :, (2 * h) * TQ:(2 * h + 1) * TQ] = xt[:IDX_DIM]
                iqt_ref[qb, :, (2 * h + 1) * TQ:(2 * h + 2) * TQ] = xt[IDX_DIM:]
        else:
            iq_ref[:, h * LANES:(h + 1) * LANES] = x.astype(BF16)
    j0 = i0 + N_IDX_HEADS * IDX_DIM
    xk = _rms(p_ref[:, j0:j0 + IDX_DIM], ikg_ref[...])
    xk2 = jnp.concatenate([xk, xk], axis=1)
    ik = _rope(xk2, c2, u2, d2, IDX_DIM // 8)[:, :IDX_DIM]
    ik_ref[...] = ik
    if prompt:
        ikb_ref[...] = ik.astype(BF16)
    iw_ref[...] = p_ref[:, j0 + IDX_DIM:j0 + IDX_DIM + N_IDX_HEADS] * (N_IDX_HEADS ** -0.5 * IDX_DIM ** -0.5)


def dsa_post(proj, pos, q_gain, k_gain, ik_gain, tm, prompt):
    m, n = proj.shape
    npos = pos.shape[0]
    t1 = _rope_tables(pos, HEAD_DIM, HEAD_DIM)
    t2 = _rope_tables(pos, LANES, IDX_DIM)
    nb = npos // tm
    row = lambda i: (i, 0)
    tab = lambda i: (i % nb, 0)
    const = lambda i: (0, 0)
    nkv = N_KV * HEAD_DIM
    niq = N_IDX_HEADS * IDX_DIM
    outs = [((m, N_HEADS * HEAD_DIM), BF16, pl.BlockSpec((tm, N_HEADS * HEAD_DIM), row)),
            ((m, nkv), F32, pl.BlockSpec((tm, nkv), row)),
            ((m, nkv), F32, pl.BlockSpec((tm, nkv), row)),
            ((m, IDX_DIM), F32, pl.BlockSpec((tm, IDX_DIM), row)),
            ((m, N_IDX_HEADS), F32, pl.BlockSpec((tm, N_IDX_HEADS), row))]
    if not prompt:
        outs += [((m, niq), BF16, pl.BlockSpec((tm, niq), row))]
    if prompt:
        assert tm == KC and tm % TQ == 0
        outs += [((N_KV, m, HEAD_DIM), BF16, pl.BlockSpec((N_KV, tm, HEAD_DIM), lambda i: (0, i, 0))),
                 ((N_KV, m // KC, HEAD_DIM, KC), BF16, pl.BlockSpec((N_KV, 1, HEAD_DIM, KC), lambda i: (0, i, 0, 0))),
                 ((m, IDX_DIM), BF16, pl.BlockSpec((tm, IDX_DIM), row)),
                 ((m // TQ, IDX_DIM, N_IDX_HEADS * TQ), BF16,
                  pl.BlockSpec((tm // TQ, IDX_DIM, N_IDX_HEADS * TQ), lambda i: (i, 0, 0)))]
    return pl.pallas_call(
        functools.partial(_dsa_post_kernel, prompt=prompt),
        grid=(m // tm,),
        in_specs=[pl.BlockSpec((tm, n), row),
                  pl.BlockSpec((1, HEAD_DIM), const), pl.BlockSpec((1, HEAD_DIM), const),
                  pl.BlockSpec((1, IDX_DIM), const)]
                 + [pl.BlockSpec((tm, LANES), tab)] * 6,
        out_specs=[o[2] for o in outs],
        out_shape=[jax.ShapeDtypeStruct(o[0], o[1]) for o in outs],
        compiler_params=_params(("parallel",)),
        name="dsa_post",
    )(proj, q_gain.reshape(1, -1), k_gain.reshape(1, -1), ik_gain.reshape(1, -1), *t1, *t2)


def _log_sigmoid(x):
    return jnp.minimum(x, 0.0) - jnp.log1p(jnp.exp(-jnp.abs(x)))


def _fox_post_kernel(p_ref, qg_ref, kg_ref, bf_ref, q_ref, k_ref, v_ref, lf_ref, *prompt_refs, prompt):
    qg, kg = qg_ref[...], kg_ref[...]
    scale = HEAD_DIM ** -0.5 * LOG2E
    for h in range(N_HEADS):
        x = p_ref[:, h * HEAD_DIM:(h + 1) * HEAD_DIM]
        q_ref[:, h * HEAD_DIM:(h + 1) * HEAD_DIM] = (_rms(x, qg) * scale).astype(BF16)
    k0 = N_HEADS * HEAD_DIM
    v0 = k0 + N_KV * HEAD_DIM
    kh_ref, vt_ref = prompt_refs if prompt else (None, None)
    _store_kv_heads(p_ref, k0, v0, lambda x: _rms(x, kg), k_ref, v_ref, kh_ref, vt_ref)
    f0 = v0 + N_KV * HEAD_DIM
    lf_ref[...] = _log_sigmoid(p_ref[:, f0:f0 + N_HEADS] + bf_ref[...])


def fox_post(proj, b_f, q_gain, k_gain, tm, prompt):
    m, n = proj.shape
    row = lambda i: (i, 0)
    const = lambda i: (0, 0)
    nkv = N_KV * HEAD_DIM
    outs = [((m, N_HEADS * HEAD_DIM), BF16, pl.BlockSpec((tm, N_HEADS * HEAD_DIM), row)),
            ((m, nkv), F32, pl.BlockSpec((tm, nkv), row)),
            ((m, nkv), F32, pl.BlockSpec((tm, nkv), row)),
            ((m, N_HEADS), F32, pl.BlockSpec((tm, N_HEADS), row))]
    if prompt:
        assert tm == KC
        outs += [((N_KV, m, HEAD_DIM), BF16, pl.BlockSpec((N_KV, tm, HEAD_DIM), lambda i: (0, i, 0))),
                 ((N_KV, m // KC, HEAD_DIM, KC), BF16, pl.BlockSpec((N_KV, 1, HEAD_DIM, KC), lambda i: (0, i, 0, 0)))]
    return pl.pallas_call(
        functools.partial(_fox_post_kernel, prompt=prompt),
        grid=(m // tm,),
        in_specs=[pl.BlockSpec((tm, n), row), pl.BlockSpec((1, HEAD_DIM), const),
                  pl.BlockSpec((1, HEAD_DIM), const), pl.BlockSpec((1, N_HEADS), const)],
        out_specs=[o[2] for o in outs],
        out_shape=[jax.ShapeDtypeStruct(o[0], o[1]) for o in outs],
        compiler_params=_params(("parallel",)),
        name="fox_post",
    )(proj, q_gain.reshape(1, -1), k_gain.reshape(1, -1), b_f.reshape(1, -1))


def _queries_t(q_ref, j):
    cols = []
    for g in range(GROUP):
        x = q_ref[0, :, (GROUP * j + g) * HEAD_DIM:(GROUP * j + g + 1) * HEAD_DIM]
        cols.append(x.astype(F32).T.astype(BF16))
    return jnp.concatenate(cols, axis=1)


def _attend_heads(logits_fn, values_fn, m_ref, l_ref, acc_ref):
    s_next = logits_fn(0)
    pending = None
    for j in range(N_KV):
        s_t = s_next
        if j + 1 < N_KV:
            s_next = logits_fn(j + 1)
        m_run = m_ref[j]
        m_new = jnp.maximum(m_run, jnp.max(s_t, axis=0, keepdims=True))
        alpha = jnp.exp2(m_run - m_new)
        p = jnp.exp2(s_t - m_new)
        l_ref[j] = alpha * l_ref[j] + jnp.sum(p, axis=0, keepdims=True)
        m_ref[j] = m_new
        pv = jnp.dot(values_fn(j), p.astype(BF16), preferred_element_type=F32)
        if pending is not None:
            jp, alpha_p, pv_p = pending
            acc_ref[jp] = alpha_p * acc_ref[jp] + pv_p
        pending = (j, alpha, pv)
    jp, alpha_p, pv_p = pending
    acc_ref[jp] = alpha_p * acc_ref[jp] + pv_p


def _softmax_reset(m_ref, l_ref, acc_ref):
    m_ref[...] = jnp.full_like(m_ref, NEG)
    l_ref[...] = jnp.zeros_like(l_ref)
    acc_ref[...] = jnp.zeros_like(acc_ref)


def _softmax_scratch(qk_depth):
    rows = GROUP * TQ
    return [pltpu.VMEM((N_KV, qk_depth, rows), BF16), pltpu.VMEM((N_KV, 1, rows), F32),
            pltpu.VMEM((N_KV, 1, rows), F32), pltpu.VMEM((N_KV, HEAD_DIM, rows), F32)]


def _store_heads_t(o_ref, l_ref, acc_ref):
    for j in range(N_KV):
        o_t = acc_ref[j] / l_ref[j]
        for g in range(GROUP):
            o_ref[0, :, (GROUP * j + g) * HEAD_DIM:(GROUP * j + g + 1) * HEAD_DIM] = (
                o_t[:, g * TQ:(g + 1) * TQ].T.astype(o_ref.dtype))


def _cumsum_kernel(lf_ref, hi_ref, mid_ref, lo_ref):
    t = lf_ref.shape[1]
    r = lax.broadcasted_iota(I32, (LANES, LANES), 0)
    c = lax.broadcasted_iota(I32, (LANES, LANES), 1)
    lower = (c <= r).astype(F32)
    carry = jnp.zeros((1, lf_ref.shape[2]), F32)
    for b in range(t // LANES):
        rows = slice(b * LANES, (b + 1) * LANES)
        cs = jnp.dot(lower, lf_ref[0, rows, :], precision=HI, preferred_element_type=F32) + carry
        carry = cs[LANES - 1:LANES, :]
        x = cs * LOG2E
        hi = x.astype(BF16)
        r1 = x - hi.astype(F32)
        mid = r1.astype(BF16)
        hi_ref[0, rows, :] = hi
        mid_ref[0, rows, :] = mid
        lo_ref[0, rows, :] = (r1 - mid.astype(F32)).astype(BF16)


def cumsum_rows_split(lf):
    b, t, h = lf.shape
    spec = pl.BlockSpec((1, t, h), lambda i: (i, 0, 0))
    return pl.pallas_call(
        _cumsum_kernel,
        grid=(b,),
        in_specs=[spec],
        out_specs=[spec] * 3,
        out_shape=[jax.ShapeDtypeStruct((b, t, h), BF16)] * 3,
        compiler_params=_params(("parallel",)),
        name="cumsum_rows_split",
    )(lf)


def _fox_kernel(q_ref, k_ref, cp_ref, v_ref, o_ref, qt_ref, m_ref, l_ref, acc_ref):
    i = pl.program_id(1)
    rows = GROUP * TQ
    last = (i * TQ) // KC
    r_io = lax.broadcasted_iota(I32, (HEAD_DIM, rows), 0)
    g_io = lax.broadcasted_iota(I32, (HEAD_DIM, rows), 1) // TQ
    sel = jnp.where((r_io >= 3 * g_io) & (r_io < 3 * g_io + 3), -1.0, 0.0).astype(BF16)
    for j in range(N_KV):
        qt_ref[j] = jnp.concatenate([_queries_t(q_ref, j), sel], axis=0)
    _softmax_reset(m_ref, l_ref, acc_ref)

    def logits(j, c):
        start = pl.multiple_of(c * KC, KC)
        kc = jnp.concatenate([k_ref[j, 0, pl.ds(start, KC), :], cp_ref[0, j, pl.ds(start, KC), :]], axis=1)
        return jnp.dot(kc, qt_ref[j], preferred_element_type=F32)

    def body(c, _):
        _attend_heads(lambda j: logits(j, c), lambda j: v_ref[j, 0, c], m_ref, l_ref, acc_ref)
        return 0

    lax.fori_loop(0, last, body, 0)
    key_io = lax.broadcasted_iota(I32, (KC, rows), 0)
    q_io = lax.broadcasted_iota(I32, (KC, rows), 1) % TQ + i * TQ
    causal = key_io + last * KC <= q_io
    _attend_heads(lambda j: jnp.where(causal, logits(j, last), NEG), lambda j: v_ref[j, 0, last],
                  m_ref, l_ref, acc_ref)
    _store_heads_t(o_ref, l_ref, acc_ref)


def fox_attention(q, kh, cparts, vt):
    b, t, _ = q.shape
    return pl.pallas_call(
        _fox_kernel,
        grid=(b, t // TQ),
        in_specs=[pl.BlockSpec((1, TQ, N_HEADS * HEAD_DIM), lambda bi, i: (bi, i, 0)),
                  pl.BlockSpec((N_KV, 1, t, HEAD_DIM), lambda bi, i: (0, bi, 0, 0)),
                  pl.BlockSpec((1, N_KV, t, HEAD_DIM), lambda bi, i: (bi, 0, 0, 0)),
                  pl.BlockSpec((N_KV, 1, t // KC, HEAD_DIM, KC), lambda bi, i: (0, bi, 0, 0, 0))],
        out_specs=pl.BlockSpec((1, TQ, N_HEADS * HEAD_DIM), lambda bi, i: (bi, i, 0)),
        out_shape=jax.ShapeDtypeStruct((b, t, N_HEADS * HEAD_DIM), BF16),
        scratch_shapes=_softmax_scratch(2 * HEAD_DIM),
        compiler_params=_params(("parallel", "arbitrary")),
        name="fox_attention",
    )(q, kh, cparts, vt)


def _all_true(flags):
    return (jnp.min(flags.astype(F32)) > 0.5).astype(I32)


def _bisect_threshold(count_ge, lo, hi, cnt_lo, nsel):
    steps = 2

    def cond(state):
        it, _, _, _, settled, _ = state
        return (it < BISECT_MAX // steps) & (settled == 0)

    def body(state):
        it, lo, hi, cnt_lo, _, _ = state
        stuck = jnp.zeros(lo.shape, I32)
        for _ in range(steps):
            mid = lo + (hi - lo) * 0.5
            stuck = jnp.where((mid <= lo) | (mid >= hi), 1, 0)
            cnt = count_ge(mid)
            up = cnt >= nsel
            lo = jnp.where(up, mid, lo)
            cnt_lo = jnp.where(up, cnt, cnt_lo)
            hi = jnp.where(up, hi, mid)
        exact = jnp.where(cnt_lo == nsel, 1, 0)
        return it + 1, lo, hi, cnt_lo, _all_true(jnp.maximum(exact, stuck)), _all_true(exact)

    init = (jnp.int32(0), lo, hi, cnt_lo, jnp.int32(0), jnp.int32(0))
    _, lo, _, cnt_lo, _, all_exact = lax.while_loop(cond, body, init)
    return lo, cnt_lo, all_exact


def _exact_threshold(min_ge, count_gt_and_next, lo, nsel):
    def cond(state):
        it, _, _, _, _, finished = state
        return (it < BISECT_MAX) & (finished == 0)

    def body(state):
        it, v, tau, need, done, _ = state
        n_gt, v_next = count_gt_and_next(v)
        fin = (n_gt < nsel) & (done == 0)
        tau = jnp.where(fin, v, tau)
        need = jnp.where(fin, nsel - n_gt, need)
        done = jnp.where(fin, 1, done)
        v = jnp.where(done == 1, v, v_next)
        return it + 1, v, tau, need, done, _all_true(done)

    v0 = min_ge(lo)
    init = (jnp.int32(0), v0, v0, jnp.zeros(lo.shape, F32), jnp.zeros(lo.shape, I32), jnp.int32(0))
    _, _, tau, need, _, _ = lax.while_loop(cond, body, init)
    return tau, need


def _select_tie_index(count_eq_below, need, nbits, shape):
    def bit_body(bi, m):
        cand = m | jnp.left_shift(jnp.int32(1), nbits - 1 - bi)
        return jnp.where(count_eq_below(cand) < need, cand, m)

    return lax.fori_loop(0, nbits, bit_body, jnp.zeros(shape, I32))


def _dsa_kernel(iq_ref, iw_ref, ik_ref, q_ref, k_ref, v_ref, o_ref, sc_ref, bias_ref, qt_ref, m_ref, l_ref, acc_ref,
                *, nsel, t_total):
    i = pl.program_id(1)
    nch = (i * TQ + TQ + KC - 1) // KC
    w = iw_ref[0, 0]
    iq_t = iq_ref[0]
    key_io = lax.broadcasted_iota(I32, (KC, TQ), 0)
    qpos = i * TQ + lax.broadcasted_iota(I32, (KC, TQ), 1)
    inf = float("inf")

    def fold(x, op):
        part = op(x.reshape(4, KC // (4 * SUBLANES), SUBLANES, TQ), axis=1)
        return op(part, axis=0)

    def score_body(c, carry):
        lo, hi = carry
        ikc = ik_ref[0, pl.ds(pl.multiple_of(c * KC, KC), KC), :]
        y = jnp.dot(ikc, iq_t, preferred_element_type=F32)
        acc = jnp.zeros((KC, TQ), F32)
        for h in range(N_IDX_HEADS):
            acc = acc + w[h:h + 1, :] * jnp.maximum(y[:, h * TQ:(h + 1) * TQ], 0.0)
        causal = key_io + c * KC <= qpos
        sc_ref[c] = jnp.where(causal, acc, -inf)
        return (jnp.minimum(lo, fold(jnp.where(causal, acc, inf), jnp.min)),
                jnp.maximum(hi, fold(jnp.where(causal, acc, -inf), jnp.max)))

    lo8, hi8 = lax.fori_loop(0, nch, score_body,
                             (jnp.full((SUBLANES, TQ), inf, F32), jnp.full((SUBLANES, TQ), -inf, F32)))

    def count(pred):
        def body(c, acc):
            return acc + fold(jnp.where(pred(sc_ref[c], c), 1.0, 0.0), jnp.sum)
        acc = lax.fori_loop(0, nch, body, jnp.zeros((SUBLANES, TQ), F32))
        return jnp.sum(acc, axis=0, keepdims=True)

    def select():
        lo = jnp.min(lo8, axis=0, keepdims=True)
        hi = jnp.max(hi8, axis=0, keepdims=True)
        n_causal = (qpos[:1] + 1).astype(F32)
        lo, _, all_exact = _bisect_threshold(lambda thr: count(lambda s, c: s >= thr), lo, hi, n_causal, float(nsel))

        def ties():
            def min_ge(lo):
                def body(c, acc):
                    s = sc_ref[c]
                    return jnp.minimum(acc, fold(jnp.where(s >= lo, s, inf), jnp.min))
                acc = lax.fori_loop(0, nch, body, jnp.full((SUBLANES, TQ), inf, F32))
                return jnp.min(acc, axis=0, keepdims=True)

            def count_gt_and_next(v):
                def body(c, carry):
                    n, nxt = carry
                    s = sc_ref[c]
                    above = s > v
                    return (n + fold(jnp.where(above, 1.0, 0.0), jnp.sum),
                            jnp.minimum(nxt, fold(jnp.where(above, s, inf), jnp.min)))
                n, nxt = lax.fori_loop(0, nch, body, (jnp.zeros((SUBLANES, TQ), F32),
                                                      jnp.full((SUBLANES, TQ), inf, F32)))
                return jnp.sum(n, axis=0, keepdims=True), jnp.min(nxt, axis=0, keepdims=True)

            tau, need = _exact_threshold(min_ge, count_gt_and_next, lo, float(nsel))
            nbits = max(1, int(t_total - 1).bit_length())
            m_star = _select_tie_index(
                lambda cand: count(lambda s, c: (s == tau) & (key_io + c * KC < cand)), need, nbits, (1, TQ))
            return tau, m_star

        return lax.cond(all_exact == 1, lambda: (lo, jnp.full((1, TQ), t_total, I32)), ties)

    tau, m_star = lax.cond(i * TQ + TQ <= nsel,
                           lambda: (jnp.full((1, TQ), -inf, F32), jnp.full((1, TQ), t_total, I32)), select)

    def bias_body(c, _):
        s = sc_ref[c]
        idx = key_io + c * KC
        keep = ((s > tau) | ((s == tau) & (idx <= m_star))) & (idx <= qpos)
        bias_ref[c] = jnp.where(keep, 0.0, NEG)
        return 0

    lax.fori_loop(0, nch, bias_body, 0)

    for j in range(N_KV):
        qt_ref[j] = _queries_t(q_ref, j)
    _softmax_reset(m_ref, l_ref, acc_ref)

    def body(c, _):
        start = pl.multiple_of(c * KC, KC)
        bias = jnp.concatenate([bias_ref[c]] * GROUP, axis=1)
        _attend_heads(
            lambda j: jnp.dot(k_ref[j, 0, pl.ds(start, KC), :], qt_ref[j], preferred_element_type=F32) + bias,
            lambda j: v_ref[j, 0, c], m_ref, l_ref, acc_ref)
        return 0

    lax.fori_loop(0, nch, body, 0)
    _store_heads_t(o_ref, l_ref, acc_ref)


def dsa_attention(iq_t, iw_t, ikb, q, kh, vt):
    b, t, _ = q.shape
    nsel = min(TOPK_MAX, t // 4)
    nch = t // KC
    kern = functools.partial(_dsa_kernel, nsel=nsel, t_total=t)
    return pl.pallas_call(
        kern,
        grid=(b, t // TQ),
        in_specs=[pl.BlockSpec((1, IDX_DIM, N_IDX_HEADS * TQ), lambda bi, i: (bi, i, 0)),
                  pl.BlockSpec((1, 1, N_IDX_HEADS, TQ), lambda bi, i: (bi, i, 0, 0)),
                  pl.BlockSpec((1, t, IDX_DIM), lambda bi, i: (bi, 0, 0)),
                  pl.BlockSpec((1, TQ, N_HEADS * HEAD_DIM), lambda bi, i: (bi, i, 0)),
                  pl.BlockSpec((N_KV, 1, t, HEAD_DIM), lambda bi, i: (0, bi, 0, 0)),
                  pl.BlockSpec((N_KV, 1, nch, HEAD_DIM, KC), lambda bi, i: (0, bi, 0, 0, 0))],
        out_specs=pl.BlockSpec((1, TQ, N_HEADS * HEAD_DIM), lambda bi, i: (bi, i, 0)),
        out_shape=jax.ShapeDtypeStruct((b, t, N_HEADS * HEAD_DIM), BF16),
        scratch_shapes=[pltpu.VMEM((nch, KC, TQ), F32), pltpu.VMEM((nch, KC, TQ), F32)] + _softmax_scratch(HEAD_DIM),
        compiler_params=_params(("parallel", "arbitrary")),
        name="dsa_attention",
    )(iq_t, iw_t, ikb, q, kh, vt)


def _page_copy(pool_ref, layer, pt_ref, b, p, buf, slot, sem):
    return pltpu.make_async_copy(pool_ref.at[layer, pt_ref[b, p]], buf.at[slot, p], sem.at[slot])


def _fetch_sequence_pages(pool_ref, layer, pt_ref, buf, sem):
    b = pl.program_id(0)
    nb = pl.num_programs(0)
    npages = buf.shape[1]
    slot = b % 2

    def start_all(bb, sl):
        lax.fori_loop(0, npages, lambda p, _: (_page_copy(pool_ref, layer, pt_ref, bb, p, buf, sl, sem).start(), 0)[1], 0)

    @pl.when(b == 0)
    def _():
        start_all(0, 0)

    @pl.when(b + 1 < nb)
    def _():
        start_all(b + 1, 1 - slot)

    lax.fori_loop(0, npages, lambda p, _: (_page_copy(pool_ref, layer, pt_ref, b, p, buf, slot, sem).wait(), 0)[1], 0)
    return slot


def _dsa_sample_index_kernel(pt_ref, iq_ref, iw_ref, pool_ref, ikn_ref, bias_ref, biasn_ref, buf, sem, sc_ref,
                             *, layer, nsel):
    slot = _fetch_sequence_pages(pool_ref, layer, pt_ref, buf, sem)
    npages = buf.shape[1]
    iq = iq_ref[0]
    w = iw_ref[0]

    def score_body(p, _):
        y = jnp.dot(iq, buf[slot, p].astype(BF16), preferred_element_type=F32)
        sc_ref[pl.ds(p, 1), :] = jnp.sum(w * jnp.maximum(y, 0.0), axis=0, keepdims=True)
        return 0

    lax.fori_loop(0, npages, score_body, 0)
    prod = iq.astype(F32) * ikn_ref[0].astype(BF16).astype(F32)
    yn = jnp.sum(prod, axis=1, keepdims=True)
    sn = jnp.sum(w * jnp.maximum(yn, 0.0), axis=0, keepdims=True)
    sc = sc_ref[...]
    idx = lax.broadcasted_iota(I32, sc.shape, 0) * PAGE + lax.broadcasted_iota(I32, sc.shape, 1)
    inf = float("inf")

    def total(x, op=jnp.sum):
        return op(op(x, axis=0, keepdims=True), axis=1, keepdims=True)

    def count_ge(thr):
        return total(jnp.where(sc >= thr, 1.0, 0.0)) + jnp.where(sn >= thr, 1.0, 0.0)

    def min_ge(lo):
        return jnp.minimum(total(jnp.where(sc >= lo, sc, inf), jnp.min), jnp.where(sn >= lo, sn, inf))

    def count_gt_and_next(v):
        n = total(jnp.where(sc > v, 1.0, 0.0)) + jnp.where(sn > v, 1.0, 0.0)
        return n, jnp.minimum(total(jnp.where(sc > v, sc, inf), jnp.min), jnp.where(sn > v, sn, inf))

    lo = jnp.minimum(total(sc, jnp.min), sn)
    hi = jnp.maximum(total(sc, jnp.max), sn)
    lo, _, _ = _bisect_threshold(count_ge, lo, hi, jnp.full((1, 1), float(npages * PAGE + 1), F32), float(nsel))
    tau, need = _exact_threshold(min_ge, count_gt_and_next, lo, float(nsel))
    nbits = int(npages * PAGE).bit_length()
    m_star = _select_tie_index(
        lambda cand: total(jnp.where((sc == tau) & (idx < cand), 1.0, 0.0)), need, nbits, (1, 1))
    n_eq = total(jnp.where(sc == tau, 1.0, 0.0))
    keep = (sc > tau) | ((sc == tau) & (idx <= m_star))
    keep_rows = jnp.dot(jnp.where(keep, 1.0, 0.0).astype(BF16), _row_expansion().astype(BF16),
                        preferred_element_type=F32)
    bias_ref[0] = jnp.where(keep_rows > 0.5, 0.0, NEG)
    keep_n = (sn > tau) | ((sn == tau) & (n_eq < need))
    biasn_ref[0] = jnp.broadcast_to(jnp.where(keep_n, 0.0, NEG), (1, LANES))


def dsa_sample_index(page_table, iq, iw, pool_ik_t, layer, ik_new):
    b, npages = page_table.shape
    nsel = min(TOPK_MAX, (npages * PAGE + 1) // 4)
    per_b = lambda bi, pt: (bi, 0, 0)
    grid_spec = pltpu.PrefetchScalarGridSpec(
        num_scalar_prefetch=1,
        grid=(b,),
        in_specs=[pl.BlockSpec((1, N_IDX_HEADS, IDX_DIM), per_b),
                  pl.BlockSpec((1, N_IDX_HEADS, 1), per_b),
                  pl.BlockSpec(memory_space=pl.ANY),
                  pl.BlockSpec((1, 1, IDX_DIM), per_b)],
        out_specs=[pl.BlockSpec((1, npages, PAGE * N_KV), per_b), pl.BlockSpec((1, 1, LANES), per_b)],
        scratch_shapes=[pltpu.VMEM((2, npages, IDX_DIM, PAGE), F32), pltpu.SemaphoreType.DMA((2,)),
                        pltpu.VMEM((npages, PAGE), F32)])
    return pl.pallas_call(
        functools.partial(_dsa_sample_index_kernel, layer=layer, nsel=nsel),
        grid_spec=grid_spec,
        out_shape=[jax.ShapeDtypeStruct((b, npages, PAGE * N_KV), F32), jax.ShapeDtypeStruct((b, 1, LANES), F32)],
        compiler_params=_params(("arbitrary",)),
        name="dsa_sample_index",
    )(page_table, iq, iw, pool_ik_t, ik_new)


def _fox_sample_bias_kernel(pt_ref, pool_ref, lfn_ref, bias_ref, buf, sem, *, layer):
    slot = _fetch_sequence_pages(pool_ref, layer, pt_ref, buf, sem)
    npages, h = buf.shape[1], buf.shape[2]
    rows = npages * h
    x = buf[slot].reshape(rows, PAGE)
    r = lax.broadcasted_iota(I32, (PAGE, PAGE), 0)
    c = lax.broadcasted_iota(I32, (PAGE, PAGE), 1)
    within = jnp.dot(x, (r > c).astype(F32), precision=HI, preferred_element_type=F32)
    tot = jnp.dot(x, jnp.ones((PAGE, PAGE), F32), precision=HI, preferred_element_type=F32)
    row = lax.broadcasted_iota(I32, (rows, PAGE), 0)
    suffix = tot
    shift = h
    while shift < rows:
        suffix = suffix + jnp.where(row + shift < rows, pltpu.roll(suffix, rows - shift, 0), 0.0)
        shift *= 2
    per_tok = within + (suffix - tot)
    per_row = jnp.dot(per_tok, _row_expansion().astype(F32), precision=HI, preferred_element_type=F32)
    bias_ref[0] = (per_row.reshape(npages, h, PAGE * N_KV) + lfn_ref[0][None]) * LOG2E


def fox_sample_bias(page_table, pool_lf_t, layer, lf_new):
    b, npages = page_table.shape
    h = pool_lf_t.shape[2]
    grid_spec = pltpu.PrefetchScalarGridSpec(
        num_scalar_prefetch=1,
        grid=(b,),
        in_specs=[pl.BlockSpec(memory_space=pl.ANY),
                  pl.BlockSpec((1, h, 1), lambda bi, pt: (bi, 0, 0))],
        out_specs=pl.BlockSpec((1, npages, h, PAGE * N_KV), lambda bi, pt: (bi, 0, 0, 0)),
        scratch_shapes=[pltpu.VMEM((2, npages, h, PAGE), F32), pltpu.SemaphoreType.DMA((2,))])
    return pl.pallas_call(
        functools.partial(_fox_sample_bias_kernel, layer=layer),
        grid_spec=grid_spec,
        out_shape=jax.ShapeDtypeStruct((b, npages, h, PAGE * N_KV), F32),
        compiler_params=_params(("arbitrary",)),
        name="fox_sample_bias",
    )(page_table, pool_lf_t, lf_new)


def _decode_kernel(pt_ref, q_ref, pk_ref, pv_ref, b_ref, kn_ref, vn_ref, bn_ref, o_ref, kbuf, vbuf, sem,
                   *, layer, per_head_bias):
    b = pl.program_id(0)
    npages = b_ref.shape[1]
    pg = kbuf.shape[1]
    ngroups = npages // pg
    rows = PAGE * N_KV
    q = q_ref[0]
    row_kv = lax.broadcasted_iota(I32, (N_HEADS, rows), 1) % N_KV
    head_mask = jnp.where(row_kv == lax.broadcasted_iota(I32, (N_HEADS, rows), 0) // GROUP, 0.0, NEG)

    def copies(g, slot):
        out = []
        for i in range(pg):
            page = pt_ref[b, g * pg + i]
            out.append(pltpu.make_async_copy(pk_ref.at[layer, page], kbuf.at[slot, i], sem.at[0, slot]))
            out.append(pltpu.make_async_copy(pv_ref.at[layer, page], vbuf.at[slot, i], sem.at[1, slot]))
        return out

    for cp in copies(0, 0):
        cp.start()

    def body(g, carry):
        m_run, l_run, acc = carry
        slot = g % 2

        @pl.when(g + 1 < ngroups)
        def _():
            for cp in copies(g + 1, 1 - slot):
                cp.start()

        for cp in copies(g, slot):
            cp.wait()
        logits = []
        for i in range(pg):
            s = lax.dot_general(q, kbuf[slot, i].astype(BF16), NT, preferred_element_type=F32)
            page_bias = b_ref[0, g * pg + i] if per_head_bias else b_ref[0, pl.ds(g * pg + i, 1), :]
            logits.append(s + page_bias + head_mask)
        s_all = jnp.concatenate(logits, axis=1)
        m_new = jnp.maximum(m_run, jnp.max(s_all, axis=1, keepdims=True))
        alpha = jnp.exp2(m_run - m_new)
        p = jnp.exp2(s_all - m_new)
        l_new = alpha * l_run + jnp.sum(p, axis=1, keepdims=True)
        pv = jnp.zeros((N_HEADS, HEAD_DIM), F32)
        for i in range(pg):
            pb = p[:, i * rows:(i + 1) * rows].astype(BF16)
            pv = pv + jnp.dot(pb, vbuf[slot, i].astype(BF16), preferred_element_type=F32)
        return m_new, l_new, alpha * acc + pv

    init = (jnp.full((N_HEADS, 1), NEG, F32), jnp.zeros((N_HEADS, 1), F32), jnp.zeros((N_HEADS, HEAD_DIM), F32))
    m_run, l_run, acc = lax.fori_loop(0, ngroups, body, init)

    kn = kn_ref[0].astype(BF16).astype(F32)
    s_n = jnp.sum(q.astype(F32) * kn, axis=1, keepdims=True) + bn_ref[0]
    m_fin = jnp.maximum(m_run, s_n)
    alpha = jnp.exp2(m_run - m_fin)
    p_n = jnp.exp2(s_n - m_fin)
    l_fin = alpha * l_run + p_n
    vn = vn_ref[0].astype(BF16).astype(F32)
    o_ref[0] = (alpha * acc + p_n.astype(BF16).astype(F32) * vn) / l_fin


def paged_decode_attention(page_table, q, cache_k, cache_v, layer, bias, k_new, v_new, bias_new):
    b, npages = page_table.shape
    rows = PAGE * N_KV
    nl, npool = cache_k.shape[:2]
    per_head_bias = bias.ndim == 4
    bias_spec = (pl.BlockSpec((1, npages, N_HEADS, rows), lambda bi, pt: (bi, 0, 0, 0)) if per_head_bias
                 else pl.BlockSpec((1, npages, rows), lambda bi, pt: (bi, 0, 0)))
    pg = math.gcd(DECODE_PAGES, npages)
    per_b = lambda bi, pt: (bi, 0, 0)
    per_head = lambda x: jnp.repeat(x.reshape(b, N_KV, HEAD_DIM), GROUP, axis=1)
    grid_spec = pltpu.PrefetchScalarGridSpec(
        num_scalar_prefetch=1,
        grid=(b,),
        in_specs=[pl.BlockSpec((1, N_HEADS, HEAD_DIM), per_b),
                  pl.BlockSpec(memory_space=pl.ANY),
                  pl.BlockSpec(memory_space=pl.ANY),
                  bias_spec,
                  pl.BlockSpec((1, N_HEADS, HEAD_DIM), per_b),
                  pl.BlockSpec((1, N_HEADS, HEAD_DIM), per_b),
                  pl.BlockSpec((1, N_HEADS, 1), per_b)],
        out_specs=pl.BlockSpec((1, N_HEADS, HEAD_DIM), per_b),
        scratch_shapes=[pltpu.VMEM((2, pg, rows, HEAD_DIM), F32),
                        pltpu.VMEM((2, pg, rows, HEAD_DIM), F32),
                        pltpu.SemaphoreType.DMA((2, 2))])
    return pl.pallas_call(
        functools.partial(_decode_kernel, layer=layer, per_head_bias=per_head_bias),
        grid_spec=grid_spec,
        out_shape=jax.ShapeDtypeStruct((b, N_HEADS, HEAD_DIM), F32),
        compiler_params=_params(("arbitrary",)),
        name="paged_decode_attention",
    )(page_table, q, cache_k.reshape(nl, npool, rows, HEAD_DIM), cache_v.reshape(nl, npool, rows, HEAD_DIM),
      bias, per_head(k_new), per_head(v_new), bias_new)


def _row_expansion():
    t = lax.broadcasted_iota(I32, (PAGE, PAGE * N_KV), 0)
    r = lax.broadcasted_iota(I32, (PAGE, PAGE * N_KV), 1)
    return r // N_KV == t


def _cmul(ar, ai, br, bi):
    return ar * br - ai * bi, ar * bi + ai * br


def _ssm_tables(a_re, a_im, log_dt, b_re, b_im, c_re, c_im, d_skip):
    L, cdim = SSM_CHUNK, SSM_GROUP
    g = a_re.shape[0]
    dt = jnp.exp(log_dt)[:, None]
    mag = jnp.exp(a_re * dt)
    ab = (mag * jnp.cos(a_im * dt), mag * jnp.sin(a_im * dt))
    den = a_re * a_re + a_im * a_im
    num = _cmul(ab[0] - 1.0, ab[1], a_re, -a_im)
    sc = (num[0] / den, num[1] / den)
    bs = _cmul(b_re, b_im, sc[0][:, :, None], sc[1][:, :, None])
    pows = [(jnp.ones_like(ab[0]), jnp.zeros_like(ab[0]))]
    for _ in range(L):
        pows.append(_cmul(*pows[-1], *ab))
    pw = (jnp.stack([p[0] for p in pows]), jnp.stack([p[1] for p in pows]))
    ca = _cmul(c_re[None], c_im[None], pw[0][:L, :, None, :], pw[1][:L, :, None, :])
    kd = (jnp.einsum('dgcn,gne->dgce', ca[0], bs[0], precision=HI)
          - jnp.einsum('dgcn,gne->dgce', ca[1], bs[1], precision=HI))
    dmat = jnp.eye(cdim, dtype=F32)[None] * d_skip.reshape(g, cdim)[:, :, None]
    kd = kd.at[0].add(dmat)
    ii = jnp.arange(L)
    diff = ii[None, :] - ii[:, None]
    kt = jnp.where((diff >= 0)[:, :, None, None, None], kd[jnp.clip(diff, 0, L - 1)], 0.0)
    m = kt.transpose(2, 0, 4, 1, 3).reshape(g, L * cdim, L * cdim)
    bst = (bs[0].transpose(0, 2, 1)[None], bs[1].transpose(0, 2, 1)[None])
    pj = _cmul(pw[0][L - 1 - ii][:, :, None, :], pw[1][L - 1 - ii][:, :, None, :], *bst)
    p = jnp.concatenate([x.transpose(1, 0, 2, 3).reshape(g, L * cdim, -1) for x in pj], axis=-1)
    cp = _cmul(c_re[None], c_im[None], pw[0][1:L + 1][:, :, None, :], pw[1][1:L + 1][:, :, None, :])
    cp = [x.transpose(1, 3, 0, 2).reshape(g, -1, L * cdim) for x in cp]
    q = jnp.concatenate([cp[0], -cp[1]], axis=1)
    return m, p, q, pw[0][L], pw[1][L], ab, bs


def _ssm_kernel(u_ref, m_ref, p_ref, q_ref, ar_ref, ai_ref, y_ref, hf_ref, *, nb):
    u = u_ref[0]
    r = u.shape[0]
    nk = r // nb
    n = SSM_STATE
    h = jnp.dot(u, p_ref[0], precision=HI, preferred_element_type=F32)
    row = lax.broadcasted_iota(I32, (r, 2 * n), 0) % nk
    lane = lax.broadcasted_iota(I32, (1, 2 * n), 1)
    dr = jnp.concatenate([ar_ref[0], ar_ref[0]], axis=1)
    di = jnp.concatenate([-ai_ref[0], ai_ref[0]], axis=1)
    shift = 1
    while shift < nk:
        hs = jnp.where(row >= shift, pltpu.roll(h, shift, 0), 0.0)
        h = h + dr * hs + di * pltpu.roll(hs, n, 1)
        re, im = dr, jnp.where(lane < n, -di, di)
        re2, im2 = re * re - im * im, 2.0 * re * im
        dr, di = re2, jnp.where(lane < n, -im2, im2)
        shift *= 2
    hprev = jnp.where(row >= 1, pltpu.roll(h, 1, 0), 0.0)
    y_ref[0] = (jnp.dot(u, m_ref[0], precision=HI, preferred_element_type=F32)
                + jnp.dot(hprev, q_ref[0], precision=HI, preferred_element_type=F32))
    for b in range(nb):
        hf_ref[0, b:b + 1, :] = h[(b + 1) * nk - 1:(b + 1) * nk, :]


def ssm_prompt(u_g, m, p, q, al_re, al_im, nb):
    g, r, w = u_g.shape
    n2 = p.shape[2]
    gmap = lambda i: (i, 0, 0)
    return pl.pallas_call(
        functools.partial(_ssm_kernel, nb=nb),
        grid=(g,),
        in_specs=[pl.BlockSpec((1, r, w), gmap), pl.BlockSpec((1, w, w), gmap), pl.BlockSpec((1, w, n2), gmap),
                  pl.BlockSpec((1, n2, w), gmap), pl.BlockSpec((1, 1, n2 // 2), gmap),
                  pl.BlockSpec((1, 1, n2 // 2), gmap)],
        out_specs=[pl.BlockSpec((1, r, w), gmap), pl.BlockSpec((1, nb, n2), gmap)],
        out_shape=[jax.ShapeDtypeStruct((g, r, w), F32), jax.ShapeDtypeStruct((g, nb, n2), F32)],
        compiler_params=_params(("parallel",)),
        name="ssm_prompt",
    )(u_g, m, p, q, al_re.reshape(g, 1, -1), al_im.reshape(g, 1, -1))


def _ssm_step_kernel(u_ref, hr_ref, hi_ref, ar_ref, ai_ref, br_ref, bi_ref, cr_ref, ci_ref, d_ref,
                     y_ref, or_ref, oi_ref):
    u = u_ref[0]
    bur = jnp.zeros(hr_ref.shape[1:], F32)
    bui = jnp.zeros(hr_ref.shape[1:], F32)
    for c in range(SSM_GROUP):
        uc = u[:, c:c + 1]
        bur = bur + br_ref[c] * uc
        bui = bui + bi_ref[c] * uc
    ar, ai = ar_ref[...], ai_ref[...]
    h0r, h0i = hr_ref[0], hi_ref[0]
    hr = ar * h0r - ai * h0i + bur
    hi = ar * h0i + ai * h0r + bui
    or_ref[0] = hr
    oi_ref[0] = hi
    lane = lax.broadcasted_iota(I32, u.shape, 1)
    y = d_ref[...] * u
    for c in range(SSM_GROUP):
        col = jnp.sum(cr_ref[c] * hr - ci_ref[c] * hi, axis=1, keepdims=True)
        y = y + jnp.where(lane == c, col, 0.0)
    y_ref[0] = y


def ssm_step(u, h_re, h_im, a_bar, bs, c_re, c_im, d_skip):
    b, g, c = u.shape
    n = h_re.shape[2]
    per_b = lambda i: (i, 0, 0)
    c3 = lambda i: (0, 0, 0)
    c2 = lambda i: (0, 0)
    return pl.pallas_call(
        _ssm_step_kernel,
        grid=(b,),
        in_specs=[pl.BlockSpec((1, g, c), per_b), pl.BlockSpec((1, g, n), per_b), pl.BlockSpec((1, g, n), per_b),
                  pl.BlockSpec((g, n), c2), pl.BlockSpec((g, n), c2),
                  pl.BlockSpec((c, g, n), c3), pl.BlockSpec((c, g, n), c3),
                  pl.BlockSpec((c, g, n), c3), pl.BlockSpec((c, g, n), c3),
                  pl.BlockSpec((g, c), c2)],
        out_specs=[pl.BlockSpec((1, g, c), per_b), pl.BlockSpec((1, g, n), per_b), pl.BlockSpec((1, g, n), per_b)],
        out_shape=[jax.ShapeDtypeStruct((b, g, c), F32), jax.ShapeDtypeStruct((b, g, n), F32),
                   jax.ShapeDtypeStruct((b, g, n), F32)],
        compiler_params=_params(("parallel",)),
        name="ssm_step",
    )(u, h_re, h_im, a_bar[0], a_bar[1], bs[0].transpose(2, 0, 1), bs[1].transpose(2, 0, 1),
      c_re.transpose(1, 0, 2), c_im.transpose(1, 0, 2), d_skip.reshape(g, c))


def _dsa_layer(xp, xs, gain, cache_k, cache_v, cache_ik, layer, page_table, w_in, w_out, q_gain, k_gain, ik_gain,
               bp, t, past_len, tm):
    mp, ms = xp.shape[0], xs.shape[0]
    nkv = N_KV * HEAD_DIM
    nq = t // TQ
    proj = norm_proj(xp, gain, w_in, tm)
    q, k, v, ik, iw, kh, vt, ikb, iq_t = dsa_post(proj, jnp.arange(t), q_gain, k_gain, ik_gain, KC, True)
    iw_t = iw.reshape(bp, nq, TQ, N_IDX_HEADS).transpose(0, 1, 3, 2)
    o = dsa_attention(iq_t.reshape(bp, nq * IDX_DIM, N_IDX_HEADS * TQ), iw_t, ikb.reshape(bp, t, IDX_DIM),
                      q.reshape(bp, t, -1), kh.reshape(N_KV, bp, t, HEAD_DIM),
                      vt.reshape(N_KV, bp, t // KC, HEAD_DIM, KC))
    xp = out_res(xp, o.reshape(mp, -1), w_out, tm)
    projs = norm_proj(xs, gain, w_in, ms)
    pos_s = jnp.full((ms,), past_len, I32)
    qs, ks, vs, iks, iws, iqs = dsa_post(projs, pos_s, q_gain, k_gain, ik_gain, ms, False)
    bias, bias_n = dsa_sample_index(page_table, iqs.reshape(ms, N_IDX_HEADS, IDX_DIM),
                                    iws.reshape(ms, N_IDX_HEADS, 1), cache_ik.transpose(0, 1, 3, 2), layer,
                                    iks.reshape(ms, 1, IDX_DIM))
    bias_nh = jnp.broadcast_to(bias_n[:, :, :1], (ms, N_HEADS, 1))
    os_ = paged_decode_attention(page_table, qs.reshape(ms, N_HEADS, HEAD_DIM), cache_k, cache_v, layer, bias,
                                 ks, vs, bias_nh)
    xs = out_res(xs, os_.reshape(ms, -1), w_out, ms)
    return xp, xs, (k, v, ik), (ks, vs, iks)


def _key_bias_columns(parts, bp, t):
    cols = jnp.stack(parts, axis=-1).reshape(bp, t, N_KV, GROUP * 3).transpose(0, 2, 1, 3)
    return jnp.pad(cols, ((0, 0), (0, 0), (0, 0), (0, HEAD_DIM - GROUP * 3)))


def _fox_layer(xp, xs, gain, cache_k, cache_v, cache_lf, layer, page_table, w_in, b_f, w_out, q_gain, k_gain,
               bp, t, tm):
    mp, ms = xp.shape[0], xs.shape[0]
    nkv = N_KV * HEAD_DIM
    proj = norm_proj(xp, gain, w_in, tm)
    q, k, v, lf, kh, vt = fox_post(proj, b_f, q_gain, k_gain, KC, True)
    c_parts = cumsum_rows_split(lf.reshape(bp, t, N_HEADS))
    o = fox_attention(q.reshape(bp, t, -1), kh.reshape(N_KV, bp, t, HEAD_DIM), _key_bias_columns(c_parts, bp, t),
                      vt.reshape(N_KV, bp, t // KC, HEAD_DIM, KC))
    xp = out_res(xp, o.reshape(mp, -1), w_out, tm)
    projs = norm_proj(xs, gain, w_in, ms)
    qs, ks, vs, lfs = fox_post(projs, b_f, q_gain, k_gain, ms, False)
    bias = fox_sample_bias(page_table, cache_lf.transpose(0, 1, 3, 2), layer, lfs.reshape(ms, N_HEADS, 1))
    os_ = paged_decode_attention(page_table, qs.reshape(ms, N_HEADS, HEAD_DIM), cache_k, cache_v, layer, bias,
                                 ks, vs, jnp.zeros((ms, N_HEADS, 1), F32))
    xs = out_res(xs, os_.reshape(ms, -1), w_out, ms)
    return xp, xs, (k, v, lf), (ks, vs, lfs)


def _ssm_layer(xp, xs, gain, h0_re, h0_im, a_re, a_im, log_dt, b_re, b_im, c_re, c_im, d_skip, w_glu, bp, t, tm):
    mp, ms = xp.shape[0], xs.shape[0]
    g, n, cdim, L = N_SSM_GROUPS, SSM_STATE, SSM_GROUP, SSM_CHUNK
    m, p, q, al_re, al_im, a_bar, bs = _ssm_tables(a_re, a_im, log_dt, b_re, b_im, c_re, c_im, d_skip)
    u = rmsnorm(xp, gain, tm)
    u_g = u.reshape(mp // L, L, g, cdim).transpose(2, 0, 1, 3).reshape(g, mp // L, L * cdim)
    y_g, hf = ssm_prompt(u_g, m, p, q, al_re, al_im, bp)
    y = y_g.reshape(g, mp // L, L, cdim).transpose(1, 2, 0, 3).reshape(mp, g * cdim)
    xp = glu_res(xp, y, w_glu, tm)
    hf = hf.transpose(1, 0, 2)
    us = rmsnorm(xs, gain, ms)
    ys, hs_re, hs_im = ssm_step(us.reshape(ms, g, cdim), h0_re, h0_im, a_bar, bs, c_re, c_im, d_skip)
    xs = glu_res(xs, ys.reshape(ms, g * cdim), w_glu, ms)
    return xp, xs, (hf[..., :n], hf[..., n:]), (hs_re, hs_im)


def kernel(x_prompt, x_sample, cache_dsa_k, cache_dsa_v, cache_dsa_idx_k, cache_fox_k, cache_fox_v, cache_fox_logf, state_ssm_re, state_ssm_im, page_table, norm_mix, norm_ffn, dsa_w_in, dsa_w_out, dsa_q_gain, dsa_k_gain, dsa_ik_gain, fox_w_in, fox_b_f, fox_w_out, fox_q_gain, fox_k_gain, ssm_a_re, ssm_a_im, ssm_log_dt, ssm_b_re, ssm_b_im, ssm_c_re, ssm_c_im, ssm_d, ssm_w_glu, ffn_w_gu, ffn_w_down):
    bp, t, d = x_prompt.shape
    bs_, ts, _ = x_sample.shape
    assert ts == 1, "the sample group carries one new position per sequence"
    assert t % KC == 0 and KC % TQ == 0
    depth = norm_mix.shape[0]
    past_len = page_table.shape[1] * PAGE
    tm = min(512, bp * t)
    xp = x_prompt.reshape(bp * t, d)
    xs = x_sample.reshape(bs_ * ts, d)
    bf = lambda w: w.astype(BF16)
    dsa_p, dsa_s, fox_p, fox_s, ssm_p, ssm_s = [], [], [], [], [], []
    for i in range(depth):
        kind, j = i % N_MIXERS, i // N_MIXERS
        if kind == 0:
            xp, xs, outp, outs = _dsa_layer(
                xp, xs, norm_mix[i], cache_dsa_k, cache_dsa_v, cache_dsa_idx_k, j, page_table,
                bf(dsa_w_in[j]), bf(dsa_w_out[j]), dsa_q_gain[j], dsa_k_gain[j], dsa_ik_gain[j], bp, t, past_len, tm)
            dsa_p.append(outp); dsa_s.append(outs)
        elif kind == 1:
            xp, xs, outp, outs = _fox_layer(
                xp, xs, norm_mix[i], cache_fox_k, cache_fox_v, cache_fox_logf, j, page_table,
                bf(fox_w_in[j]), fox_b_f[j], bf(fox_w_out[j]), fox_q_gain[j], fox_k_gain[j], bp, t, tm)
            fox_p.append(outp); fox_s.append(outs)
        else:
            xp, xs, outp, outs = _ssm_layer(
                xp, xs, norm_mix[i], state_ssm_re[j], state_ssm_im[j], ssm_a_re[j], ssm_a_im[j], ssm_log_dt[j],
                ssm_b_re[j], ssm_b_im[j], ssm_c_re[j], ssm_c_im[j], ssm_d[j], bf(ssm_w_glu[j]), bp, t, tm)
            ssm_p.append(outp); ssm_s.append(outs)
        w_gu, w_down = bf(ffn_w_gu[i]), bf(ffn_w_down[i])
        xp = ffn_res(xp, norm_ffn[i], w_gu, w_down, tm)
        xs = ffn_res(xs, norm_ffn[i], w_gu, w_down, bs_ * ts)

    def stack(items, idx, shape):
        return jnp.stack([it[idx].reshape(shape) for it in items])

    kv_p, kv_s = (bp, t, N_KV, HEAD_DIM), (bs_, ts, N_KV, HEAD_DIM)
    st_p, st_s = (bp, N_SSM_GROUPS, SSM_STATE), (bs_, N_SSM_GROUPS, SSM_STATE)
    return (xp.reshape(bp, t, d), xs.reshape(bs_, ts, d),
            stack(dsa_p, 0, kv_p), stack(dsa_p, 1, kv_p), stack(dsa_p, 2, (bp, t, IDX_DIM)),
            stack(dsa_s, 0, kv_s), stack(dsa_s, 1, kv_s), stack(dsa_s, 2, (bs_, ts, IDX_DIM)),
            stack(fox_p, 0, kv_p), stack(fox_p, 1, kv_p), stack(fox_p, 2, (bp, t, N_HEADS)),
            stack(fox_s, 0, kv_s), stack(fox_s, 1, kv_s), stack(fox_s, 2, (bs_, ts, N_HEADS)),
            stack(ssm_p, 0, st_p), stack(ssm_p, 1, st_p), stack(ssm_s, 0, st_s), stack(ssm_s, 1, st_s))
```

```python
import functools
import math

import jax
import jax.numpy as jnp
from jax import lax
from jax.experimental import pallas as pl
from jax.experimental.pallas import tpu as pltpu

F32 = jnp.float32
BF16 = jnp.bfloat16
I32 = jnp.int32

D_MODEL = 2048
N_HEADS = 16
HEAD_DIM = 128
N_KV = 4
GROUP = N_HEADS // N_KV
N_IDX_HEADS = 16
IDX_DIM = 64
TOPK_MAX = 256
ROPE_THETA = 500000.0
PAGE = 128
SSM_GROUP = 16
N_SSM_GROUPS = D_MODEL // SSM_GROUP
SSM_STATE = 64
SSM_CHUNK = 16
EPS = 1e-6
N_MIXERS = 3

LANES = 128
SUBLANES = 8
TQ = 128
KC = 256
DECODE_PAGES = 8
LOG2E = math.log2(math.e)
NEG = -1e30
BISECT_MAX = 64
HI = lax.Precision.HIGHEST
VMEM_LIMIT = 56 * 1024 * 1024

NT = (((1,), (1,)), ((), ()))


def _params(sem):
    return pltpu.CompilerParams(dimension_semantics=sem, vmem_limit_bytes=VMEM_LIMIT)


def _rms(x, gain):
    return x * lax.rsqrt(jnp.mean(x * x, axis=-1, keepdims=True) + EPS) * gain


def _proj_kernel(x_ref, g_ref, w_ref, o_ref, h_ref):
    @pl.when(pl.program_id(1) == 0)
    def _():
        h_ref[...] = _rms(x_ref[...], g_ref[...]).astype(BF16)

    o_ref[...] = jnp.dot(h_ref[...], w_ref[...], preferred_element_type=F32)


def _pick_tn(n, cap=1536):
    k = -(-n // cap)
    return -(-n // (k * LANES)) * LANES


def norm_proj(x, gain, w, tm):
    m, d = x.shape
    n = w.shape[1]
    tn = _pick_tn(n)
    return pl.pallas_call(
        _proj_kernel,
        grid=(m // tm, pl.cdiv(n, tn)),
        in_specs=[pl.BlockSpec((tm, d), lambda i, j: (i, 0)),
                  pl.BlockSpec((1, d), lambda i, j: (0, 0)),
                  pl.BlockSpec((d, tn), lambda i, j: (0, j))],
        out_specs=pl.BlockSpec((tm, tn), lambda i, j: (i, j)),
        out_shape=jax.ShapeDtypeStruct((m, n), F32),
        scratch_shapes=[pltpu.VMEM((tm, d), BF16)],
        compiler_params=_params(("parallel", "arbitrary")),
        name="norm_proj",
    )(x, gain.reshape(1, d), w)


def _out_res_kernel(x_ref, o_ref, w_ref, y_ref):
    y_ref[...] = x_ref[...] + jnp.dot(o_ref[...].astype(BF16), w_ref[...], preferred_element_type=F32)


def out_res(x, o, w, tm):
    m, d = x.shape
    k = w.shape[0]
    return pl.pallas_call(
        _out_res_kernel,
        grid=(m // tm,),
        in_specs=[pl.BlockSpec((tm, d), lambda i: (i, 0)),
                  pl.BlockSpec((tm, k), lambda i: (i, 0)),
                  pl.BlockSpec((k, d), lambda i: (0, 0))],
        out_specs=pl.BlockSpec((tm, d), lambda i: (i, 0)),
        out_shape=jax.ShapeDtypeStruct((m, d), F32),
        compiler_params=_params(("parallel",)),
        name="out_res",
    )(x, o, w)


def _ffn_kernel(x_ref, g_ref, wg_ref, wu_ref, wd_ref, y_ref, h_ref, acc_ref):
    f = pl.program_id(1)

    @pl.when(f == 0)
    def _():
        h_ref[...] = _rms(x_ref[...], g_ref[...]).astype(BF16)
        acc_ref[...] = jnp.zeros_like(acc_ref)

    h = h_ref[...]
    g = jnp.dot(h, wg_ref[...], preferred_element_type=F32)
    u = jnp.dot(h, wu_ref[...], preferred_element_type=F32)
    a = (g * jax.nn.sigmoid(g) * u).astype(BF16)
    acc_ref[...] += jnp.dot(a, wd_ref[...], preferred_element_type=F32)

    @pl.when(f == pl.num_programs(1) - 1)
    def _():
        y_ref[...] = x_ref[...] + acc_ref[...]


def ffn_res(x, gain, w_gu, w_down, tm, tf=512):
    m, d = x.shape
    dff = w_down.shape[0]
    nf = dff // tf
    return pl.pallas_call(
        _ffn_kernel,
        grid=(m // tm, nf),
        in_specs=[pl.BlockSpec((tm, d), lambda i, f: (i, 0)),
                  pl.BlockSpec((1, d), lambda i, f: (0, 0)),
                  pl.BlockSpec((d, tf), lambda i, f: (0, f)),
                  pl.BlockSpec((d, tf), lambda i, f: (0, f + nf)),
                  pl.BlockSpec((tf, d), lambda i, f: (f, 0))],
        out_specs=pl.BlockSpec((tm, d), lambda i, f: (i, 0)),
        out_shape=jax.ShapeDtypeStruct((m, d), F32),
        scratch_shapes=[pltpu.VMEM((tm, d), BF16), pltpu.VMEM((tm, d), F32)],
        compiler_params=_params(("parallel", "arbitrary")),
        name="ffn_res",
    )(x, gain.reshape(1, d), w_gu, w_gu, w_down)


def _gelu_tanh(y):
    return 0.5 * y * (1.0 + jnp.tanh(math.sqrt(2.0 / math.pi) * (y + 0.044715 * (y * y * y))))


def _glu_kernel(x_ref, y_ref, wa_ref, wb_ref, o_ref, g_ref):
    @pl.when(pl.program_id(1) == 0)
    def _():
        g_ref[...] = _gelu_tanh(y_ref[...]).astype(BF16)

    g = g_ref[...]
    a = jnp.dot(g, wa_ref[...], preferred_element_type=F32)
    b = jnp.dot(g, wb_ref[...], preferred_element_type=F32)
    o_ref[...] = x_ref[...] + a * jax.nn.sigmoid(b)


def glu_res(x, y, w_glu, tm, tn=512):
    m, d = x.shape
    nj = d // tn
    return pl.pallas_call(
        _glu_kernel,
        grid=(m // tm, nj),
        in_specs=[pl.BlockSpec((tm, tn), lambda i, j: (i, j)),
                  pl.BlockSpec((tm, d), lambda i, j: (i, 0)),
                  pl.BlockSpec((d, tn), lambda i, j: (0, j)),
                  pl.BlockSpec((d, tn), lambda i, j: (0, j + nj))],
        out_specs=pl.BlockSpec((tm, tn), lambda i, j: (i, j)),
        out_shape=jax.ShapeDtypeStruct((m, d), F32),
        scratch_shapes=[pltpu.VMEM((tm, d), BF16)],
        compiler_params=_params(("parallel", "arbitrary")),
        name="glu_res",
    )(x, y, w_glu, w_glu)


def _rmsnorm_kernel(x_ref, g_ref, o_ref):
    o_ref[...] = _rms(x_ref[...], g_ref[...])


def rmsnorm(x, gain, tm):
    m, d = x.shape
    return pl.pallas_call(
        _rmsnorm_kernel,
        grid=(m // tm,),
        in_specs=[pl.BlockSpec((tm, d), lambda i: (i, 0)), pl.BlockSpec((1, d), lambda i: (0, 0))],
        out_specs=pl.BlockSpec((tm, d), lambda i: (i, 0)),
        out_shape=jax.ShapeDtypeStruct((m, d), F32),
        compiler_params=_params(("parallel",)),
        name="rmsnorm",
    )(x, gain.reshape(1, d))


def _rope_tables(pos, width, period):
    rot = period // 4
    half = rot // 2
    inv = ROPE_THETA ** (-jnp.arange(half, dtype=F32) / half)
    ang = pos.astype(F32)[:, None] * inv[None, :]
    cos, sin = jnp.cos(ang), jnp.sin(ang)
    n = pos.shape[0]
    ones = jnp.ones((n, period - rot), F32)
    zeros = jnp.zeros((n, period - rot), F32)
    zh = jnp.zeros((n, half), F32)
    c = jnp.concatenate([cos, cos, ones], axis=1)
    s_up = jnp.concatenate([zh, sin, zeros], axis=1)
    s_dn = jnp.concatenate([-sin, zh, zeros], axis=1)
    reps = width // period
    return tuple(jnp.tile(a, (1, reps)) for a in (c, s_up, s_dn))


def _rope(x, cos, s_up, s_dn, half):
    w = x.shape[-1]
    return x * cos + pltpu.roll(x, half, 1) * s_up + pltpu.roll(x, w - half, 1) * s_dn


def _store_kv_heads(p_ref, k0, v0, k_fn, k_ref, v_ref, kh_ref, vt_ref):
    for h in range(N_KV):
        y = k_fn(p_ref[:, k0 + h * HEAD_DIM:k0 + (h + 1) * HEAD_DIM])
        k_ref[:, h * HEAD_DIM:(h + 1) * HEAD_DIM] = y
        if kh_ref is not None:
            kh_ref[h] = y.astype(BF16)
            vh = p_ref[:, v0 + h * HEAD_DIM:v0 + (h + 1) * HEAD_DIM]
            vt_ref[h, 0] = vh.T.astype(BF16)
    v_ref[...] = p_ref[:, v0:v0 + N_KV * HEAD_DIM]


def _dsa_post_kernel(p_ref, qg_ref, kg_ref, ikg_ref, c1_ref, u1_ref, d1_ref, c2_ref, u2_ref, d2_ref,
                     q_ref, k_ref, v_ref, ik_ref, iw_ref, *mode_refs, prompt):
    c1, u1, d1 = c1_ref[...], u1_ref[...], d1_ref[...]
    c2, u2, d2 = c2_ref[...], u2_ref[...], d2_ref[...]
    qg, kg = qg_ref[...], kg_ref[...]
    scale = HEAD_DIM ** -0.5 * LOG2E
    for h in range(N_HEADS):
        x = p_ref[:, h * HEAD_DIM:(h + 1) * HEAD_DIM]
        y = _rope(_rms(x, qg), c1, u1, d1, HEAD_DIM // 8)
        q_ref[:, h * HEAD_DIM:(h + 1) * HEAD_DIM] = (y * scale).astype(BF16)
    k0 = N_HEADS * HEAD_DIM
    v0 = k0 + N_KV * HEAD_DIM
    kh_ref, vt_ref, ikb_ref, iqt_ref = mode_refs if prompt else (None, None, None, None)
    iq_ref = None if prompt else mode_refs[0]
    _store_kv_heads(p_ref, k0, v0, lambda x: _rope(_rms(x, kg), c1, u1, d1, HEAD_DIM // 8),
                    k_ref, v_ref, kh_ref, vt_ref)
    i0 = v0 + N_KV * HEAD_DIM
    tm = p_ref.shape[0]
    for h in range(N_IDX_HEADS * IDX_DIM // LANES):
        x = _rope(p_ref[:, i0 + h * LANES:i0 + (h + 1) * LANES], c2, u2, d2, IDX_DIM // 8)
        if prompt:
            for qb in range(tm // TQ):
                xt = x[qb * TQ:(qb + 1) * TQ].T.astype(BF16)
                iqt_ref[qb, :, (2 * h) * TQ:(2 * h + 1) * TQ] = xt[:IDX_DIM]
                iqt_ref[qb, :, (2 * h + 1) * TQ:(2 * h + 2) * TQ] = xt[IDX_DIM:]
        else:
            iq_ref[:, h * LANES:(h + 1) * LANES] = x.astype(BF16)
    j0 = i0 + N_IDX_HEADS * IDX_DIM
    xk = _rms(p_ref[:, j0:j0 + IDX_DIM], ikg_ref[...])
    xk2 = jnp.concatenate([xk, xk], axis=1)
    ik = _rope(xk2, c2, u2, d2, IDX_DIM // 8)[:, :IDX_DIM]
    ik_ref[...] = ik
    if prompt:
        ikb_ref[...] = ik.astype(BF16)
    iw_ref[...] = p_ref[:, j0 + IDX_DIM:j0 + IDX_DIM + N_IDX_HEADS] * (N_IDX_HEADS ** -0.5 * IDX_DIM ** -0.5)


def dsa_post(proj, pos, q_gain, k_gain, ik_gain, tm, prompt):
    m, n = proj.shape
    npos = pos.shape[0]
    t1 = _rope_tables(pos, HEAD_DIM, HEAD_DIM)
    t2 = _rope_tables(pos, LANES, IDX_DIM)
    nb = npos // tm
    row = lambda i: (i, 0)
    tab = lambda i: (i % nb, 0)
    const = lambda i: (0, 0)
    nkv = N_KV * HEAD_DIM
    niq = N_IDX_HEADS * IDX_DIM
    outs = [((m, N_HEADS * HEAD_DIM), BF16, pl.BlockSpec((tm, N_HEADS * HEAD_DIM), row)),
            ((m, nkv), F32, pl.BlockSpec((tm, nkv), row)),
            ((m, nkv), F32, pl.BlockSpec((tm, nkv), row)),
            ((m, IDX_DIM), F32, pl.BlockSpec((tm, IDX_DIM), row)),
            ((m, N_IDX_HEADS), F32, pl.BlockSpec((tm, N_IDX_HEADS), row))]
    if not prompt:
        outs += [((m, niq), BF16, pl.BlockSpec((tm, niq), row))]
    if prompt:
        assert tm == KC and tm % TQ == 0
        outs += [((N_KV, m, HEAD_DIM), BF16, pl.BlockSpec((N_KV, tm, HEAD_DIM), lambda i: (0, i, 0))),
                 ((N_KV, m // KC, HEAD_DIM, KC), BF16, pl.BlockSpec((N_KV, 1, HEAD_DIM, KC), lambda i: (0, i, 0, 0))),
                 ((m, IDX_DIM), BF16, pl.BlockSpec((tm, IDX_DIM), row)),
                 ((m // TQ, IDX_DIM, N_IDX_HEADS * TQ), BF16,
                  pl.BlockSpec((tm // TQ, IDX_DIM, N_IDX_HEADS * TQ), lambda i: (i, 0, 0)))]
    return pl.pallas_call(
        functools.partial(_dsa_post_kernel, prompt=prompt),
        grid=(m // tm,),
        in_specs=[pl.BlockSpec((tm, n), row),
                  pl.BlockSpec((1, HEAD_DIM), const), pl.BlockSpec((1, HEAD_DIM), const),
                  pl.BlockSpec((1, IDX_DIM), const)]
                 + [pl.BlockSpec((tm, LANES), tab)] * 6,
        out_specs=[o[2] for o in outs],
        out_shape=[jax.ShapeDtypeStruct(o[0], o[1]) for o in outs],
        compiler_params=_params(("parallel",)),
        name="dsa_post",
    )(proj, q_gain.reshape(1, -1), k_gain.reshape(1, -1), ik_gain.reshape(1, -1), *t1, *t2)


def _log_sigmoid(x):
    return jnp.minimum(x, 0.0) - jnp.log1p(jnp.exp(-jnp.abs(x)))


def _fox_post_kernel(p_ref, qg_ref, kg_ref, bf_ref, q_ref, k_ref, v_ref, lf_ref, *prompt_refs, prompt):
    qg, kg = qg_ref[...], kg_ref[...]
    scale = HEAD_DIM ** -0.5 * LOG2E
    for h in range(N_HEADS):
        x = p_ref[:, h * HEAD_DIM:(h + 1) * HEAD_DIM]
        q_ref[:, h * HEAD_DIM:(h + 1) * HEAD_DIM] = (_rms(x, qg) * scale).astype(BF16)
    k0 = N_HEADS * HEAD_DIM
    v0 = k0 + N_KV * HEAD_DIM
    kh_ref, vt_ref = prompt_refs if prompt else (None, None)
    _store_kv_heads(p_ref, k0, v0, lambda x: _rms(x, kg), k_ref, v_ref, kh_ref, vt_ref)
    f0 = v0 + N_KV * HEAD_DIM
    lf_ref[...] = _log_sigmoid(p_ref[:, f0:f0 + N_HEADS] + bf_ref[...])


def fox_post(proj, b_f, q_gain, k_gain, tm, prompt):
    m, n = proj.shape
    row = lambda i: (i, 0)
    const = lambda i: (0, 0)
    nkv = N_KV * HEAD_DIM
    outs = [((m, N_HEADS * HEAD_DIM), BF16, pl.BlockSpec((tm, N_HEADS * HEAD_DIM), row)),
            ((m, nkv), F32, pl.BlockSpec((tm, nkv), row)),
            ((m, nkv), F32, pl.BlockSpec((tm, nkv), row)),
            ((m, N_HEADS), F32, pl.BlockSpec((tm, N_HEADS), row))]
    if prompt:
        assert tm == KC
        outs += [((N_KV, m, HEAD_DIM), BF16, pl.BlockSpec((N_KV, tm, HEAD_DIM), lambda i: (0, i, 0))),
                 ((N_KV, m // KC, HEAD_DIM, KC), BF16, pl.BlockSpec((N_KV, 1, HEAD_DIM, KC), lambda i: (0, i, 0, 0)))]
    return pl.pallas_call(
        functools.partial(_fox_post_kernel, prompt=prompt),
        grid=(m // tm,),
        in_specs=[pl.BlockSpec((tm, n), row), pl.BlockSpec((1, HEAD_DIM), const),
                  pl.BlockSpec((1, HEAD_DIM), const), pl.BlockSpec((1, N_HEADS), const)],
        out_specs=[o[2] for o in outs],
        out_shape=[jax.ShapeDtypeStruct(o[0], o[1]) for o in outs],
        compiler_params=_params(("parallel",)),
        name="fox_post",
    )(proj, q_gain.reshape(1, -1), k_gain.reshape(1, -1), b_f.reshape(1, -1))


def _queries_t(q_ref, j):
    cols = []
    for g in range(GROUP):
        x = q_ref[0, :, (GROUP * j + g) * HEAD_DIM:(GROUP * j + g + 1) * HEAD_DIM]
        cols.append(x.astype(F32).T.astype(BF16))
    return jnp.concatenate(cols, axis=1)


def _attend_heads(logits_fn, values_fn, m_ref, l_ref, acc_ref):
    s_next = logits_fn(0)
    pending = None
    for j in range(N_KV):
        s_t = s_next
        if j + 1 < N_KV:
            s_next = logits_fn(j + 1)
        m_run = m_ref[j]
        m_new = jnp.maximum(m_run, jnp.max(s_t, axis=0, keepdims=True))
        alpha = jnp.exp2(m_run - m_new)
        p = jnp.exp2(s_t - m_new)
        l_ref[j] = alpha * l_ref[j] + jnp.sum(p, axis=0, keepdims=True)
        m_ref[j] = m_new
        pv = jnp.dot(values_fn(j), p.astype(BF16), preferred_element_type=F32)
        if pending is not None:
            jp, alpha_p, pv_p = pending
            acc_ref[jp] = alpha_p * acc_ref[jp] + pv_p
        pending = (j, alpha, pv)
    jp, alpha_p, pv_p = pending
    acc_ref[jp] = alpha_p * acc_ref[jp] + pv_p


def _softmax_reset(m_ref, l_ref, acc_ref):
    m_ref[...] = jnp.full_like(m_ref, NEG)
    l_ref[...] = jnp.zeros_like(l_ref)
    acc_ref[...] = jnp.zeros_like(acc_ref)


def _softmax_scratch(qk_depth):
    rows = GROUP * TQ
    return [pltpu.VMEM((N_KV, qk_depth, rows), BF16), pltpu.VMEM((N_KV, 1, rows), F32),
            pltpu.VMEM((N_KV, 1, rows), F32), pltpu.VMEM((N_KV, HEAD_DIM, rows), F32)]


def _store_heads_t(o_ref, l_ref, acc_ref):
    for j in range(N_KV):
        o_t = acc_ref[j] / l_ref[j]
        for g in range(GROUP):
            o_ref[0, :, (GROUP * j + g) * HEAD_DIM:(GROUP * j + g + 1) * HEAD_DIM] = (
                o_t[:, g * TQ:(g + 1) * TQ].T.astype(o_ref.dtype))


def _cumsum_kernel(lf_ref, hi_ref, mid_ref, lo_ref):
    t = lf_ref.shape[1]
    r = lax.broadcasted_iota(I32, (LANES, LANES), 0)
    c = lax.broadcasted_iota(I32, (LANES, LANES), 1)
    lower = (c <= r).astype(F32)
    carry = jnp.zeros((1, lf_ref.shape[2]), F32)
    for b in range(t // LANES):
        rows = slice(b * LANES, (b + 1) * LANES)
        cs = jnp.dot(lower, lf_ref[0, rows, :], precision=HI, preferred_element_type=F32) + carry
        carry = cs[LANES - 1:LANES, :]
        x = cs * LOG2E
        hi = x.astype(BF16)
        r1 = x - hi.astype(F32)
        mid = r1.astype(BF16)
        hi_ref[0, rows, :] = hi
        mid_ref[0, rows, :] = mid
        lo_ref[0, rows, :] = (r1 - mid.astype(F32)).astype(BF16)


def cumsum_rows_split(lf):
    b, t, h = lf.shape
    spec = pl.BlockSpec((1, t, h), lambda i: (i, 0, 0))
    return pl.pallas_call(
        _cumsum_kernel,
        grid=(b,),
        in_specs=[spec],
        out_specs=[spec] * 3,
        out_shape=[jax.ShapeDtypeStruct((b, t, h), BF16)] * 3,
        compiler_params=_params(("parallel",)),
        name="cumsum_rows_split",
    )(lf)


def _fox_kernel(q_ref, k_ref, cp_ref, v_ref, o_ref, qt_ref, m_ref, l_ref, acc_ref):
    i = pl.program_id(1)
    rows = GROUP * TQ
    last = (i * TQ) // KC
    r_io = lax.broadcasted_iota(I32, (HEAD_DIM, rows), 0)
    g_io = lax.broadcasted_iota(I32, (HEAD_DIM, rows), 1) // TQ
    sel = jnp.where((r_io >= 3 * g_io) & (r_io < 3 * g_io + 3), -1.0, 0.0).astype(BF16)
    for j in range(N_KV):
        qt_ref[j] = jnp.concatenate([_queries_t(q_ref, j), sel], axis=0)
    _softmax_reset(m_ref, l_ref, acc_ref)

    def logits(j, c):
        start = pl.multiple_of(c * KC, KC)
        kc = jnp.concatenate([k_ref[j, 0, pl.ds(start, KC), :], cp_ref[0, j, pl.ds(start, KC), :]], axis=1)
        return jnp.dot(kc, qt_ref[j], preferred_element_type=F32)

    def body(c, _):
        _attend_heads(lambda j: logits(j, c), lambda j: v_ref[j, 0, c], m_ref, l_ref, acc_ref)
        return 0

    lax.fori_loop(0, last, body, 0)
    key_io = lax.broadcasted_iota(I32, (KC, rows), 0)
    q_io = lax.broadcasted_iota(I32, (KC, rows), 1) % TQ + i * TQ
    causal = key_io + last * KC <= q_io
    _attend_heads(lambda j: jnp.where(causal, logits(j, last), NEG), lambda j: v_ref[j, 0, last],
                  m_ref, l_ref, acc_ref)
    _store_heads_t(o_ref, l_ref, acc_ref)


def fox_attention(q, kh, cparts, vt):
    b, t, _ = q.shape
    return pl.pallas_call(
        _fox_kernel,
        grid=(b, t // TQ),
        in_specs=[pl.BlockSpec((1, TQ, N_HEADS * HEAD_DIM), lambda bi, i: (bi, i, 0)),
                  pl.BlockSpec((N_KV, 1, t, HEAD_DIM), lambda bi, i: (0, bi, 0, 0)),
                  pl.BlockSpec((1, N_KV, t, HEAD_DIM), lambda bi, i: (bi, 0, 0, 0)),
                  pl.BlockSpec((N_KV, 1, t // KC, HEAD_DIM, KC), lambda bi, i: (0, bi, 0, 0, 0))],
        out_specs=pl.BlockSpec((1, TQ, N_HEADS * HEAD_DIM), lambda bi, i: (bi, i, 0)),
        out_shape=jax.ShapeDtypeStruct((b, t, N_HEADS * HEAD_DIM), BF16),
        scratch_shapes=_softmax_scratch(2 * HEAD_DIM),
        compiler_params=_params(("parallel", "arbitrary")),
        name="fox_attention",
    )(q, kh, cparts, vt)


def _all_true(flags):
    return (jnp.min(flags.astype(F32)) > 0.5).astype(I32)


def _bisect_threshold(count_ge, lo, hi, cnt_lo, nsel):
    steps = 2

    def cond(state):
        it, _, _, _, settled, _ = state
        return (it < BISECT_MAX // steps) & (settled == 0)

    def body(state):
        it, lo, hi, cnt_lo, _, _ = state
        stuck = jnp.zeros(lo.shape, I32)
        for _ in range(steps):
            mid = lo + (hi - lo) * 0.5
            stuck = jnp.where((mid <= lo) | (mid >= hi), 1, 0)
            cnt = count_ge(mid)
            up = cnt >= nsel
            lo = jnp.where(up, mid, lo)
            cnt_lo = jnp.where(up, cnt, cnt_lo)
            hi = jnp.where(up, hi, mid)
        exact = jnp.where(cnt_lo == nsel, 1, 0)
        return it + 1, lo, hi, cnt_lo, _all_true(jnp.maximum(exact, stuck)), _all_true(exact)

    init = (jnp.int32(0), lo, hi, cnt_lo, jnp.int32(0), jnp.int32(0))
    _, lo, _, cnt_lo, _, all_exact = lax.while_loop(cond, body, init)
    return lo, cnt_lo, all_exact


def _exact_threshold(min_ge, count_gt_and_next, lo, nsel):
    def cond(state):
        it, _, _, _, _, finished = state
        return (it < BISECT_MAX) & (finished == 0)

    def body(state):
        it, v, tau, need, done, _ = state
        n_gt, v_next = count_gt_and_next(v)
        fin = (n_gt < nsel) & (done == 0)
        tau = jnp.where(fin, v, tau)
        need = jnp.where(fin, nsel - n_gt, need)
        done = jnp.where(fin, 1, done)
        v = jnp.where(done == 1, v, v_next)
        return it + 1, v, tau, need, done, _all_true(done)

    v0 = min_ge(lo)
    init = (jnp.int32(0), v0, v0, jnp.zeros(lo.shape, F32), jnp.zeros(lo.shape, I32), jnp.int32(0))
    _, _, tau, need, _, _ = lax.while_loop(cond, body, init)
    return tau, need


def _select_tie_index(count_eq_below, need, nbits, shape):
    def bit_body(bi, m):
        cand = m | jnp.left_shift(jnp.int32(1), nbits - 1 - bi)
        return jnp.where(count_eq_below(cand) < need, cand, m)

    return lax.fori_loop(0, nbits, bit_body, jnp.zeros(shape, I32))


def _dsa_kernel(iq_ref, iw_ref, ik_ref, q_ref, k_ref, v_ref, o_ref, sc_ref, bias_ref, qt_ref, m_ref, l_ref, acc_ref,
                *, nsel, t_total):
    i = pl.program_id(1)
    nch = (i * TQ + TQ + KC - 1) // KC
    w = iw_ref[0, 0]
    iq_t = iq_ref[0]
    key_io = lax.broadcasted_iota(I32, (KC, TQ), 0)
    qpos = i * TQ + lax.broadcasted_iota(I32, (KC, TQ), 1)
    inf = float("inf")

    def fold(x, op):
        part = op(x.reshape(4, KC // (4 * SUBLANES), SUBLANES, TQ), axis=1)
        return op(part, axis=0)

    def score_body(c, carry):
        lo, hi = carry
        ikc = ik_ref[0, pl.ds(pl.multiple_of(c * KC, KC), KC), :]
        y = jnp.dot(ikc, iq_t, preferred_element_type=F32)
        acc = jnp.zeros((KC, TQ), F32)
        for h in range(N_IDX_HEADS):
            acc = acc + w[h:h + 1, :] * jnp.maximum(y[:, h * TQ:(h + 1) * TQ], 0.0)
        causal = key_io + c * KC <= qpos
        sc_ref[c] = jnp.where(causal, acc, -inf)
        return (jnp.minimum(lo, fold(jnp.where(causal, acc, inf), jnp.min)),
                jnp.maximum(hi, fold(jnp.where(causal, acc, -inf), jnp.max)))

    lo8, hi8 = lax.fori_loop(0, nch, score_body,
                             (jnp.full((SUBLANES, TQ), inf, F32), jnp.full((SUBLANES, TQ), -inf, F32)))

    def count(pred):
        def body(c, acc):
            return acc + fold(jnp.where(pred(sc_ref[c], c), 1.0, 0.0), jnp.sum)
        acc = lax.fori_loop(0, nch, body, jnp.zeros((SUBLANES, TQ), F32))
        return jnp.sum(acc, axis=0, keepdims=True)

    def select():
        lo = jnp.min(lo8, axis=0, keepdims=True)
        hi = jnp.max(hi8, axis=0, keepdims=True)
        n_causal = (qpos[:1] + 1).astype(F32)
        lo, _, all_exact = _bisect_threshold(lambda thr: count(lambda s, c: s >= thr), lo, hi, n_causal, float(nsel))

        def ties():
            def min_ge(lo):
                def body(c, acc):
                    s = sc_ref[c]
                    return jnp.minimum(acc, fold(jnp.where(s >= lo, s, inf), jnp.min))
                acc = lax.fori_loop(0, nch, body, jnp.full((SUBLANES, TQ), inf, F32))
                return jnp.min(acc, axis=0, keepdims=True)

            def count_gt_and_next(v):
                def body(c, carry):
                    n, nxt = carry
                    s = sc_ref[c]
                    above = s > v
                    return (n + fold(jnp.where(above, 1.0, 0.0), jnp.sum),
                            jnp.minimum(nxt, fold(jnp.where(above, s, inf), jnp.min)))
                n, nxt = lax.fori_loop(0, nch, body, (jnp.zeros((SUBLANES, TQ), F32),
                                                      jnp.full((SUBLANES, TQ), inf, F32)))
                return jnp.sum(n, axis=0, keepdims=True), jnp.min(nxt, axis=0, keepdims=True)

            tau, need = _exact_threshold(min_ge, count_gt_and_next, lo, float(nsel))
            nbits = max(1, int(t_total - 1).bit_length())
            m_star = _select_tie_index(
                lambda cand: count(lambda s, c: (s == tau) & (key_io + c * KC < cand)), need, nbits, (1, TQ))
            return tau, m_star

        return lax.cond(all_exact == 1, lambda: (lo, jnp.full((1, TQ), t_total, I32)), ties)

    tau, m_star = lax.cond(i * TQ + TQ <= nsel,
                           lambda: (jnp.full((1, TQ), -inf, F32), jnp.full((1, TQ), t_total, I32)), select)

    def bias_body(c, _):
        s = sc_ref[c]
        idx = key_io + c * KC
        keep = ((s > tau) | ((s == tau) & (idx <= m_star))) & (idx <= qpos)
        bias_ref[c] = jnp.where(keep, 0.0, NEG)
        return 0

    lax.fori_loop(0, nch, bias_body, 0)

    for j in range(N_KV):
        qt_ref[j] = _queries_t(q_ref, j)
    _softmax_reset(m_ref, l_ref, acc_ref)

    def body(c, _):
        start = pl.multiple_of(c * KC, KC)
        bias = jnp.concatenate([bias_ref[c]] * GROUP, axis=1)
        _attend_heads(
            lambda j: jnp.dot(k_ref[j, 0, pl.ds(start, KC), :], qt_ref[j], preferred_element_type=F32) + bias,
            lambda j: v_ref[j, 0, c], m_ref, l_ref, acc_ref)
        return 0

    lax.fori_loop(0, nch, body, 0)
    _store_heads_t(o_ref, l_ref, acc_ref)


def dsa_attention(iq_t, iw_t, ikb, q, kh, vt):
    b, t, _ = q.shape
    nsel = min(TOPK_MAX, t // 4)
    nch = t // KC
    kern = functools.partial(_dsa_kernel, nsel=nsel, t_total=t)
    return pl.pallas_call(
        kern,
        grid=(b, t // TQ),
        in_specs=[pl.BlockSpec((1, IDX_DIM, N_IDX_HEADS * TQ), lambda bi, i: (bi, i, 0)),
                  pl.BlockSpec((1, 1, N_IDX_HEADS, TQ), lambda bi, i: (bi, i, 0, 0)),
                  pl.BlockSpec((1, t, IDX_DIM), lambda bi, i: (bi, 0, 0)),
                  pl.BlockSpec((1, TQ, N_HEADS * HEAD_DIM), lambda bi, i: (bi, i, 0)),
                  pl.BlockSpec((N_KV, 1, t, HEAD_DIM), lambda bi, i: (0, bi, 0, 0)),
                  pl.BlockSpec((N_KV, 1, nch, HEAD_DIM, KC), lambda bi, i: (0, bi, 0, 0, 0))],
        out_specs=pl.BlockSpec((1, TQ, N_HEADS * HEAD_DIM), lambda bi, i: (bi, i, 0)),
        out_shape=jax.ShapeDtypeStruct((b, t, N_HEADS * HEAD_DIM), BF16),
        scratch_shapes=[pltpu.VMEM((nch, KC, TQ), F32), pltpu.VMEM((nch, KC, TQ), F32)] + _softmax_scratch(HEAD_DIM),
        compiler_params=_params(("parallel", "arbitrary")),
        name="dsa_attention",
    )(iq_t, iw_t, ikb, q, kh, vt)


def _page_copy(pool_ref, layer, pt_ref, b, p, buf, slot, sem):
    return pltpu.make_async_copy(pool_ref.at[layer, pt_ref[b, p]], buf.at[slot, p], sem.at[slot])


def _fetch_sequence_pages(pool_ref, layer, pt_ref, buf, sem):
    b = pl.program_id(0)
    nb = pl.num_programs(0)
    npages = buf.shape[1]
    slot = b % 2

    def start_all(bb, sl):
        lax.fori_loop(0, npages, lambda p, _: (_page_copy(pool_ref, layer, pt_ref, bb, p, buf, sl, sem).start(), 0)[1], 0)

    @pl.when(b == 0)
    def _():
        start_all(0, 0)

    @pl.when(b + 1 < nb)
    def _():
        start_all(b + 1, 1 - slot)

    lax.fori_loop(0, npages, lambda p, _: (_page_copy(pool_ref, layer, pt_ref, b, p, buf, slot, sem).wait(), 0)[1], 0)
    return slot


def _dsa_sample_index_kernel(pt_ref, iq_ref, iw_ref, pool_ref, ikn_ref, bias_ref, biasn_ref, buf, sem, sc_ref,
                             *, layer, nsel):
    slot = _fetch_sequence_pages(pool_ref, layer, pt_ref, buf, sem)
    npages = buf.shape[1]
    iq = iq_ref[0]
    w = iw_ref[0]

    unroll = math.gcd(npages, 8)

    def score_body(pb, _):
        for k in range(unroll):
            p = pb * unroll + k
            y = jnp.dot(iq, buf[slot, p].astype(BF16), preferred_element_type=F32)
            sc_ref[pl.ds(p, 1), :] = jnp.sum(w * jnp.maximum(y, 0.0), axis=0, keepdims=True)
        return 0

    lax.fori_loop(0, npages // unroll, score_body, 0)
    prod = iq.astype(F32) * ikn_ref[0].astype(BF16).astype(F32)
    yn = jnp.sum(prod, axis=1, keepdims=True)
    sn = jnp.sum(w * jnp.maximum(yn, 0.0), axis=0, keepdims=True)
    sc = sc_ref[...]
    idx = lax.broadcasted_iota(I32, sc.shape, 0) * PAGE + lax.broadcasted_iota(I32, sc.shape, 1)
    inf = float("inf")

    def total(x, op=jnp.sum):
        return op(op(x, axis=0, keepdims=True), axis=1, keepdims=True)

    def count_ge(thr):
        return total(jnp.where(sc >= thr, 1.0, 0.0)) + jnp.where(sn >= thr, 1.0, 0.0)

    def min_ge(lo):
        return jnp.minimum(total(jnp.where(sc >= lo, sc, inf), jnp.min), jnp.where(sn >= lo, sn, inf))

    def count_gt_and_next(v):
        n = total(jnp.where(sc > v, 1.0, 0.0)) + jnp.where(sn > v, 1.0, 0.0)
        return n, jnp.minimum(total(jnp.where(sc > v, sc, inf), jnp.min), jnp.where(sn > v, sn, inf))

    lo = jnp.minimum(total(sc, jnp.min), sn)
    hi = jnp.maximum(total(sc, jnp.max), sn)
    lo, _, _ = _bisect_threshold(count_ge, lo, hi, jnp.full((1, 1), float(npages * PAGE + 1), F32), float(nsel))
    tau, need = _exact_threshold(min_ge, count_gt_and_next, lo, float(nsel))
    nbits = int(npages * PAGE).bit_length()
    m_star = _select_tie_index(
        lambda cand: total(jnp.where((sc == tau) & (idx < cand), 1.0, 0.0)), need, nbits, (1, 1))
    n_eq = total(jnp.where(sc == tau, 1.0, 0.0))
    keep = (sc > tau) | ((sc == tau) & (idx <= m_star))
    keep_rows = jnp.dot(jnp.where(keep, 1.0, 0.0).astype(BF16), _row_expansion().astype(BF16),
                        preferred_element_type=F32)
    bias_ref[0] = jnp.where(keep_rows > 0.5, 0.0, NEG)
    keep_n = (sn > tau) | ((sn == tau) & (n_eq < need))
    biasn_ref[0] = jnp.broadcast_to(jnp.where(keep_n, 0.0, NEG), (1, LANES))


def dsa_sample_index(page_table, iq, iw, pool_ik_t, layer, ik_new):
    b, npages = page_table.shape
    nsel = min(TOPK_MAX, (npages * PAGE + 1) // 4)
    per_b = lambda bi, pt: (bi, 0, 0)
    grid_spec = pltpu.PrefetchScalarGridSpec(
        num_scalar_prefetch=1,
        grid=(b,),
        in_specs=[pl.BlockSpec((1, N_IDX_HEADS, IDX_DIM), per_b),
                  pl.BlockSpec((1, N_IDX_HEADS, 1), per_b),
                  pl.BlockSpec(memory_space=pl.ANY),
                  pl.BlockSpec((1, 1, IDX_DIM), per_b)],
        out_specs=[pl.BlockSpec((1, npages, PAGE * N_KV), per_b), pl.BlockSpec((1, 1, LANES), per_b)],
        scratch_shapes=[pltpu.VMEM((2, npages, IDX_DIM, PAGE), F32), pltpu.SemaphoreType.DMA((2,)),
                        pltpu.VMEM((npages, PAGE), F32)])
    return pl.pallas_call(
        functools.partial(_dsa_sample_index_kernel, layer=layer, nsel=nsel),
        grid_spec=grid_spec,
        out_shape=[jax.ShapeDtypeStruct((b, npages, PAGE * N_KV), F32), jax.ShapeDtypeStruct((b, 1, LANES), F32)],
        compiler_params=_params(("arbitrary",)),
        name="dsa_sample_index",
    )(page_table, iq, iw, pool_ik_t, ik_new)


def _fox_sample_bias_kernel(pt_ref, pool_ref, lfn_ref, bias_ref, buf, sem, *, layer):
    slot = _fetch_sequence_pages(pool_ref, layer, pt_ref, buf, sem)
    npages, h = buf.shape[1], buf.shape[2]
    rows = npages * h
    x = buf[slot].reshape(rows, PAGE)
    r = lax.broadcasted_iota(I32, (PAGE, PAGE), 0)
    c = lax.broadcasted_iota(I32, (PAGE, PAGE), 1)
    within = jnp.dot(x, (r > c).astype(F32), precision=HI, preferred_element_type=F32)
    tot = jnp.dot(x, jnp.ones((PAGE, PAGE), F32), precision=HI, preferred_element_type=F32)
    row = lax.broadcasted_iota(I32, (rows, PAGE), 0)
    suffix = tot
    shift = h
    while shift < rows:
        suffix = suffix + jnp.where(row + shift < rows, pltpu.roll(suffix, rows - shift, 0), 0.0)
        shift *= 2
    per_tok = within + (suffix - tot)
    per_row = jnp.dot(per_tok, _row_expansion().astype(F32), precision=HI, preferred_element_type=F32)
    bias_ref[0] = (per_row.reshape(npages, h, PAGE * N_KV) + lfn_ref[0][None]) * LOG2E


def fox_sample_bias(page_table, pool_lf_t, layer, lf_new):
    b, npages = page_table.shape
    h = pool_lf_t.shape[2]
    grid_spec = pltpu.PrefetchScalarGridSpec(
        num_scalar_prefetch=1,
        grid=(b,),
        in_specs=[pl.BlockSpec(memory_space=pl.ANY),
                  pl.BlockSpec((1, h, 1), lambda bi, pt: (bi, 0, 0))],
        out_specs=pl.BlockSpec((1, npages, h, PAGE * N_KV), lambda bi, pt: (bi, 0, 0, 0)),
        scratch_shapes=[pltpu.VMEM((2, npages, h, PAGE), F32), pltpu.SemaphoreType.DMA((2,))])
    return pl.pallas_call(
        functools.partial(_fox_sample_bias_kernel, layer=layer),
        grid_spec=grid_spec,
        out_shape=jax.ShapeDtypeStruct((b, npages, h, PAGE * N_KV), F32),
        compiler_params=_params(("arbitrary",)),
        name="fox_sample_bias",
    )(page_table, pool_lf_t, lf_new)


def _decode_kernel(pt_ref, q_ref, pk_ref, pv_ref, b_ref, kn_ref, vn_ref, bn_ref, o_ref, kbuf, vbuf, sem,
                   *, layer, per_head_bias):
    b = pl.program_id(0)
    npages = b_ref.shape[1]
    pg = kbuf.shape[1]
    ngroups = npages // pg
    rows = PAGE * N_KV
    q = q_ref[0]
    row_kv = lax.broadcasted_iota(I32, (N_HEADS, rows), 1) % N_KV
    head_mask = jnp.where(row_kv == lax.broadcasted_iota(I32, (N_HEADS, rows), 0) // GROUP, 0.0, NEG)

    def copies(g, slot):
        out = []
        for i in range(pg):
            page = pt_ref[b, g * pg + i]
            out.append(pltpu.make_async_copy(pk_ref.at[layer, page], kbuf.at[slot, i], sem.at[0, slot]))
            out.append(pltpu.make_async_copy(pv_ref.at[layer, page], vbuf.at[slot, i], sem.at[1, slot]))
        return out

    for cp in copies(0, 0):
        cp.start()

    def body(g, carry):
        m_run, l_run, acc = carry
        slot = g % 2

        @pl.when(g + 1 < ngroups)
        def _():
            for cp in copies(g + 1, 1 - slot):
                cp.start()

        for cp in copies(g, slot):
            cp.wait()
        logits = []
        for i in range(pg):
            s = lax.dot_general(q, kbuf[slot, i].astype(BF16), NT, preferred_element_type=F32)
            page_bias = b_ref[0, g * pg + i] if per_head_bias else b_ref[0, pl.ds(g * pg + i, 1), :]
            logits.append(s + page_bias + head_mask)
        s_all = jnp.concatenate(logits, axis=1)
        m_new = jnp.maximum(m_run, jnp.max(s_all, axis=1, keepdims=True))
        alpha = jnp.exp2(m_run - m_new)
        p = jnp.exp2(s_all - m_new)
        l_new = alpha * l_run + jnp.sum(p, axis=1, keepdims=True)
        pv = jnp.zeros((N_HEADS, HEAD_DIM), F32)
        for i in range(pg):
            pb = p[:, i * rows:(i + 1) * rows].astype(BF16)
            pv = pv + jnp.dot(pb, vbuf[slot, i].astype(BF16), preferred_element_type=F32)
        return m_new, l_new, alpha * acc + pv

    init = (jnp.full((N_HEADS, 1), NEG, F32), jnp.zeros((N_HEADS, 1), F32), jnp.zeros((N_HEADS, HEAD_DIM), F32))
    m_run, l_run, acc = lax.fori_loop(0, ngroups, body, init)

    kn = kn_ref[0].astype(BF16).astype(F32)
    s_n = jnp.sum(q.astype(F32) * kn, axis=1, keepdims=True) + bn_ref[0]
    m_fin = jnp.maximum(m_run, s_n)
    alpha = jnp.exp2(m_run - m_fin)
    p_n = jnp.exp2(s_n - m_fin)
    l_fin = alpha * l_run + p_n
    vn = vn_ref[0].astype(BF16).astype(F32)
    o_ref[0] = (alpha * acc + p_n.astype(BF16).astype(F32) * vn) / l_fin


def paged_decode_attention(page_table, q, cache_k, cache_v, layer, bias, k_new, v_new, bias_new):
    b, npages = page_table.shape
    rows = PAGE * N_KV
    nl, npool = cache_k.shape[:2]
    per_head_bias = bias.ndim == 4
    bias_spec = (pl.BlockSpec((1, npages, N_HEADS, rows), lambda bi, pt: (bi, 0, 0, 0)) if per_head_bias
                 else pl.BlockSpec((1, npages, rows), lambda bi, pt: (bi, 0, 0)))
    pg = math.gcd(DECODE_PAGES, npages)
    per_b = lambda bi, pt: (bi, 0, 0)
    per_head = lambda x: jnp.repeat(x.reshape(b, N_KV, HEAD_DIM), GROUP, axis=1)
    grid_spec = pltpu.PrefetchScalarGridSpec(
        num_scalar_prefetch=1,
        grid=(b,),
        in_specs=[pl.BlockSpec((1, N_HEADS, HEAD_DIM), per_b),
                  pl.BlockSpec(memory_space=pl.ANY),
                  pl.BlockSpec(memory_space=pl.ANY),
                  bias_spec,
                  pl.BlockSpec((1, N_HEADS, HEAD_DIM), per_b),
                  pl.BlockSpec((1, N_HEADS, HEAD_DIM), per_b),
                  pl.BlockSpec((1, N_HEADS, 1), per_b)],
        out_specs=pl.BlockSpec((1, N_HEADS, HEAD_DIM), per_b),
        scratch_shapes=[pltpu.VMEM((2, pg, rows, HEAD_DIM), F32),
                        pltpu.VMEM((2, pg, rows, HEAD_DIM), F32),
                        pltpu.SemaphoreType.DMA((2, 2))])
    return pl.pallas_call(
        functools.partial(_decode_kernel, layer=layer, per_head_bias=per_head_bias),
        grid_spec=grid_spec,
        out_shape=jax.ShapeDtypeStruct((b, N_HEADS, HEAD_DIM), F32),
        compiler_params=_params(("arbitrary",)),
        name="paged_decode_attention",
    )(page_table, q, cache_k.reshape(nl, npool, rows, HEAD_DIM), cache_v.reshape(nl, npool, rows, HEAD_DIM),
      bias, per_head(k_new), per_head(v_new), bias_new)


def _row_expansion():
    t = lax.broadcasted_iota(I32, (PAGE, PAGE * N_KV), 0)
    r = lax.broadcasted_iota(I32, (PAGE, PAGE * N_KV), 1)
    return r // N_KV == t


def _cmul(ar, ai, br, bi):
    return ar * br - ai * bi, ar * bi + ai * br


def _ssm_tables(a_re, a_im, log_dt, b_re, b_im, c_re, c_im, d_skip):
    L, cdim = SSM_CHUNK, SSM_GROUP
    g = a_re.shape[0]
    dt = jnp.exp(log_dt)[:, None]
    mag = jnp.exp(a_re * dt)
    ab = (mag * jnp.cos(a_im * dt), mag * jnp.sin(a_im * dt))
    den = a_re * a_re + a_im * a_im
    num = _cmul(ab[0] - 1.0, ab[1], a_re, -a_im)
    sc = (num[0] / den, num[1] / den)
    bs = _cmul(b_re, b_im, sc[0][:, :, None], sc[1][:, :, None])
    pows = [(jnp.ones_like(ab[0]), jnp.zeros_like(ab[0]))]
    for _ in range(L):
        pows.append(_cmul(*pows[-1], *ab))
    pw = (jnp.stack([p[0] for p in pows]), jnp.stack([p[1] for p in pows]))
    ca = _cmul(c_re[None], c_im[None], pw[0][:L, :, None, :], pw[1][:L, :, None, :])
    kd = (jnp.einsum('dgcn,gne->dgce', ca[0], bs[0], precision=HI)
          - jnp.einsum('dgcn,gne->dgce', ca[1], bs[1], precision=HI))
    dmat = jnp.eye(cdim, dtype=F32)[None] * d_skip.reshape(g, cdim)[:, :, None]
    kd = kd.at[0].add(dmat)
    ii = jnp.arange(L)
    diff = ii[None, :] - ii[:, None]
    kt = jnp.where((diff >= 0)[:, :, None, None, None], kd[jnp.clip(diff, 0, L - 1)], 0.0)
    m = kt.transpose(2, 0, 4, 1, 3).reshape(g, L * cdim, L * cdim)
    bst = (bs[0].transpose(0, 2, 1)[None], bs[1].transpose(0, 2, 1)[None])
    pj = _cmul(pw[0][L - 1 - ii][:, :, None, :], pw[1][L - 1 - ii][:, :, None, :], *bst)
    p = jnp.concatenate([x.transpose(1, 0, 2, 3).reshape(g, L * cdim, -1) for x in pj], axis=-1)
    cp = _cmul(c_re[None], c_im[None], pw[0][1:L + 1][:, :, None, :], pw[1][1:L + 1][:, :, None, :])
    cp = [x.transpose(1, 3, 0, 2).reshape(g, -1, L * cdim) for x in cp]
    q = jnp.concatenate([cp[0], -cp[1]], axis=1)
    return m.astype(BF16), p, q.astype(BF16), pw[0][L], pw[1][L], ab, bs


def _ssm_kernel(u_ref, m_ref, p_ref, q_ref, ar_ref, ai_ref, y_ref, hf_ref, ug_ref, yg_ref, *, nb):
    L, cdim, n = SSM_CHUNK, SSM_GROUP, SSM_STATE
    gpb = LANES // cdim
    r = ug_ref.shape[1]
    nk = r // nb
    for j in range(L):
        xj = u_ref[pl.ds(j, r, stride=L), :]
        for gg in range(gpb):
            ug_ref[gg, :, j * cdim:(j + 1) * cdim] = xj[:, gg * cdim:(gg + 1) * cdim]
    row = lax.broadcasted_iota(I32, (r, 2 * n), 0) % nk
    lane = lax.broadcasted_iota(I32, (1, 2 * n), 1)

    def group_body(gg, _):
        u = ug_ref[gg]
        h = jnp.dot(u, p_ref[gg], precision=HI, preferred_element_type=F32)
        dr = jnp.concatenate([ar_ref[gg], ar_ref[gg]], axis=1)
        di = jnp.concatenate([-ai_ref[gg], ai_ref[gg]], axis=1)
        shift = 1
        while shift < nk:
            hs = jnp.where(row >= shift, pltpu.roll(h, shift, 0), 0.0)
            h = h + dr * hs + di * pltpu.roll(hs, n, 1)
            re, im = dr, jnp.where(lane < n, -di, di)
            re2, im2 = re * re - im * im, 2.0 * re * im
            dr, di = re2, jnp.where(lane < n, -im2, im2)
            shift *= 2
        hprev = jnp.where(row >= 1, pltpu.roll(h, 1, 0), 0.0)
        yg_ref[gg] = (jnp.dot(u.astype(BF16), m_ref[gg], preferred_element_type=F32)
                      + jnp.dot(hprev.astype(BF16), q_ref[gg], preferred_element_type=F32))
        for b in range(nb):
            hf_ref[gg, b:b + 1, :] = h[(b + 1) * nk - 1:(b + 1) * nk, :]
        return 0

    lax.fori_loop(0, gpb, group_body, 0)
    for j in range(L):
        y_ref[pl.ds(j, r, stride=L), :] = jnp.concatenate(
            [yg_ref[gg, :, j * cdim:(j + 1) * cdim] for gg in range(gpb)], axis=1)


def ssm_prompt(u, m, p, q, al_re, al_im, nb):
    rows, d = u.shape
    g, w, n2 = p.shape
    gpb = LANES // SSM_GROUP
    r = rows // SSM_CHUNK
    gmap = lambda i: (i, 0, 0)
    return pl.pallas_call(
        functools.partial(_ssm_kernel, nb=nb),
        grid=(d // LANES,),
        in_specs=[pl.BlockSpec((rows, LANES), lambda i: (0, i)),
                  pl.BlockSpec((gpb, w, w), gmap), pl.BlockSpec((gpb, w, n2), gmap), pl.BlockSpec((gpb, n2, w), gmap),
                  pl.BlockSpec((gpb, 1, n2 // 2), gmap), pl.BlockSpec((gpb, 1, n2 // 2), gmap)],
        out_specs=[pl.BlockSpec((rows, LANES), lambda i: (0, i)), pl.BlockSpec((gpb, nb, n2), gmap)],
        out_shape=[jax.ShapeDtypeStruct((rows, d), F32), jax.ShapeDtypeStruct((g, nb, n2), F32)],
        scratch_shapes=[pltpu.VMEM((gpb, r, w), F32), pltpu.VMEM((gpb, r, w), F32)],
        compiler_params=_params(("parallel",)),
        name="ssm_prompt",
    )(u, m, p, q, al_re.reshape(g, 1, -1), al_im.reshape(g, 1, -1))


def _ssm_step_kernel(u_ref, hr_ref, hi_ref, ar_ref, ai_ref, br_ref, bi_ref, cr_ref, ci_ref, d_ref,
                     y_ref, or_ref, oi_ref):
    u = u_ref[0]
    bur = jnp.zeros(hr_ref.shape[1:], F32)
    bui = jnp.zeros(hr_ref.shape[1:], F32)
    for c in range(SSM_GROUP):
        uc = u[:, c:c + 1]
        bur = bur + br_ref[c] * uc
        bui = bui + bi_ref[c] * uc
    ar, ai = ar_ref[...], ai_ref[...]
    h0r, h0i = hr_ref[0], hi_ref[0]
    hr = ar * h0r - ai * h0i + bur
    hi = ar * h0i + ai * h0r + bui
    or_ref[0] = hr
    oi_ref[0] = hi
    lane = lax.broadcasted_iota(I32, u.shape, 1)
    y = d_ref[...] * u
    for c in range(SSM_GROUP):
        col = jnp.sum(cr_ref[c] * hr - ci_ref[c] * hi, axis=1, keepdims=True)
        y = y + jnp.where(lane == c, col, 0.0)
    y_ref[0] = y


def ssm_step(u, h_re, h_im, a_bar, bs, c_re, c_im, d_skip):
    b, g, c = u.shape
    n = h_re.shape[2]
    per_b = lambda i: (i, 0, 0)
    c3 = lambda i: (0, 0, 0)
    c2 = lambda i: (0, 0)
    return pl.pallas_call(
        _ssm_step_kernel,
        grid=(b,),
        in_specs=[pl.BlockSpec((1, g, c), per_b), pl.BlockSpec((1, g, n), per_b), pl.BlockSpec((1, g, n), per_b),
                  pl.BlockSpec((g, n), c2), pl.BlockSpec((g, n), c2),
                  pl.BlockSpec((c, g, n), c3), pl.BlockSpec((c, g, n), c3),
                  pl.BlockSpec((c, g, n), c3), pl.BlockSpec((c, g, n), c3),
                  pl.BlockSpec((g, c), c2)],
        out_specs=[pl.BlockSpec((1, g, c), per_b), pl.BlockSpec((1, g, n), per_b), pl.BlockSpec((1, g, n), per_b)],
        out_shape=[jax.ShapeDtypeStruct((b, g, c), F32), jax.ShapeDtypeStruct((b, g, n), F32),
                   jax.ShapeDtypeStruct((b, g, n), F32)],
        compiler_params=_params(("parallel",)),
        name="ssm_step",
    )(u, h_re, h_im, a_bar[0], a_bar[1], bs[0].transpose(2, 0, 1), bs[1].transpose(2, 0, 1),
      c_re.transpose(1, 0, 2), c_im.transpose(1, 0, 2), d_skip.reshape(g, c))


def _dsa_layer(xp, xs, gain, cache_k, cache_v, cache_ik, layer, page_table, w_in, w_out, q_gain, k_gain, ik_gain,
               bp, t, past_len, tm):
    mp, ms = xp.shape[0], xs.shape[0]
    nkv = N_KV * HEAD_DIM
    nq = t // TQ
    proj = norm_proj(xp, gain, w_in, tm)
    q, k, v, ik, iw, kh, vt, ikb, iq_t = dsa_post(proj, jnp.arange(t), q_gain, k_gain, ik_gain, KC, True)
    iw_t = iw.reshape(bp, nq, TQ, N_IDX_HEADS).transpose(0, 1, 3, 2)
    o = dsa_attention(iq_t.reshape(bp, nq * IDX_DIM, N_IDX_HEADS * TQ), iw_t, ikb.reshape(bp, t, IDX_DIM),
                      q.reshape(bp, t, -1), kh.reshape(N_KV, bp, t, HEAD_DIM),
                      vt.reshape(N_KV, bp, t // KC, HEAD_DIM, KC))
    xp = out_res(xp, o.reshape(mp, -1), w_out, tm)
    projs = norm_proj(xs, gain, w_in, ms)
    pos_s = jnp.full((ms,), past_len, I32)
    qs, ks, vs, iks, iws, iqs = dsa_post(projs, pos_s, q_gain, k_gain, ik_gain, ms, False)
    bias, bias_n = dsa_sample_index(page_table, iqs.reshape(ms, N_IDX_HEADS, IDX_DIM),
                                    iws.reshape(ms, N_IDX_HEADS, 1), cache_ik.transpose(0, 1, 3, 2), layer,
                                    iks.reshape(ms, 1, IDX_DIM))
    bias_nh = jnp.broadcast_to(bias_n[:, :, :1], (ms, N_HEADS, 1))
    os_ = paged_decode_attention(page_table, qs.reshape(ms, N_HEADS, HEAD_DIM), cache_k, cache_v, layer, bias,
                                 ks, vs, bias_nh)
    xs = out_res(xs, os_.reshape(ms, -1), w_out, ms)
    return xp, xs, (k, v, ik), (ks, vs, iks)


def _key_bias_columns(parts, bp, t):
    cols = jnp.stack(parts, axis=-1).reshape(bp, t, N_KV, GROUP * 3).transpose(0, 2, 1, 3)
    return jnp.pad(cols, ((0, 0), (0, 0), (0, 0), (0, HEAD_DIM - GROUP * 3)))


def _fox_layer(xp, xs, gain, cache_k, cache_v, cache_lf, layer, page_table, w_in, b_f, w_out, q_gain, k_gain,
               bp, t, tm):
    mp, ms = xp.shape[0], xs.shape[0]
    nkv = N_KV * HEAD_DIM
    proj = norm_proj(xp, gain, w_in, tm)
    q, k, v, lf, kh, vt = fox_post(proj, b_f, q_gain, k_gain, KC, True)
    c_parts = cumsum_rows_split(lf.reshape(bp, t, N_HEADS))
    o = fox_attention(q.reshape(bp, t, -1), kh.reshape(N_KV, bp, t, HEAD_DIM), _key_bias_columns(c_parts, bp, t),
                      vt.reshape(N_KV, bp, t // KC, HEAD_DIM, KC))
    xp = out_res(xp, o.reshape(mp, -1), w_out, tm)
    projs = norm_proj(xs, gain, w_in, ms)
    qs, ks, vs, lfs = fox_post(projs, b_f, q_gain, k_gain, ms, False)
    bias = fox_sample_bias(page_table, cache_lf.transpose(0, 1, 3, 2), layer, lfs.reshape(ms, N_HEADS, 1))
    os_ = paged_decode_attention(page_table, qs.reshape(ms, N_HEADS, HEAD_DIM), cache_k, cache_v, layer, bias,
                                 ks, vs, jnp.zeros((ms, N_HEADS, 1), F32))
    xs = out_res(xs, os_.reshape(ms, -1), w_out, ms)
    return xp, xs, (k, v, lf), (ks, vs, lfs)


def _ssm_layer(xp, xs, gain, h0_re, h0_im, a_re, a_im, log_dt, b_re, b_im, c_re, c_im, d_skip, w_glu, bp, t, tm):
    mp, ms = xp.shape[0], xs.shape[0]
    g, n, cdim, L = N_SSM_GROUPS, SSM_STATE, SSM_GROUP, SSM_CHUNK
    m, p, q, al_re, al_im, a_bar, bs = _ssm_tables(a_re, a_im, log_dt, b_re, b_im, c_re, c_im, d_skip)
    u = rmsnorm(xp, gain, tm)
    y, hf = ssm_prompt(u, m, p, q, al_re, al_im, bp)
    xp = glu_res(xp, y, w_glu, tm)
    hf = hf.transpose(1, 0, 2)
    us = rmsnorm(xs, gain, ms)
    ys, hs_re, hs_im = ssm_step(us.reshape(ms, g, cdim), h0_re, h0_im, a_bar, bs, c_re, c_im, d_skip)
    xs = glu_res(xs, ys.reshape(ms, g * cdim), w_glu, ms)
    return xp, xs, (hf[..., :n], hf[..., n:]), (hs_re, hs_im)


def kernel(x_prompt, x_sample, cache_dsa_k, cache_dsa_v, cache_dsa_idx_k, cache_fox_k, cache_fox_v, cache_fox_logf, state_ssm_re, state_ssm_im, page_table, norm_mix, norm_ffn, dsa_w_in, dsa_w_out, dsa_q_gain, dsa_k_gain, dsa_ik_gain, fox_w_in, fox_b_f, fox_w_out, fox_q_gain, fox_k_gain, ssm_a_re, ssm_a_im, ssm_log_dt, ssm_b_re, ssm_b_im, ssm_c_re, ssm_c_im, ssm_d, ssm_w_glu, ffn_w_gu, ffn_w_down):
    bp, t, d = x_prompt.shape
    bs_, ts, _ = x_sample.shape
    assert ts == 1, "the sample group carries one new position per sequence"
    assert t % KC == 0 and KC % TQ == 0
    depth = norm_mix.shape[0]
    past_len = page_table.shape[1] * PAGE
    tm = min(512, bp * t)
    xp = x_prompt.reshape(bp * t, d)
    xs = x_sample.reshape(bs_ * ts, d)
    bf = lambda w: w.astype(BF16)
    dsa_p, dsa_s, fox_p, fox_s, ssm_p, ssm_s = [], [], [], [], [], []
    for i in range(depth):
        kind, j = i % N_MIXERS, i // N_MIXERS
        if kind == 0:
            xp, xs, outp, outs = _dsa_layer(
                xp, xs, norm_mix[i], cache_dsa_k, cache_dsa_v, cache_dsa_idx_k, j, page_table,
                bf(dsa_w_in[j]), bf(dsa_w_out[j]), dsa_q_gain[j], dsa_k_gain[j], dsa_ik_gain[j], bp, t, past_len, tm)
            dsa_p.append(outp); dsa_s.append(outs)
        elif kind == 1:
            xp, xs, outp, outs = _fox_layer(
                xp, xs, norm_mix[i], cache_fox_k, cache_fox_v, cache_fox_logf, j, page_table,
                bf(fox_w_in[j]), fox_b_f[j], bf(fox_w_out[j]), fox_q_gain[j], fox_k_gain[j], bp, t, tm)
            fox_p.append(outp); fox_s.append(outs)
        else:
            xp, xs, outp, outs = _ssm_layer(
                xp, xs, norm_mix[i], state_ssm_re[j], state_ssm_im[j], ssm_a_re[j], ssm_a_im[j], ssm_log_dt[j],
                ssm_b_re[j], ssm_b_im[j], ssm_c_re[j], ssm_c_im[j], ssm_d[j], bf(ssm_w_glu[j]), bp, t, tm)
            ssm_p.append(outp); ssm_s.append(outs)
        w_gu, w_down = bf(ffn_w_gu[i]), bf(ffn_w_down[i])
        xp = ffn_res(xp, norm_ffn[i], w_gu, w_down, tm)
        xs = ffn_res(xs, norm_ffn[i], w_gu, w_down, bs_ * ts)

    def stack(items, idx, shape):
        return jnp.stack([it[idx].reshape(shape) for it in items])

    kv_p, kv_s = (bp, t, N_KV, HEAD_DIM), (bs_, ts, N_KV, HEAD_DIM)
    st_p, st_s = (bp, N_SSM_GROUPS, SSM_STATE), (bs_, N_SSM_GROUPS, SSM_STATE)
    return (xp.reshape(bp, t, d), xs.reshape(bs_, ts, d),
            stack(dsa_p, 0, kv_p), stack(dsa_p, 1, kv_p), stack(dsa_p, 2, (bp, t, IDX_DIM)),
            stack(dsa_s, 0, kv_s), stack(dsa_s, 1, kv_s), stack(dsa_s, 2, (bs_, ts, IDX_DIM)),
            stack(fox_p, 0, kv_p), stack(fox_p, 1, kv_p), stack(fox_p, 2, (bp, t, N_HEADS)),
            stack(fox_s, 0, kv_s), stack(fox_s, 1, kv_s), stack(fox_s, 2, (bs_, ts, N_HEADS)),
            stack(ssm_p, 0, st_p), stack(ssm_p, 1, st_p), stack(ssm_s, 0, st_s), stack(ssm_s, 1, st_s))
```

```python
import functools
import math

import jax
import jax.numpy as jnp
from jax import lax
from jax.experimental import pallas as pl
from jax.experimental.pallas import tpu as pltpu

F32 = jnp.float32
BF16 = jnp.bfloat16
I32 = jnp.int32

D_MODEL = 2048
N_HEADS = 16
HEAD_DIM = 128
N_KV = 4
GROUP = N_HEADS // N_KV
N_IDX_HEADS = 16
IDX_DIM = 64
TOPK_MAX = 256
ROPE_THETA = 500000.0
PAGE = 128
SSM_GROUP = 16
N_SSM_GROUPS = D_MODEL // SSM_GROUP
SSM_STATE = 64
SSM_CHUNK = 16
EPS = 1e-6
N_MIXERS = 3

LANES = 128
SUBLANES = 8
TQ = 128
KC = 256
DECODE_PAGES = 8
LOG2E = math.log2(math.e)
NEG = -1e30
BISECT_MAX = 64
HI = lax.Precision.HIGHEST
VMEM_LIMIT = 56 * 1024 * 1024

NT = (((1,), (1,)), ((), ()))


def _params(sem):
    return pltpu.CompilerParams(dimension_semantics=sem, vmem_limit_bytes=VMEM_LIMIT)


def _rms(x, gain):
    return x * lax.rsqrt(jnp.mean(x * x, axis=-1, keepdims=True) + EPS) * gain


def _proj_kernel(x_ref, g_ref, w_ref, o_ref, h_ref):
    @pl.when(pl.program_id(1) == 0)
    def _():
        h_ref[...] = _rms(x_ref[...], g_ref[...]).astype(BF16)

    o_ref[...] = jnp.dot(h_ref[...], w_ref[...], preferred_element_type=F32)


def _pick_tn(n, cap=1536):
    k = -(-n // cap)
    return -(-n // (k * LANES)) * LANES


def norm_proj(x, gain, w, layer, tm):
    m, d = x.shape
    n = w.shape[2]
    tn = _pick_tn(n)
    return pl.pallas_call(
        _proj_kernel,
        grid=(m // tm, pl.cdiv(n, tn)),
        in_specs=[pl.BlockSpec((tm, d), lambda i, j: (i, 0)),
                  pl.BlockSpec((1, d), lambda i, j: (0, 0)),
                  pl.BlockSpec((None, d, tn), lambda i, j: (layer, 0, j))],
        out_specs=pl.BlockSpec((tm, tn), lambda i, j: (i, j)),
        out_shape=jax.ShapeDtypeStruct((m, n), F32),
        scratch_shapes=[pltpu.VMEM((tm, d), BF16)],
        compiler_params=_params(("parallel", "arbitrary")),
        name="norm_proj",
    )(x, gain.reshape(1, d), w)


def _out_res_kernel(x_ref, o_ref, w_ref, y_ref):
    y_ref[...] = x_ref[...] + jnp.dot(o_ref[...].astype(BF16), w_ref[...], preferred_element_type=F32)


def out_res(x, o, w, layer, tm):
    m, d = x.shape
    k = w.shape[1]
    return pl.pallas_call(
        _out_res_kernel,
        grid=(m // tm,),
        in_specs=[pl.BlockSpec((tm, d), lambda i: (i, 0)),
                  pl.BlockSpec((tm, k), lambda i: (i, 0)),
                  pl.BlockSpec((None, k, d), lambda i: (layer, 0, 0))],
        out_specs=pl.BlockSpec((tm, d), lambda i: (i, 0)),
        out_shape=jax.ShapeDtypeStruct((m, d), F32),
        compiler_params=_params(("parallel",)),
        name="out_res",
    )(x, o, w)


def _ffn_kernel(x_ref, g_ref, wg_ref, wu_ref, wd_ref, y_ref, h_ref, acc_ref):
    f = pl.program_id(1)

    @pl.when(f == 0)
    def _():
        h_ref[...] = _rms(x_ref[...], g_ref[...]).astype(BF16)
        acc_ref[...] = jnp.zeros_like(acc_ref)

    h = h_ref[...]
    g = jnp.dot(h, wg_ref[...], preferred_element_type=F32)
    u = jnp.dot(h, wu_ref[...], preferred_element_type=F32)
    a = (g * jax.nn.sigmoid(g) * u).astype(BF16)
    acc_ref[...] += jnp.dot(a, wd_ref[...], preferred_element_type=F32)

    @pl.when(f == pl.num_programs(1) - 1)
    def _():
        y_ref[...] = x_ref[...] + acc_ref[...]


def ffn_res(x, gain, w_gu, w_down, layer, tm, tf=512):
    m, d = x.shape
    dff = w_down.shape[1]
    nf = dff // tf
    return pl.pallas_call(
        _ffn_kernel,
        grid=(m // tm, nf),
        in_specs=[pl.BlockSpec((tm, d), lambda i, f: (i, 0)),
                  pl.BlockSpec((1, d), lambda i, f: (0, 0)),
                  pl.BlockSpec((None, d, tf), lambda i, f: (layer, 0, f)),
                  pl.BlockSpec((None, d, tf), lambda i, f: (layer, 0, f + nf)),
                  pl.BlockSpec((None, tf, d), lambda i, f: (layer, f, 0))],
        out_specs=pl.BlockSpec((tm, d), lambda i, f: (i, 0)),
        out_shape=jax.ShapeDtypeStruct((m, d), F32),
        scratch_shapes=[pltpu.VMEM((tm, d), BF16), pltpu.VMEM((tm, d), F32)],
        compiler_params=_params(("parallel", "arbitrary")),
        name="ffn_res",
    )(x, gain.reshape(1, d), w_gu, w_gu, w_down)


def _gelu_tanh(y):
    return 0.5 * y * (1.0 + jnp.tanh(math.sqrt(2.0 / math.pi) * (y + 0.044715 * (y * y * y))))


def _glu_kernel(x_ref, y_ref, wa_ref, wb_ref, o_ref, g_ref):
    @pl.when(pl.program_id(1) == 0)
    def _():
        g_ref[...] = _gelu_tanh(y_ref[...]).astype(BF16)

    g = g_ref[...]
    a = jnp.dot(g, wa_ref[...], preferred_element_type=F32)
    b = jnp.dot(g, wb_ref[...], preferred_element_type=F32)
    o_ref[...] = x_ref[...] + a * jax.nn.sigmoid(b)


def glu_res(x, y, w_glu, layer, tm, tn=512):
    m, d = x.shape
    nj = d // tn
    return pl.pallas_call(
        _glu_kernel,
        grid=(m // tm, nj),
        in_specs=[pl.BlockSpec((tm, tn), lambda i, j: (i, j)),
                  pl.BlockSpec((tm, d), lambda i, j: (i, 0)),
                  pl.BlockSpec((None, d, tn), lambda i, j: (layer, 0, j)),
                  pl.BlockSpec((None, d, tn), lambda i, j: (layer, 0, j + nj))],
        out_specs=pl.BlockSpec((tm, tn), lambda i, j: (i, j)),
        out_shape=jax.ShapeDtypeStruct((m, d), F32),
        scratch_shapes=[pltpu.VMEM((tm, d), BF16)],
        compiler_params=_params(("parallel", "arbitrary")),
        name="glu_res",
    )(x, y, w_glu, w_glu)


def _rmsnorm_kernel(x_ref, g_ref, o_ref):
    o_ref[...] = _rms(x_ref[...], g_ref[...])


def rmsnorm(x, gain, tm):
    m, d = x.shape
    return pl.pallas_call(
        _rmsnorm_kernel,
        grid=(m // tm,),
        in_specs=[pl.BlockSpec((tm, d), lambda i: (i, 0)), pl.BlockSpec((1, d), lambda i: (0, 0))],
        out_specs=pl.BlockSpec((tm, d), lambda i: (i, 0)),
        out_shape=jax.ShapeDtypeStruct((m, d), F32),
        compiler_params=_params(("parallel",)),
        name="rmsnorm",
    )(x, gain.reshape(1, d))


def _rope_tables(pos, width, period):
    rot = period // 4
    half = rot // 2
    inv = ROPE_THETA ** (-jnp.arange(half, dtype=F32) / half)
    ang = pos.astype(F32)[:, None] * inv[None, :]
    cos, sin = jnp.cos(ang), jnp.sin(ang)
    n = pos.shape[0]
    ones = jnp.ones((n, period - rot), F32)
    zeros = jnp.zeros((n, period - rot), F32)
    zh = jnp.zeros((n, half), F32)
    c = jnp.concatenate([cos, cos, ones], axis=1)
    s_up = jnp.concatenate([zh, sin, zeros], axis=1)
    s_dn = jnp.concatenate([-sin, zh, zeros], axis=1)
    reps = width // period
    return tuple(jnp.tile(a, (1, reps)) for a in (c, s_up, s_dn))


def _rope(x, cos, s_up, s_dn, half):
    w = x.shape[-1]
    return x * cos + pltpu.roll(x, half, 1) * s_up + pltpu.roll(x, w - half, 1) * s_dn


def _store_kv_heads(p_ref, k0, v0, k_fn, k_ref, v_ref, kh_ref, vt_ref):
    for h in range(N_KV):
        y = k_fn(p_ref[:, k0 + h * HEAD_DIM:k0 + (h + 1) * HEAD_DIM])
        k_ref[:, h * HEAD_DIM:(h + 1) * HEAD_DIM] = y
        if kh_ref is not None:
            kh_ref[h] = y.astype(BF16)
            vh = p_ref[:, v0 + h * HEAD_DIM:v0 + (h + 1) * HEAD_DIM]
            vt_ref[h, 0] = vh.T.astype(BF16)
    v_ref[...] = p_ref[:, v0:v0 + N_KV * HEAD_DIM]


def _dsa_post_kernel(p_ref, qg_ref, kg_ref, ikg_ref, c1_ref, u1_ref, d1_ref, c2_ref, u2_ref, d2_ref,
                     q_ref, k_ref, v_ref, ik_ref, iw_ref, *mode_refs, prompt):
    c1, u1, d1 = c1_ref[...], u1_ref[...], d1_ref[...]
    c2, u2, d2 = c2_ref[...], u2_ref[...], d2_ref[...]
    qg, kg = qg_ref[...], kg_ref[...]
    scale = HEAD_DIM ** -0.5 * LOG2E
    for h in range(N_HEADS):
        x = p_ref[:, h * HEAD_DIM:(h + 1) * HEAD_DIM]
        y = _rope(_rms(x, qg), c1, u1, d1, HEAD_DIM // 8)
        q_ref[:, h * HEAD_DIM:(h + 1) * HEAD_DIM] = (y * scale).astype(BF16)
    k0 = N_HEADS * HEAD_DIM
    v0 = k0 + N_KV * HEAD_DIM
    kh_ref, vt_ref, ikb_ref, iqt_ref = mode_refs if prompt else (None, None, None, None)
    iq_ref = None if prompt else mode_refs[0]
    _store_kv_heads(p_ref, k0, v0, lambda x: _rope(_rms(x, kg), c1, u1, d1, HEAD_DIM // 8),
                    k_ref, v_ref, kh_ref, vt_ref)
    i0 = v0 + N_KV * HEAD_DIM
    tm = p_ref.shape[0]
    for h in range(N_IDX_HEADS * IDX_DIM // LANES):
        x = _rope(p_ref[:, i0 + h * LANES:i0 + (h + 1) * LANES], c2, u2, d2, IDX_DIM // 8)
        if prompt:
            for qb in range(tm // TQ):
                xt = x[qb * TQ:(qb + 1) * TQ].T.astype(BF16)
                iqt_ref[qb, :, (2 * h) * TQ:(2 * h + 1) * TQ] = xt[:IDX_DIM]
                iqt_ref[qb, :, (2 * h + 1) * TQ:(2 * h + 2) * TQ] = xt[IDX_DIM:]
        else:
            iq_ref[:, h * LANES:(h + 1) * LANES] = x.astype(BF16)
    j0 = i0 + N_IDX_HEADS * IDX_DIM
    xk = _rms(p_ref[:, j0:j0 + IDX_DIM], ikg_ref[...])
    xk2 = jnp.concatenate([xk, xk], axis=1)
    ik = _rope(xk2, c2, u2, d2, IDX_DIM // 8)[:, :IDX_DIM]
    ik_ref[...] = ik
    if prompt:
        ikb_ref[...] = ik.astype(BF16)
    iw_ref[...] = p_ref[:, j0 + IDX_DIM:j0 + IDX_DIM + N_IDX_HEADS] * (N_IDX_HEADS ** -0.5 * IDX_DIM ** -0.5)


def dsa_post(proj, pos, q_gain, k_gain, ik_gain, tm, prompt):
    m, n = proj.shape
    npos = pos.shape[0]
    t1 = _rope_tables(pos, HEAD_DIM, HEAD_DIM)
    t2 = _rope_tables(pos, LANES, IDX_DIM)
    nb = npos // tm
    row = lambda i: (i, 0)
    tab = lambda i: (i % nb, 0)
    const = lambda i: (0, 0)
    nkv = N_KV * HEAD_DIM
    niq = N_IDX_HEADS * IDX_DIM
    outs = [((m, N_HEADS * HEAD_DIM), BF16, pl.BlockSpec((tm, N_HEADS * HEAD_DIM), row)),
            ((m, nkv), F32, pl.BlockSpec((tm, nkv), row)),
            ((m, nkv), F32, pl.BlockSpec((tm, nkv), row)),
            ((m, IDX_DIM), F32, pl.BlockSpec((tm, IDX_DIM), row)),
            ((m, N_IDX_HEADS), F32, pl.BlockSpec((tm, N_IDX_HEADS), row))]
    if not prompt:
        outs += [((m, niq), BF16, pl.BlockSpec((tm, niq), row))]
    if prompt:
        assert tm == KC and tm % TQ == 0
        outs += [((N_KV, m, HEAD_DIM), BF16, pl.BlockSpec((N_KV, tm, HEAD_DIM), lambda i: (0, i, 0))),
                 ((N_KV, m // KC, HEAD_DIM, KC), BF16, pl.BlockSpec((N_KV, 1, HEAD_DIM, KC), lambda i: (0, i, 0, 0))),
                 ((m, IDX_DIM), BF16, pl.BlockSpec((tm, IDX_DIM), row)),
                 ((m // TQ, IDX_DIM, N_IDX_HEADS * TQ), BF16,
                  pl.BlockSpec((tm // TQ, IDX_DIM, N_IDX_HEADS * TQ), lambda i: (i, 0, 0)))]
    return pl.pallas_call(
        functools.partial(_dsa_post_kernel, prompt=prompt),
        grid=(m // tm,),
        in_specs=[pl.BlockSpec((tm, n), row),
                  pl.BlockSpec((1, HEAD_DIM), const), pl.BlockSpec((1, HEAD_DIM), const),
                  pl.BlockSpec((1, IDX_DIM), const)]
                 + [pl.BlockSpec((tm, LANES), tab)] * 6,
        out_specs=[o[2] for o in outs],
        out_shape=[jax.ShapeDtypeStruct(o[0], o[1]) for o in outs],
        compiler_params=_params(("parallel",)),
        name="dsa_post",
    )(proj, q_gain.reshape(1, -1), k_gain.reshape(1, -1), ik_gain.reshape(1, -1), *t1, *t2)


def _log_sigmoid(x):
    return jnp.minimum(x, 0.0) - jnp.log1p(jnp.exp(-jnp.abs(x)))


def _fox_post_kernel(p_ref, qg_ref, kg_ref, bf_ref, q_ref, k_ref, v_ref, lf_ref, *prompt_refs, prompt):
    qg, kg = qg_ref[...], kg_ref[...]
    scale = HEAD_DIM ** -0.5 * LOG2E
    for h in range(N_HEADS):
        x = p_ref[:, h * HEAD_DIM:(h + 1) * HEAD_DIM]
        q_ref[:, h * HEAD_DIM:(h + 1) * HEAD_DIM] = (_rms(x, qg) * scale).astype(BF16)
    k0 = N_HEADS * HEAD_DIM
    v0 = k0 + N_KV * HEAD_DIM
    kh_ref, vt_ref = prompt_refs if prompt else (None, None)
    _store_kv_heads(p_ref, k0, v0, lambda x: _rms(x, kg), k_ref, v_ref, kh_ref, vt_ref)
    f0 = v0 + N_KV * HEAD_DIM
    lf_ref[...] = _log_sigmoid(p_ref[:, f0:f0 + N_HEADS] + bf_ref[...])


def fox_post(proj, b_f, q_gain, k_gain, tm, prompt):
    m, n = proj.shape
    row = lambda i: (i, 0)
    const = lambda i: (0, 0)
    nkv = N_KV * HEAD_DIM
    outs = [((m, N_HEADS * HEAD_DIM), BF16, pl.BlockSpec((tm, N_HEADS * HEAD_DIM), row)),
            ((m, nkv), F32, pl.BlockSpec((tm, nkv), row)),
            ((m, nkv), F32, pl.BlockSpec((tm, nkv), row)),
            ((m, N_HEADS), F32, pl.BlockSpec((tm, N_HEADS), row))]
    if prompt:
        assert tm == KC
        outs += [((N_KV, m, HEAD_DIM), BF16, pl.BlockSpec((N_KV, tm, HEAD_DIM), lambda i: (0, i, 0))),
                 ((N_KV, m // KC, HEAD_DIM, KC), BF16, pl.BlockSpec((N_KV, 1, HEAD_DIM, KC), lambda i: (0, i, 0, 0)))]
    return pl.pallas_call(
        functools.partial(_fox_post_kernel, prompt=prompt),
        grid=(m // tm,),
        in_specs=[pl.BlockSpec((tm, n), row), pl.BlockSpec((1, HEAD_DIM), const),
                  pl.BlockSpec((1, HEAD_DIM), const), pl.BlockSpec((1, N_HEADS), const)],
        out_specs=[o[2] for o in outs],
        out_shape=[jax.ShapeDtypeStruct(o[0], o[1]) for o in outs],
        compiler_params=_params(("parallel",)),
        name="fox_post",
    )(proj, q_gain.reshape(1, -1), k_gain.reshape(1, -1), b_f.reshape(1, -1))


def _queries_t(q_ref, j):
    cols = []
    for g in range(GROUP):
        x = q_ref[0, :, (GROUP * j + g) * HEAD_DIM:(GROUP * j + g + 1) * HEAD_DIM]
        cols.append(x.astype(F32).T.astype(BF16))
    return jnp.concatenate(cols, axis=1)


def _attend_heads(s_first, logits_fn, values_fn, m_ref, l_ref, acc_ref, next_first_fn=None):
    s_next = s_first
    pending = None
    for j in range(N_KV):
        s_t = s_next
        if j + 1 < N_KV:
            s_next = logits_fn(j + 1)
        elif next_first_fn is not None:
            s_next = next_first_fn()
        m_run = m_ref[j]
        m_new = jnp.maximum(m_run, jnp.max(s_t, axis=0, keepdims=True))
        alpha = jnp.exp2(m_run - m_new)
        p = jnp.exp2(s_t - m_new)
        l_ref[j] = alpha * l_ref[j] + jnp.sum(p, axis=0, keepdims=True)
        m_ref[j] = m_new
        pv = jnp.dot(values_fn(j), p.astype(BF16), preferred_element_type=F32)
        if pending is not None:
            jp, alpha_p, pv_p = pending
            acc_ref[jp] = alpha_p * acc_ref[jp] + pv_p
        pending = (j, alpha, pv)
    jp, alpha_p, pv_p = pending
    acc_ref[jp] = alpha_p * acc_ref[jp] + pv_p
    return s_next


def _softmax_reset(m_ref, l_ref, acc_ref):
    m_ref[...] = jnp.full_like(m_ref, NEG)
    l_ref[...] = jnp.zeros_like(l_ref)
    acc_ref[...] = jnp.zeros_like(acc_ref)


def _softmax_scratch(qk_depth):
    rows = GROUP * TQ
    return [pltpu.VMEM((N_KV, qk_depth, rows), BF16), pltpu.VMEM((N_KV, 1, rows), F32),
            pltpu.VMEM((N_KV, 1, rows), F32), pltpu.VMEM((N_KV, HEAD_DIM, rows), F32)]


def _store_heads_t(o_ref, l_ref, acc_ref):
    for j in range(N_KV):
        o_t = acc_ref[j] / l_ref[j]
        for g in range(GROUP):
            o_ref[0, :, (GROUP * j + g) * HEAD_DIM:(GROUP * j + g + 1) * HEAD_DIM] = (
                o_t[:, g * TQ:(g + 1) * TQ].T.astype(o_ref.dtype))


def _cumsum_kernel(lf_ref, hi_ref, mid_ref, lo_ref):
    t = lf_ref.shape[1]
    r = lax.broadcasted_iota(I32, (LANES, LANES), 0)
    c = lax.broadcasted_iota(I32, (LANES, LANES), 1)
    lower = (c <= r).astype(F32)
    carry = jnp.zeros((1, lf_ref.shape[2]), F32)
    for b in range(t // LANES):
        rows = slice(b * LANES, (b + 1) * LANES)
        cs = jnp.dot(lower, lf_ref[0, rows, :], precision=HI, preferred_element_type=F32) + carry
        carry = cs[LANES - 1:LANES, :]
        x = cs * LOG2E
        hi = x.astype(BF16)
        r1 = x - hi.astype(F32)
        mid = r1.astype(BF16)
        hi_ref[0, rows, :] = hi
        mid_ref[0, rows, :] = mid
        lo_ref[0, rows, :] = (r1 - mid.astype(F32)).astype(BF16)


def cumsum_rows_split(lf):
    b, t, h = lf.shape
    spec = pl.BlockSpec((1, t, h), lambda i: (i, 0, 0))
    return pl.pallas_call(
        _cumsum_kernel,
        grid=(b,),
        in_specs=[spec],
        out_specs=[spec] * 3,
        out_shape=[jax.ShapeDtypeStruct((b, t, h), BF16)] * 3,
        compiler_params=_params(("parallel",)),
        name="cumsum_rows_split",
    )(lf)


def _fox_kernel(q_ref, k_ref, cp_ref, v_ref, o_ref, qt_ref, m_ref, l_ref, acc_ref):
    i = pl.program_id(1)
    rows = GROUP * TQ
    last = (i * TQ) // KC
    r_io = lax.broadcasted_iota(I32, (HEAD_DIM, rows), 0)
    g_io = lax.broadcasted_iota(I32, (HEAD_DIM, rows), 1) // TQ
    sel = jnp.where((r_io >= 3 * g_io) & (r_io < 3 * g_io + 3), -1.0, 0.0).astype(BF16)
    for j in range(N_KV):
        qt_ref[j] = jnp.concatenate([_queries_t(q_ref, j), sel], axis=0)
    _softmax_reset(m_ref, l_ref, acc_ref)

    def logits(j, c):
        start = pl.multiple_of(c * KC, KC)
        kc = jnp.concatenate([k_ref[j, 0, pl.ds(start, KC), :], cp_ref[0, j, pl.ds(start, KC), :]], axis=1)
        return jnp.dot(kc, qt_ref[j], preferred_element_type=F32)

    def body(c, s_first):
        return _attend_heads(s_first, lambda j: logits(j, c), lambda j: v_ref[j, 0, c], m_ref, l_ref, acc_ref,
                             next_first_fn=lambda: logits(0, c + 1))

    s_first = lax.fori_loop(0, last, body, logits(0, 0))
    key_io = lax.broadcasted_iota(I32, (KC, rows), 0)
    q_io = lax.broadcasted_iota(I32, (KC, rows), 1) % TQ + i * TQ
    causal = key_io + last * KC <= q_io
    _attend_heads(jnp.where(causal, s_first, NEG), lambda j: jnp.where(causal, logits(j, last), NEG),
                  lambda j: v_ref[j, 0, last], m_ref, l_ref, acc_ref)
    _store_heads_t(o_ref, l_ref, acc_ref)


def fox_attention(q, kh, cparts, vt):
    b, t, _ = q.shape
    return pl.pallas_call(
        _fox_kernel,
        grid=(b, t // TQ),
        in_specs=[pl.BlockSpec((1, TQ, N_HEADS * HEAD_DIM), lambda bi, i: (bi, i, 0)),
                  pl.BlockSpec((N_KV, 1, t, HEAD_DIM), lambda bi, i: (0, bi, 0, 0)),
                  pl.BlockSpec((1, N_KV, t, HEAD_DIM), lambda bi, i: (bi, 0, 0, 0)),
                  pl.BlockSpec((N_KV, 1, t // KC, HEAD_DIM, KC), lambda bi, i: (0, bi, 0, 0, 0))],
        out_specs=pl.BlockSpec((1, TQ, N_HEADS * HEAD_DIM), lambda bi, i: (bi, i, 0)),
        out_shape=jax.ShapeDtypeStruct((b, t, N_HEADS * HEAD_DIM), BF16),
        scratch_shapes=_softmax_scratch(2 * HEAD_DIM),
        compiler_params=_params(("parallel", "arbitrary")),
        name="fox_attention",
    )(q, kh, cparts, vt)


def _all_true(flags):
    return (jnp.min(flags.astype(F32)) > 0.5).astype(I32)


def _bisect_threshold(count_ge, lo, hi, cnt_lo, nsel):
    steps = 2

    def cond(state):
        it, _, _, _, settled, _ = state
        return (it < BISECT_MAX // steps) & (settled == 0)

    def body(state):
        it, lo, hi, cnt_lo, _, _ = state
        stuck = jnp.zeros(lo.shape, I32)
        for _ in range(steps):
            mid = lo + (hi - lo) * 0.5
            stuck = jnp.where((mid <= lo) | (mid >= hi), 1, 0)
            cnt = count_ge(mid)
            up = cnt >= nsel
            lo = jnp.where(up, mid, lo)
            cnt_lo = jnp.where(up, cnt, cnt_lo)
            hi = jnp.where(up, hi, mid)
        exact = jnp.where(cnt_lo == nsel, 1, 0)
        return it + 1, lo, hi, cnt_lo, _all_true(jnp.maximum(exact, stuck)), _all_true(exact)

    init = (jnp.int32(0), lo, hi, cnt_lo, jnp.int32(0), jnp.int32(0))
    _, lo, _, cnt_lo, _, all_exact = lax.while_loop(cond, body, init)
    return lo, cnt_lo, all_exact


def _exact_threshold(min_ge, count_gt_and_next, lo, nsel):
    def cond(state):
        it, _, _, _, _, finished = state
        return (it < BISECT_MAX) & (finished == 0)

    def body(state):
        it, v, tau, need, done, _ = state
        n_gt, v_next = count_gt_and_next(v)
        fin = (n_gt < nsel) & (done == 0)
        tau = jnp.where(fin, v, tau)
        need = jnp.where(fin, nsel - n_gt, need)
        done = jnp.where(fin, 1, done)
        v = jnp.where(done == 1, v, v_next)
        return it + 1, v, tau, need, done, _all_true(done)

    v0 = min_ge(lo)
    init = (jnp.int32(0), v0, v0, jnp.zeros(lo.shape, F32), jnp.zeros(lo.shape, I32), jnp.int32(0))
    _, _, tau, need, _, _ = lax.while_loop(cond, body, init)
    return tau, need


def _select_tie_index(count_eq_below, need, nbits, shape):
    def bit_body(bi, m):
        cand = m | jnp.left_shift(jnp.int32(1), nbits - 1 - bi)
        return jnp.where(count_eq_below(cand) < need, cand, m)

    return lax.fori_loop(0, nbits, bit_body, jnp.zeros(shape, I32))


def _dsa_kernel(iq_ref, iw_ref, ik_ref, q_ref, k_ref, v_ref, o_ref, sc_ref, bias_ref, qt_ref, m_ref, l_ref, acc_ref,
                *, nsel, t_total):
    i = pl.program_id(1)
    nch = (i * TQ + TQ + KC - 1) // KC
    w = iw_ref[0, 0]
    iq_t = iq_ref[0]
    key_io = lax.broadcasted_iota(I32, (KC, TQ), 0)
    qpos = i * TQ + lax.broadcasted_iota(I32, (KC, TQ), 1)
    inf = float("inf")

    def fold(x, op):
        part = op(x.reshape(4, KC // (4 * SUBLANES), SUBLANES, TQ), axis=1)
        return op(part, axis=0)

    def score_body(c, carry):
        lo, hi = carry
        ikc = ik_ref[0, pl.ds(pl.multiple_of(c * KC, KC), KC), :]
        y = jnp.dot(ikc, iq_t, preferred_element_type=F32)
        acc = jnp.zeros((KC, TQ), F32)
        for h in range(N_IDX_HEADS):
            acc = acc + w[h:h + 1, :] * jnp.maximum(y[:, h * TQ:(h + 1) * TQ], 0.0)
        causal = key_io + c * KC <= qpos
        sc_ref[c] = jnp.where(causal, acc, -inf)
        return (jnp.minimum(lo, fold(jnp.where(causal, acc, inf), jnp.min)),
                jnp.maximum(hi, fold(jnp.where(causal, acc, -inf), jnp.max)))

    lo8, hi8 = lax.fori_loop(0, nch, score_body,
                             (jnp.full((SUBLANES, TQ), inf, F32), jnp.full((SUBLANES, TQ), -inf, F32)))

    def count(pred):
        def body(c, acc):
            return acc + fold(jnp.where(pred(sc_ref[c], c), 1.0, 0.0), jnp.sum)
        acc = lax.fori_loop(0, nch, body, jnp.zeros((SUBLANES, TQ), F32))
        return jnp.sum(acc, axis=0, keepdims=True)

    def select():
        lo = jnp.min(lo8, axis=0, keepdims=True)
        hi = jnp.max(hi8, axis=0, keepdims=True)
        n_causal = (qpos[:1] + 1).astype(F32)
        lo, _, all_exact = _bisect_threshold(lambda thr: count(lambda s, c: s >= thr), lo, hi, n_causal, float(nsel))

        def ties():
            def min_ge(lo):
                def body(c, acc):
                    s = sc_ref[c]
                    return jnp.minimum(acc, fold(jnp.where(s >= lo, s, inf), jnp.min))
                acc = lax.fori_loop(0, nch, body, jnp.full((SUBLANES, TQ), inf, F32))
                return jnp.min(acc, axis=0, keepdims=True)

            def count_gt_and_next(v):
                def body(c, carry):
                    n, nxt = carry
                    s = sc_ref[c]
                    above = s > v
                    return (n + fold(jnp.where(above, 1.0, 0.0), jnp.sum),
                            jnp.minimum(nxt, fold(jnp.where(above, s, inf), jnp.min)))
                n, nxt = lax.fori_loop(0, nch, body, (jnp.zeros((SUBLANES, TQ), F32),
                                                      jnp.full((SUBLANES, TQ), inf, F32)))
                return jnp.sum(n, axis=0, keepdims=True), jnp.min(nxt, axis=0, keepdims=True)

            tau, need = _exact_threshold(min_ge, count_gt_and_next, lo, float(nsel))
            nbits = max(1, int(t_total - 1).bit_length())
            m_star = _select_tie_index(
                lambda cand: count(lambda s, c: (s == tau) & (key_io + c * KC < cand)), need, nbits, (1, TQ))
            return tau, m_star

        return lax.cond(all_exact == 1, lambda: (lo, jnp.full((1, TQ), t_total, I32)), ties)

    tau, m_star = lax.cond(i * TQ + TQ <= nsel,
                           lambda: (jnp.full((1, TQ), -inf, F32), jnp.full((1, TQ), t_total, I32)), select)

    def bias_body(c, _):
        s = sc_ref[c]
        idx = key_io + c * KC
        keep = ((s > tau) | ((s == tau) & (idx <= m_star))) & (idx <= qpos)
        bias_ref[c] = jnp.where(keep, 0.0, NEG)
        return 0

    lax.fori_loop(0, nch, bias_body, 0)

    for j in range(N_KV):
        qt_ref[j] = _queries_t(q_ref, j)
    _softmax_reset(m_ref, l_ref, acc_ref)

    def logits(j, c):
        kc = k_ref[j, 0, pl.ds(pl.multiple_of(c * KC, KC), KC), :]
        bias = jnp.concatenate([bias_ref[c]] * GROUP, axis=1)
        return jnp.dot(kc, qt_ref[j], preferred_element_type=F32) + bias

    def body(c, s_first):
        return _attend_heads(s_first, lambda j: logits(j, c), lambda j: v_ref[j, 0, c], m_ref, l_ref, acc_ref,
                             next_first_fn=lambda: logits(0, jnp.minimum(c + 1, nch - 1)))

    lax.fori_loop(0, nch, body, logits(0, 0))
    _store_heads_t(o_ref, l_ref, acc_ref)


def dsa_attention(iq_t, iw_t, ikb, q, kh, vt):
    b, t, _ = q.shape
    nsel = min(TOPK_MAX, t // 4)
    nch = t // KC
    kern = functools.partial(_dsa_kernel, nsel=nsel, t_total=t)
    return pl.pallas_call(
        kern,
        grid=(b, t // TQ),
        in_specs=[pl.BlockSpec((1, IDX_DIM, N_IDX_HEADS * TQ), lambda bi, i: (bi, i, 0)),
                  pl.BlockSpec((1, 1, N_IDX_HEADS, TQ), lambda bi, i: (bi, i, 0, 0)),
                  pl.BlockSpec((1, t, IDX_DIM), lambda bi, i: (bi, 0, 0)),
                  pl.BlockSpec((1, TQ, N_HEADS * HEAD_DIM), lambda bi, i: (bi, i, 0)),
                  pl.BlockSpec((N_KV, 1, t, HEAD_DIM), lambda bi, i: (0, bi, 0, 0)),
                  pl.BlockSpec((N_KV, 1, nch, HEAD_DIM, KC), lambda bi, i: (0, bi, 0, 0, 0))],
        out_specs=pl.BlockSpec((1, TQ, N_HEADS * HEAD_DIM), lambda bi, i: (bi, i, 0)),
        out_shape=jax.ShapeDtypeStruct((b, t, N_HEADS * HEAD_DIM), BF16),
        scratch_shapes=[pltpu.VMEM((nch, KC, TQ), F32), pltpu.VMEM((nch, KC, TQ), F32)] + _softmax_scratch(HEAD_DIM),
        compiler_params=_params(("parallel", "arbitrary")),
        name="dsa_attention",
    )(iq_t, iw_t, ikb, q, kh, vt)


def _page_copy(pool_ref, layer, pt_ref, b, p, buf, slot, sem):
    return pltpu.make_async_copy(pool_ref.at[layer, pt_ref[b, p]], buf.at[slot, p], sem.at[slot])


def _fetch_sequence_pages(pool_ref, layer, pt_ref, buf, sem):
    b = pl.program_id(0)
    nb = pl.num_programs(0)
    npages = buf.shape[1]
    slot = b % 2

    def start_all(bb, sl):
        lax.fori_loop(0, npages, lambda p, _: (_page_copy(pool_ref, layer, pt_ref, bb, p, buf, sl, sem).start(), 0)[1], 0)

    @pl.when(b == 0)
    def _():
        start_all(0, 0)

    @pl.when(b + 1 < nb)
    def _():
        start_all(b + 1, 1 - slot)

    lax.fori_loop(0, npages, lambda p, _: (_page_copy(pool_ref, layer, pt_ref, b, p, buf, slot, sem).wait(), 0)[1], 0)
    return slot


def _dsa_sample_index_kernel(pt_ref, iq_ref, iw_ref, pool_ref, ikn_ref, bias_ref, biasn_ref, buf, sem, sc_ref,
                             *, layer, nsel):
    slot = _fetch_sequence_pages(pool_ref, layer, pt_ref, buf, sem)
    npages = buf.shape[1]
    iq = iq_ref[0]
    w = iw_ref[0]

    unroll = math.gcd(npages, 8)

    def score_body(pb, _):
        for k in range(unroll):
            p = pb * unroll + k
            y = jnp.dot(iq, buf[slot, p].astype(BF16), preferred_element_type=F32)
            sc_ref[pl.ds(p, 1), :] = jnp.sum(w * jnp.maximum(y, 0.0), axis=0, keepdims=True)
        return 0

    lax.fori_loop(0, npages // unroll, score_body, 0)
    prod = iq.astype(F32) * ikn_ref[0].astype(BF16).astype(F32)
    yn = jnp.sum(prod, axis=1, keepdims=True)
    sn = jnp.sum(w * jnp.maximum(yn, 0.0), axis=0, keepdims=True)
    sc = sc_ref[...]
    idx = lax.broadcasted_iota(I32, sc.shape, 0) * PAGE + lax.broadcasted_iota(I32, sc.shape, 1)
    inf = float("inf")

    def total(x, op=jnp.sum):
        return op(op(x, axis=0, keepdims=True), axis=1, keepdims=True)

    def count_ge(thr):
        return total(jnp.where(sc >= thr, 1.0, 0.0)) + jnp.where(sn >= thr, 1.0, 0.0)

    def min_ge(lo):
        return jnp.minimum(total(jnp.where(sc >= lo, sc, inf), jnp.min), jnp.where(sn >= lo, sn, inf))

    def count_gt_and_next(v):
        n = total(jnp.where(sc > v, 1.0, 0.0)) + jnp.where(sn > v, 1.0, 0.0)
        return n, jnp.minimum(total(jnp.where(sc > v, sc, inf), jnp.min), jnp.where(sn > v, sn, inf))

    lo = jnp.minimum(total(sc, jnp.min), sn)
    hi = jnp.maximum(total(sc, jnp.max), sn)
    lo, _, _ = _bisect_threshold(count_ge, lo, hi, jnp.full((1, 1), float(npages * PAGE + 1), F32), float(nsel))
    tau, need = _exact_threshold(min_ge, count_gt_and_next, lo, float(nsel))
    nbits = int(npages * PAGE).bit_length()
    m_star = _select_tie_index(
        lambda cand: total(jnp.where((sc == tau) & (idx < cand), 1.0, 0.0)), need, nbits, (1, 1))
    n_eq = total(jnp.where(sc == tau, 1.0, 0.0))
    keep = (sc > tau) | ((sc == tau) & (idx <= m_star))
    keep_rows = jnp.dot(jnp.where(keep, 1.0, 0.0).astype(BF16), _row_expansion().astype(BF16),
                        preferred_element_type=F32)
    bias_ref[0] = jnp.where(keep_rows > 0.5, 0.0, NEG)
    keep_n = (sn > tau) | ((sn == tau) & (n_eq < need))
    biasn_ref[0] = jnp.broadcast_to(jnp.where(keep_n, 0.0, NEG), (1, LANES))


def dsa_sample_index(page_table, iq, iw, pool_ik_t, layer, ik_new):
    b, npages = page_table.shape
    nsel = min(TOPK_MAX, (npages * PAGE + 1) // 4)
    per_b = lambda bi, pt: (bi, 0, 0)
    grid_spec = pltpu.PrefetchScalarGridSpec(
        num_scalar_prefetch=1,
        grid=(b,),
        in_specs=[pl.BlockSpec((1, N_IDX_HEADS, IDX_DIM), per_b),
                  pl.BlockSpec((1, N_IDX_HEADS, 1), per_b),
                  pl.BlockSpec(memory_space=pl.ANY),
                  pl.BlockSpec((1, 1, IDX_DIM), per_b)],
        out_specs=[pl.BlockSpec((1, npages, PAGE * N_KV), per_b), pl.BlockSpec((1, 1, LANES), per_b)],
        scratch_shapes=[pltpu.VMEM((2, npages, IDX_DIM, PAGE), F32), pltpu.SemaphoreType.DMA((2,)),
                        pltpu.VMEM((npages, PAGE), F32)])
    return pl.pallas_call(
        functools.partial(_dsa_sample_index_kernel, layer=layer, nsel=nsel),
        grid_spec=grid_spec,
        out_shape=[jax.ShapeDtypeStruct((b, npages, PAGE * N_KV), F32), jax.ShapeDtypeStruct((b, 1, LANES), F32)],
        compiler_params=_params(("arbitrary",)),
        name="dsa_sample_index",
    )(page_table, iq, iw, pool_ik_t, ik_new)


def _fox_sample_bias_kernel(pt_ref, pool_ref, lfn_ref, bias_ref, buf, sem, *, layer):
    slot = _fetch_sequence_pages(pool_ref, layer, pt_ref, buf, sem)
    npages, h = buf.shape[1], buf.shape[2]
    rows = npages * h
    x = buf[slot].reshape(rows, PAGE)
    r = lax.broadcasted_iota(I32, (PAGE, PAGE), 0)
    c = lax.broadcasted_iota(I32, (PAGE, PAGE), 1)
    within = jnp.dot(x, (r > c).astype(F32), precision=HI, preferred_element_type=F32)
    tot = jnp.dot(x, jnp.ones((PAGE, PAGE), F32), precision=HI, preferred_element_type=F32)
    row = lax.broadcasted_iota(I32, (rows, PAGE), 0)
    suffix = tot
    shift = h
    while shift < rows:
        suffix = suffix + jnp.where(row + shift < rows, pltpu.roll(suffix, rows - shift, 0), 0.0)
        shift *= 2
    per_tok = within + (suffix - tot)
    per_row = jnp.dot(per_tok, _row_expansion().astype(F32), precision=HI, preferred_element_type=F32)
    bias_ref[0] = (per_row.reshape(npages, h, PAGE * N_KV) + lfn_ref[0][None]) * LOG2E


def fox_sample_bias(page_table, pool_lf_t, layer, lf_new):
    b, npages = page_table.shape
    h = pool_lf_t.shape[2]
    grid_spec = pltpu.PrefetchScalarGridSpec(
        num_scalar_prefetch=1,
        grid=(b,),
        in_specs=[pl.BlockSpec(memory_space=pl.ANY),
                  pl.BlockSpec((1, h, 1), lambda bi, pt: (bi, 0, 0))],
        out_specs=pl.BlockSpec((1, npages, h, PAGE * N_KV), lambda bi, pt: (bi, 0, 0, 0)),
        scratch_shapes=[pltpu.VMEM((2, npages, h, PAGE), F32), pltpu.SemaphoreType.DMA((2,))])
    return pl.pallas_call(
        functools.partial(_fox_sample_bias_kernel, layer=layer),
        grid_spec=grid_spec,
        out_shape=jax.ShapeDtypeStruct((b, npages, h, PAGE * N_KV), F32),
        compiler_params=_params(("arbitrary",)),
        name="fox_sample_bias",
    )(page_table, pool_lf_t, lf_new)


def _decode_kernel(pt_ref, q_ref, pk_ref, pv_ref, b_ref, kn_ref, vn_ref, bn_ref, o_ref, kbuf, vbuf, sem,
                   *, layer, per_head_bias):
    b = pl.program_id(0)
    npages = b_ref.shape[1]
    pg = kbuf.shape[1]
    ngroups = npages // pg
    rows = PAGE * N_KV
    q = q_ref[0]
    row_kv = lax.broadcasted_iota(I32, (N_HEADS, rows), 1) % N_KV
    head_mask = jnp.where(row_kv == lax.broadcasted_iota(I32, (N_HEADS, rows), 0) // GROUP, 0.0, NEG)

    def copies(g, slot):
        out = []
        for i in range(pg):
            page = pt_ref[b, g * pg + i]
            out.append(pltpu.make_async_copy(pk_ref.at[layer, page], kbuf.at[slot, i], sem.at[0, slot]))
            out.append(pltpu.make_async_copy(pv_ref.at[layer, page], vbuf.at[slot, i], sem.at[1, slot]))
        return out

    for cp in copies(0, 0):
        cp.start()

    def body(g, carry):
        m_run, l_run, acc = carry
        slot = g % 2

        @pl.when(g + 1 < ngroups)
        def _():
            for cp in copies(g + 1, 1 - slot):
                cp.start()

        for cp in copies(g, slot):
            cp.wait()
        logits = []
        for i in range(pg):
            s = lax.dot_general(q, kbuf[slot, i].astype(BF16), NT, preferred_element_type=F32)
            page_bias = b_ref[0, g * pg + i] if per_head_bias else b_ref[0, pl.ds(g * pg + i, 1), :]
            logits.append(s + page_bias + head_mask)
        s_all = jnp.concatenate(logits, axis=1)
        m_new = jnp.maximum(m_run, jnp.max(s_all, axis=1, keepdims=True))
        alpha = jnp.exp2(m_run - m_new)
        p = jnp.exp2(s_all - m_new)
        l_new = alpha * l_run + jnp.sum(p, axis=1, keepdims=True)
        pv = jnp.zeros((N_HEADS, HEAD_DIM), F32)
        for i in range(pg):
            pb = p[:, i * rows:(i + 1) * rows].astype(BF16)
            pv = pv + jnp.dot(pb, vbuf[slot, i].astype(BF16), preferred_element_type=F32)
        return m_new, l_new, alpha * acc + pv

    init = (jnp.full((N_HEADS, 1), NEG, F32), jnp.zeros((N_HEADS, 1), F32), jnp.zeros((N_HEADS, HEAD_DIM), F32))
    m_run, l_run, acc = lax.fori_loop(0, ngroups, body, init)

    kn = kn_ref[0].astype(BF16).astype(F32)
    s_n = jnp.sum(q.astype(F32) * kn, axis=1, keepdims=True) + bn_ref[0]
    m_fin = jnp.maximum(m_run, s_n)
    alpha = jnp.exp2(m_run - m_fin)
    p_n = jnp.exp2(s_n - m_fin)
    l_fin = alpha * l_run + p_n
    vn = vn_ref[0].astype(BF16).astype(F32)
    o_ref[0] = (alpha * acc + p_n.astype(BF16).astype(F32) * vn) / l_fin


def paged_decode_attention(page_table, q, cache_k, cache_v, layer, bias, k_new, v_new, bias_new):
    b, npages = page_table.shape
    rows = PAGE * N_KV
    nl, npool = cache_k.shape[:2]
    per_head_bias = bias.ndim == 4
    bias_spec = (pl.BlockSpec((1, npages, N_HEADS, rows), lambda bi, pt: (bi, 0, 0, 0)) if per_head_bias
                 else pl.BlockSpec((1, npages, rows), lambda bi, pt: (bi, 0, 0)))
    pg = math.gcd(DECODE_PAGES, npages)
    per_b = lambda bi, pt: (bi, 0, 0)
    per_head = lambda x: jnp.repeat(x.reshape(b, N_KV, HEAD_DIM), GROUP, axis=1)
    grid_spec = pltpu.PrefetchScalarGridSpec(
        num_scalar_prefetch=1,
        grid=(b,),
        in_specs=[pl.BlockSpec((1, N_HEADS, HEAD_DIM), per_b),
                  pl.BlockSpec(memory_space=pl.ANY),
                  pl.BlockSpec(memory_space=pl.ANY),
                  bias_spec,
                  pl.BlockSpec((1, N_HEADS, HEAD_DIM), per_b),
                  pl.BlockSpec((1, N_HEADS, HEAD_DIM), per_b),
                  pl.BlockSpec((1, N_HEADS, 1), per_b)],
        out_specs=pl.BlockSpec((1, N_HEADS, HEAD_DIM), per_b),
        scratch_shapes=[pltpu.VMEM((2, pg, rows, HEAD_DIM), F32),
                        pltpu.VMEM((2, pg, rows, HEAD_DIM), F32),
                        pltpu.SemaphoreType.DMA((2, 2))])
    return pl.pallas_call(
        functools.partial(_decode_kernel, layer=layer, per_head_bias=per_head_bias),
        grid_spec=grid_spec,
        out_shape=jax.ShapeDtypeStruct((b, N_HEADS, HEAD_DIM), F32),
        compiler_params=_params(("arbitrary",)),
        name="paged_decode_attention",
    )(page_table, q, cache_k.reshape(nl, npool, rows, HEAD_DIM), cache_v.reshape(nl, npool, rows, HEAD_DIM),
      bias, per_head(k_new), per_head(v_new), bias_new)


def _row_expansion():
    t = lax.broadcasted_iota(I32, (PAGE, PAGE * N_KV), 0)
    r = lax.broadcasted_iota(I32, (PAGE, PAGE * N_KV), 1)
    return r // N_KV == t


def _cmul(ar, ai, br, bi):
    return ar * br - ai * bi, ar * bi + ai * br


def _ssm_tables(a_re, a_im, log_dt, b_re, b_im, c_re, c_im, d_skip):
    L, cdim = SSM_CHUNK, SSM_GROUP
    g = a_re.shape[0]
    dt = jnp.exp(log_dt)[:, None]
    mag = jnp.exp(a_re * dt)
    ab = (mag * jnp.cos(a_im * dt), mag * jnp.sin(a_im * dt))
    den = a_re * a_re + a_im * a_im
    num = _cmul(ab[0] - 1.0, ab[1], a_re, -a_im)
    sc = (num[0] / den, num[1] / den)
    bs = _cmul(b_re, b_im, sc[0][:, :, None], sc[1][:, :, None])
    pows = [(jnp.ones_like(ab[0]), jnp.zeros_like(ab[0]))]
    for _ in range(L):
        pows.append(_cmul(*pows[-1], *ab))
    pw = (jnp.stack([p[0] for p in pows]), jnp.stack([p[1] for p in pows]))
    ca = _cmul(c_re[None], c_im[None], pw[0][:L, :, None, :], pw[1][:L, :, None, :])
    def contract_n(x, y):
        return jnp.sum(x.transpose(0, 1, 3, 2)[..., None] * y[None, :, :, None, :], axis=2)

    kd = contract_n(ca[0], bs[0]) - contract_n(ca[1], bs[1])
    dmat = jnp.eye(cdim, dtype=F32)[None] * d_skip.reshape(g, cdim)[:, :, None]
    kd = kd.at[0].add(dmat)
    ii = jnp.arange(L)
    diff = ii[None, :] - ii[:, None]
    kt = jnp.where((diff >= 0)[:, :, None, None, None], kd[jnp.clip(diff, 0, L - 1)], 0.0)
    m = kt.transpose(2, 0, 4, 1, 3).reshape(g, L * cdim, L * cdim)
    bst = (bs[0].transpose(0, 2, 1)[None], bs[1].transpose(0, 2, 1)[None])
    pj = _cmul(pw[0][L - 1 - ii][:, :, None, :], pw[1][L - 1 - ii][:, :, None, :], *bst)
    p = jnp.concatenate([x.transpose(1, 0, 2, 3).reshape(g, L * cdim, -1) for x in pj], axis=-1)
    cp = _cmul(c_re[None], c_im[None], pw[0][1:L + 1][:, :, None, :], pw[1][1:L + 1][:, :, None, :])
    cp = [x.transpose(1, 3, 0, 2).reshape(g, -1, L * cdim) for x in cp]
    q = jnp.concatenate([cp[0], -cp[1]], axis=1)
    return m.astype(BF16), p, q.astype(BF16), pw[0][L], pw[1][L], ab, bs


def _ssm_kernel(u_ref, m_ref, p_ref, q_ref, ar_ref, ai_ref, y_ref, hf_ref, ug_ref, yg_ref, *, nb):
    L, cdim, n = SSM_CHUNK, SSM_GROUP, SSM_STATE
    gpb = LANES // cdim
    r = ug_ref.shape[1]
    nk = r // nb
    for j in range(L):
        xj = u_ref[pl.ds(j, r, stride=L), :]
        for gg in range(gpb):
            ug_ref[gg, :, j * cdim:(j + 1) * cdim] = xj[:, gg * cdim:(gg + 1) * cdim]
    row = lax.broadcasted_iota(I32, (r, 2 * n), 0) % nk
    lane = lax.broadcasted_iota(I32, (1, 2 * n), 1)

    def group_body(gg, _):
        u = ug_ref[gg]
        h = jnp.dot(u, p_ref[gg], precision=HI, preferred_element_type=F32)
        dr = jnp.concatenate([ar_ref[gg], ar_ref[gg]], axis=1)
        di = jnp.concatenate([-ai_ref[gg], ai_ref[gg]], axis=1)
        shift = 1
        while shift < nk:
            hs = jnp.where(row >= shift, pltpu.roll(h, shift, 0), 0.0)
            h = h + dr * hs + di * pltpu.roll(hs, n, 1)
            re, im = dr, jnp.where(lane < n, -di, di)
            re2, im2 = re * re - im * im, 2.0 * re * im
            dr, di = re2, jnp.where(lane < n, -im2, im2)
            shift *= 2
        hprev = jnp.where(row >= 1, pltpu.roll(h, 1, 0), 0.0)
        yg_ref[gg] = (jnp.dot(u.astype(BF16), m_ref[gg], preferred_element_type=F32)
                      + jnp.dot(hprev.astype(BF16), q_ref[gg], preferred_element_type=F32))
        for b in range(nb):
            hf_ref[gg, b:b + 1, :] = h[(b + 1) * nk - 1:(b + 1) * nk, :]
        return 0

    lax.fori_loop(0, gpb, group_body, 0)
    for j in range(L):
        y_ref[pl.ds(j, r, stride=L), :] = jnp.concatenate(
            [yg_ref[gg, :, j * cdim:(j + 1) * cdim] for gg in range(gpb)], axis=1)


def ssm_prompt(u, m, p, q, al_re, al_im, nb):
    rows, d = u.shape
    g, w, n2 = p.shape
    gpb = LANES // SSM_GROUP
    r = rows // SSM_CHUNK
    gmap = lambda i: (i, 0, 0)
    return pl.pallas_call(
        functools.partial(_ssm_kernel, nb=nb),
        grid=(d // LANES,),
        in_specs=[pl.BlockSpec((rows, LANES), lambda i: (0, i)),
                  pl.BlockSpec((gpb, w, w), gmap), pl.BlockSpec((gpb, w, n2), gmap), pl.BlockSpec((gpb, n2, w), gmap),
                  pl.BlockSpec((gpb, 1, n2 // 2), gmap), pl.BlockSpec((gpb, 1, n2 // 2), gmap)],
        out_specs=[pl.BlockSpec((rows, LANES), lambda i: (0, i)), pl.BlockSpec((gpb, nb, n2), gmap)],
        out_shape=[jax.ShapeDtypeStruct((rows, d), F32), jax.ShapeDtypeStruct((g, nb, n2), F32)],
        scratch_shapes=[pltpu.VMEM((gpb, r, w), F32), pltpu.VMEM((gpb, r, w), F32)],
        compiler_params=_params(("parallel",)),
        name="ssm_prompt",
    )(u, m, p, q, al_re.reshape(g, 1, -1), al_im.reshape(g, 1, -1))


def _ssm_step_kernel(u_ref, hr_ref, hi_ref, ar_ref, ai_ref, br_ref, bi_ref, cr_ref, ci_ref, d_ref,
                     y_ref, or_ref, oi_ref):
    u = u_ref[0]
    bur = jnp.zeros(hr_ref.shape[1:], F32)
    bui = jnp.zeros(hr_ref.shape[1:], F32)
    for c in range(SSM_GROUP):
        uc = u[:, c:c + 1]
        bur = bur + br_ref[c] * uc
        bui = bui + bi_ref[c] * uc
    ar, ai = ar_ref[...], ai_ref[...]
    h0r, h0i = hr_ref[0], hi_ref[0]
    hr = ar * h0r - ai * h0i + bur
    hi = ar * h0i + ai * h0r + bui
    or_ref[0] = hr
    oi_ref[0] = hi
    lane = lax.broadcasted_iota(I32, u.shape, 1)
    y = d_ref[...] * u
    for c in range(SSM_GROUP):
        col = jnp.sum(cr_ref[c] * hr - ci_ref[c] * hi, axis=1, keepdims=True)
        y = y + jnp.where(lane == c, col, 0.0)
    y_ref[0] = y


def ssm_step(u, h_re, h_im, a_bar, bs, c_re, c_im, d_skip):
    b, g, c = u.shape
    n = h_re.shape[2]
    per_b = lambda i: (i, 0, 0)
    c3 = lambda i: (0, 0, 0)
    c2 = lambda i: (0, 0)
    return pl.pallas_call(
        _ssm_step_kernel,
        grid=(b,),
        in_specs=[pl.BlockSpec((1, g, c), per_b), pl.BlockSpec((1, g, n), per_b), pl.BlockSpec((1, g, n), per_b),
                  pl.BlockSpec((g, n), c2), pl.BlockSpec((g, n), c2),
                  pl.BlockSpec((c, g, n), c3), pl.BlockSpec((c, g, n), c3),
                  pl.BlockSpec((c, g, n), c3), pl.BlockSpec((c, g, n), c3),
                  pl.BlockSpec((g, c), c2)],
        out_specs=[pl.BlockSpec((1, g, c), per_b), pl.BlockSpec((1, g, n), per_b), pl.BlockSpec((1, g, n), per_b)],
        out_shape=[jax.ShapeDtypeStruct((b, g, c), F32), jax.ShapeDtypeStruct((b, g, n), F32),
                   jax.ShapeDtypeStruct((b, g, n), F32)],
        compiler_params=_params(("parallel",)),
        name="ssm_step",
    )(u, h_re, h_im, a_bar[0], a_bar[1], bs[0].transpose(2, 0, 1), bs[1].transpose(2, 0, 1),
      c_re.transpose(1, 0, 2), c_im.transpose(1, 0, 2), d_skip.reshape(g, c))


def _dsa_layer(xp, xs, gain, cache_k, cache_v, cache_ik, layer, page_table, w_in, w_out, q_gain, k_gain, ik_gain,
               bp, t, past_len, tm):
    mp, ms = xp.shape[0], xs.shape[0]
    nkv = N_KV * HEAD_DIM
    nq = t // TQ
    proj = norm_proj(xp, gain, w_in, layer, tm)
    q, k, v, ik, iw, kh, vt, ikb, iq_t = dsa_post(proj, jnp.arange(t), q_gain, k_gain, ik_gain, KC, True)
    iw_t = iw.reshape(bp, nq, TQ, N_IDX_HEADS).transpose(0, 1, 3, 2)
    o = dsa_attention(iq_t.reshape(bp, nq * IDX_DIM, N_IDX_HEADS * TQ), iw_t, ikb.reshape(bp, t, IDX_DIM),
                      q.reshape(bp, t, -1), kh.reshape(N_KV, bp, t, HEAD_DIM),
                      vt.reshape(N_KV, bp, t // KC, HEAD_DIM, KC))
    xp = out_res(xp, o.reshape(mp, -1), w_out, layer, tm)
    projs = norm_proj(xs, gain, w_in, layer, ms)
    pos_s = jnp.full((ms,), past_len, I32)
    qs, ks, vs, iks, iws, iqs = dsa_post(projs, pos_s, q_gain, k_gain, ik_gain, ms, False)
    bias, bias_n = dsa_sample_index(page_table, iqs.reshape(ms, N_IDX_HEADS, IDX_DIM),
                                    iws.reshape(ms, N_IDX_HEADS, 1), cache_ik.transpose(0, 1, 3, 2), layer,
                                    iks.reshape(ms, 1, IDX_DIM))
    bias_nh = jnp.broadcast_to(bias_n[:, :, :1], (ms, N_HEADS, 1))
    os_ = paged_decode_attention(page_table, qs.reshape(ms, N_HEADS, HEAD_DIM), cache_k, cache_v, layer, bias,
                                 ks, vs, bias_nh)
    xs = out_res(xs, os_.reshape(ms, -1), w_out, layer, ms)
    return xp, xs, (k, v, ik), (ks, vs, iks)


def _key_bias_columns(parts, bp, t):
    cols = jnp.stack(parts, axis=-1).reshape(bp, t, N_KV, GROUP * 3).transpose(0, 2, 1, 3)
    return jnp.pad(cols, ((0, 0), (0, 0), (0, 0), (0, HEAD_DIM - GROUP * 3)))


def _fox_layer(xp, xs, gain, cache_k, cache_v, cache_lf, layer, page_table, w_in, b_f, w_out, q_gain, k_gain,
               bp, t, tm):
    mp, ms = xp.shape[0], xs.shape[0]
    nkv = N_KV * HEAD_DIM
    proj = norm_proj(xp, gain, w_in, layer, tm)
    q, k, v, lf, kh, vt = fox_post(proj, b_f, q_gain, k_gain, KC, True)
    c_parts = cumsum_rows_split(lf.reshape(bp, t, N_HEADS))
    o = fox_attention(q.reshape(bp, t, -1), kh.reshape(N_KV, bp, t, HEAD_DIM), _key_bias_columns(c_parts, bp, t),
                      vt.reshape(N_KV, bp, t // KC, HEAD_DIM, KC))
    xp = out_res(xp, o.reshape(mp, -1), w_out, layer, tm)
    projs = norm_proj(xs, gain, w_in, layer, ms)
    qs, ks, vs, lfs = fox_post(projs, b_f, q_gain, k_gain, ms, False)
    bias = fox_sample_bias(page_table, cache_lf.transpose(0, 1, 3, 2), layer, lfs.reshape(ms, N_HEADS, 1))
    os_ = paged_decode_attention(page_table, qs.reshape(ms, N_HEADS, HEAD_DIM), cache_k, cache_v, layer, bias,
                                 ks, vs, jnp.zeros((ms, N_HEADS, 1), F32))
    xs = out_res(xs, os_.reshape(ms, -1), w_out, layer, ms)
    return xp, xs, (k, v, lf), (ks, vs, lfs)


def _ssm_layer(xp, xs, gain, h0_re, h0_im, a_re, a_im, log_dt, b_re, b_im, c_re, c_im, d_skip, w_glu, layer,
               bp, t, tm):
    mp, ms = xp.shape[0], xs.shape[0]
    g, n, cdim, L = N_SSM_GROUPS, SSM_STATE, SSM_GROUP, SSM_CHUNK
    m, p, q, al_re, al_im, a_bar, bs = _ssm_tables(a_re, a_im, log_dt, b_re, b_im, c_re, c_im, d_skip)
    u = rmsnorm(xp, gain, tm)
    y, hf = ssm_prompt(u, m, p, q, al_re, al_im, bp)
    xp = glu_res(xp, y, w_glu, layer, tm)
    hf = hf.transpose(1, 0, 2)
    us = rmsnorm(xs, gain, ms)
    ys, hs_re, hs_im = ssm_step(us.reshape(ms, g, cdim), h0_re, h0_im, a_bar, bs, c_re, c_im, d_skip)
    xs = glu_res(xs, ys.reshape(ms, g * cdim), w_glu, layer, ms)
    return xp, xs, (hf[..., :n], hf[..., n:]), (hs_re, hs_im)


def kernel(x_prompt, x_sample, cache_dsa_k, cache_dsa_v, cache_dsa_idx_k, cache_fox_k, cache_fox_v, cache_fox_logf, state_ssm_re, state_ssm_im, page_table, norm_mix, norm_ffn, dsa_w_in, dsa_w_out, dsa_q_gain, dsa_k_gain, dsa_ik_gain, fox_w_in, fox_b_f, fox_w_out, fox_q_gain, fox_k_gain, ssm_a_re, ssm_a_im, ssm_log_dt, ssm_b_re, ssm_b_im, ssm_c_re, ssm_c_im, ssm_d, ssm_w_glu, ffn_w_gu, ffn_w_down):
    bp, t, d = x_prompt.shape
    bs_, ts, _ = x_sample.shape
    assert ts == 1, "the sample group carries one new position per sequence"
    assert t % KC == 0 and KC % TQ == 0
    depth = norm_mix.shape[0]
    past_len = page_table.shape[1] * PAGE
    tm = min(512, bp * t)
    xp = x_prompt.reshape(bp * t, d)
    xs = x_sample.reshape(bs_ * ts, d)
    dsa_w_in, dsa_w_out, fox_w_in, fox_w_out, ssm_w_glu, ffn_w_gu, ffn_w_down = (
        w.astype(BF16) for w in (dsa_w_in, dsa_w_out, fox_w_in, fox_w_out, ssm_w_glu, ffn_w_gu, ffn_w_down))
    dsa_p, dsa_s, fox_p, fox_s, ssm_p, ssm_s = [], [], [], [], [], []
    for i in range(depth):
        kind, j = i % N_MIXERS, i // N_MIXERS
        if kind == 0:
            xp, xs, outp, outs = _dsa_layer(
                xp, xs, norm_mix[i], cache_dsa_k, cache_dsa_v, cache_dsa_idx_k, j, page_table,
                dsa_w_in, dsa_w_out, dsa_q_gain[j], dsa_k_gain[j], dsa_ik_gain[j], bp, t, past_len, tm)
            dsa_p.append(outp); dsa_s.append(outs)
        elif kind == 1:
            xp, xs, outp, outs = _fox_layer(
                xp, xs, norm_mix[i], cache_fox_k, cache_fox_v, cache_fox_logf, j, page_table,
                fox_w_in, fox_b_f[j], fox_w_out, fox_q_gain[j], fox_k_gain[j], bp, t, tm)
            fox_p.append(outp); fox_s.append(outs)
        else:
            xp, xs, outp, outs = _ssm_layer(
                xp, xs, norm_mix[i], state_ssm_re[j], state_ssm_im[j], ssm_a_re[j], ssm_a_im[j], ssm_log_dt[j],
                ssm_b_re[j], ssm_b_im[j], ssm_c_re[j], ssm_c_im[j], ssm_d[j], ssm_w_glu, j, bp, t, tm)
            ssm_p.append(outp); ssm_s.append(outs)
        xp = ffn_res(xp, norm_ffn[i], ffn_w_gu, ffn_w_down, i, tm)
        xs = ffn_res(xs, norm_ffn[i], ffn_w_gu, ffn_w_down, i, bs_ * ts)

    def stack(items, idx, shape):
        return jnp.stack([it[idx].reshape(shape) for it in items])

    kv_p, kv_s = (bp, t, N_KV, HEAD_DIM), (bs_, ts, N_KV, HEAD_DIM)
    st_p, st_s = (bp, N_SSM_GROUPS, SSM_STATE), (bs_, N_SSM_GROUPS, SSM_STATE)
    return (xp.reshape(bp, t, d), xs.reshape(bs_, ts, d),
            stack(dsa_p, 0, kv_p), stack(dsa_p, 1, kv_p), stack(dsa_p, 2, (bp, t, IDX_DIM)),
            stack(dsa_s, 0, kv_s), stack(dsa_s, 1, kv_s), stack(dsa_s, 2, (bs_, ts, IDX_DIM)),
            stack(fox_p, 0, kv_p), stack(fox_p, 1, kv_p), stack(fox_p, 2, (bp, t, N_HEADS)),
            stack(fox_s, 0, kv_s), stack(fox_s, 1, kv_s), stack(fox_s, 2, (bs_, ts, N_HEADS)),
            stack(ssm_p, 0, st_p), stack(ssm_p, 1, st_p), stack(ssm_s, 0, st_s), stack(ssm_s, 1, st_s))
```

```python
import functools
import math

import jax
import jax.numpy as jnp
from jax import lax
from jax.experimental import pallas as pl
from jax.experimental.pallas import tpu as pltpu

F32 = jnp.float32
BF16 = jnp.bfloat16
I32 = jnp.int32

D_MODEL = 2048
N_HEADS = 16
HEAD_DIM = 128
N_KV = 4
GROUP = N_HEADS // N_KV
N_IDX_HEADS = 16
IDX_DIM = 64
TOPK_MAX = 256
ROPE_THETA = 500000.0
PAGE = 128
SSM_GROUP = 16
N_SSM_GROUPS = D_MODEL // SSM_GROUP
SSM_STATE = 64
SSM_CHUNK = 16
EPS = 1e-6
N_MIXERS = 3

LANES = 128
SUBLANES = 8
TQ = 128
KC = 256
DECODE_PAGES = 8
LOG2E = math.log2(math.e)
NEG = -1e30
BISECT_MAX = 64
HI = lax.Precision.HIGHEST
VMEM_LIMIT = 56 * 1024 * 1024

NT = (((1,), (1,)), ((), ()))


def _params(sem):
    return pltpu.CompilerParams(dimension_semantics=sem, vmem_limit_bytes=VMEM_LIMIT)


def _rms(x, gain):
    return x * lax.rsqrt(jnp.mean(x * x, axis=-1, keepdims=True) + EPS) * gain


def _proj_kernel(x_ref, g_ref, w_ref, o_ref, h_ref):
    @pl.when(pl.program_id(1) == 0)
    def _():
        h_ref[...] = _rms(x_ref[...], g_ref[...]).astype(BF16)

    o_ref[...] = jnp.dot(h_ref[...], w_ref[...], preferred_element_type=F32)


def _pick_tn(n, cap=1536):
    k = -(-n // cap)
    return -(-n // (k * LANES)) * LANES


def norm_proj(x, gain, w, layer, tm):
    m, d = x.shape
    n = w.shape[2]
    tn = _pick_tn(n)
    return pl.pallas_call(
        _proj_kernel,
        grid=(m // tm, pl.cdiv(n, tn)),
        in_specs=[pl.BlockSpec((tm, d), lambda i, j: (i, 0)),
                  pl.BlockSpec((1, d), lambda i, j: (0, 0)),
                  pl.BlockSpec((None, d, tn), lambda i, j: (layer, 0, j))],
        out_specs=pl.BlockSpec((tm, tn), lambda i, j: (i, j)),
        out_shape=jax.ShapeDtypeStruct((m, n), F32),
        scratch_shapes=[pltpu.VMEM((tm, d), BF16)],
        compiler_params=_params(("parallel", "arbitrary")),
        name="norm_proj",
    )(x, gain.reshape(1, d), w)


def _out_res_kernel(x_ref, o_ref, w_ref, y_ref):
    y_ref[...] = x_ref[...] + jnp.dot(o_ref[...].astype(BF16), w_ref[...], preferred_element_type=F32)


def out_res(x, o, w, layer, tm):
    m, d = x.shape
    k = w.shape[1]
    return pl.pallas_call(
        _out_res_kernel,
        grid=(m // tm,),
        in_specs=[pl.BlockSpec((tm, d), lambda i: (i, 0)),
                  pl.BlockSpec((tm, k), lambda i: (i, 0)),
                  pl.BlockSpec((None, k, d), lambda i: (layer, 0, 0))],
        out_specs=pl.BlockSpec((tm, d), lambda i: (i, 0)),
        out_shape=jax.ShapeDtypeStruct((m, d), F32),
        compiler_params=_params(("parallel",)),
        name="out_res",
    )(x, o, w)


def _ffn_kernel(x_ref, g_ref, wg_ref, wu_ref, wd_ref, y_ref, h_ref, acc_ref):
    f = pl.program_id(1)

    @pl.when(f == 0)
    def _():
        h_ref[...] = _rms(x_ref[...], g_ref[...]).astype(BF16)
        acc_ref[...] = jnp.zeros_like(acc_ref)

    h = h_ref[...]
    g = jnp.dot(h, wg_ref[...], preferred_element_type=F32)
    u = jnp.dot(h, wu_ref[...], preferred_element_type=F32)
    a = (g * jax.nn.sigmoid(g) * u).astype(BF16)
    acc_ref[...] += jnp.dot(a, wd_ref[...], preferred_element_type=F32)

    @pl.when(f == pl.num_programs(1) - 1)
    def _():
        y_ref[...] = x_ref[...] + acc_ref[...]


def ffn_res(x, gain, w_gu, w_down, layer, tm, tf=512):
    m, d = x.shape
    dff = w_down.shape[1]
    nf = dff // tf
    return pl.pallas_call(
        _ffn_kernel,
        grid=(m // tm, nf),
        in_specs=[pl.BlockSpec((tm, d), lambda i, f: (i, 0)),
                  pl.BlockSpec((1, d), lambda i, f: (0, 0)),
                  pl.BlockSpec((None, d, tf), lambda i, f: (layer, 0, f)),
                  pl.BlockSpec((None, d, tf), lambda i, f: (layer, 0, f + nf)),
                  pl.BlockSpec((None, tf, d), lambda i, f: (layer, f, 0))],
        out_specs=pl.BlockSpec((tm, d), lambda i, f: (i, 0)),
        out_shape=jax.ShapeDtypeStruct((m, d), F32),
        scratch_shapes=[pltpu.VMEM((tm, d), BF16), pltpu.VMEM((tm, d), F32)],
        compiler_params=_params(("parallel", "arbitrary")),
        name="ffn_res",
    )(x, gain.reshape(1, d), w_gu, w_gu, w_down)


def _gelu_tanh(y):
    return 0.5 * y * (1.0 + jnp.tanh(math.sqrt(2.0 / math.pi) * (y + 0.044715 * (y * y * y))))


def _glu_kernel(x_ref, y_ref, wa_ref, wb_ref, o_ref, g_ref):
    @pl.when(pl.program_id(1) == 0)
    def _():
        g_ref[...] = _gelu_tanh(y_ref[...]).astype(BF16)

    g = g_ref[...]
    a = jnp.dot(g, wa_ref[...], preferred_element_type=F32)
    b = jnp.dot(g, wb_ref[...], preferred_element_type=F32)
    o_ref[...] = x_ref[...] + a * jax.nn.sigmoid(b)


def glu_res(x, y, w_glu, layer, tm, tn=512):
    m, d = x.shape
    nj = d // tn
    return pl.pallas_call(
        _glu_kernel,
        grid=(m // tm, nj),
        in_specs=[pl.BlockSpec((tm, tn), lambda i, j: (i, j)),
                  pl.BlockSpec((tm, d), lambda i, j: (i, 0)),
                  pl.BlockSpec((None, d, tn), lambda i, j: (layer, 0, j)),
                  pl.BlockSpec((None, d, tn), lambda i, j: (layer, 0, j + nj))],
        out_specs=pl.BlockSpec((tm, tn), lambda i, j: (i, j)),
        out_shape=jax.ShapeDtypeStruct((m, d), F32),
        scratch_shapes=[pltpu.VMEM((tm, d), BF16)],
        compiler_params=_params(("parallel", "arbitrary")),
        name="glu_res",
    )(x, y, w_glu, w_glu)


def _rmsnorm_kernel(x_ref, g_ref, o_ref):
    o_ref[...] = _rms(x_ref[...], g_ref[...])


def rmsnorm(x, gain, tm):
    m, d = x.shape
    return pl.pallas_call(
        _rmsnorm_kernel,
        grid=(m // tm,),
        in_specs=[pl.BlockSpec((tm, d), lambda i: (i, 0)), pl.BlockSpec((1, d), lambda i: (0, 0))],
        out_specs=pl.BlockSpec((tm, d), lambda i: (i, 0)),
        out_shape=jax.ShapeDtypeStruct((m, d), F32),
        compiler_params=_params(("parallel",)),
        name="rmsnorm",
    )(x, gain.reshape(1, d))


def _rope_tables(pos, width, period):
    rot = period // 4
    half = rot // 2
    inv = ROPE_THETA ** (-jnp.arange(half, dtype=F32) / half)
    ang = pos.astype(F32)[:, None] * inv[None, :]
    cos, sin = jnp.cos(ang), jnp.sin(ang)
    n = pos.shape[0]
    ones = jnp.ones((n, period - rot), F32)
    zeros = jnp.zeros((n, period - rot), F32)
    zh = jnp.zeros((n, half), F32)
    c = jnp.concatenate([cos, cos, ones], axis=1)
    s_up = jnp.concatenate([zh, sin, zeros], axis=1)
    s_dn = jnp.concatenate([-sin, zh, zeros], axis=1)
    reps = width // period
    return tuple(jnp.tile(a, (1, reps)) for a in (c, s_up, s_dn))


def _rope(x, cos, s_up, s_dn, half):
    w = x.shape[-1]
    return x * cos + pltpu.roll(x, half, 1) * s_up + pltpu.roll(x, w - half, 1) * s_dn


def _store_kv_heads(p_ref, k0, v0, k_fn, k_ref, v_ref, kh_ref, vt_ref):
    for h in range(N_KV):
        y = k_fn(p_ref[:, k0 + h * HEAD_DIM:k0 + (h + 1) * HEAD_DIM])
        k_ref[:, h * HEAD_DIM:(h + 1) * HEAD_DIM] = y
        if kh_ref is not None:
            kh_ref[h] = y.astype(BF16)
            vh = p_ref[:, v0 + h * HEAD_DIM:v0 + (h + 1) * HEAD_DIM]
            vt_ref[h, 0] = vh.T.astype(BF16)
    v_ref[...] = p_ref[:, v0:v0 + N_KV * HEAD_DIM]


def _dsa_post_kernel(p_ref, qg_ref, kg_ref, ikg_ref, c1_ref, u1_ref, d1_ref, c2_ref, u2_ref, d2_ref,
                     q_ref, k_ref, v_ref, ik_ref, iw_ref, *mode_refs, prompt):
    c1, u1, d1 = c1_ref[...], u1_ref[...], d1_ref[...]
    c2, u2, d2 = c2_ref[...], u2_ref[...], d2_ref[...]
    qg, kg = qg_ref[...], kg_ref[...]
    scale = HEAD_DIM ** -0.5 * LOG2E
    for h in range(N_HEADS):
        x = p_ref[:, h * HEAD_DIM:(h + 1) * HEAD_DIM]
        y = _rope(_rms(x, qg), c1, u1, d1, HEAD_DIM // 8)
        q_ref[:, h * HEAD_DIM:(h + 1) * HEAD_DIM] = (y * scale).astype(BF16)
    k0 = N_HEADS * HEAD_DIM
    v0 = k0 + N_KV * HEAD_DIM
    kh_ref, vt_ref, ikb_ref, iqt_ref = mode_refs if prompt else (None, None, None, None)
    iq_ref = None if prompt else mode_refs[0]
    _store_kv_heads(p_ref, k0, v0, lambda x: _rope(_rms(x, kg), c1, u1, d1, HEAD_DIM // 8),
                    k_ref, v_ref, kh_ref, vt_ref)
    i0 = v0 + N_KV * HEAD_DIM
    tm = p_ref.shape[0]
    for h in range(N_IDX_HEADS * IDX_DIM // LANES):
        x = _rope(p_ref[:, i0 + h * LANES:i0 + (h + 1) * LANES], c2, u2, d2, IDX_DIM // 8)
        if prompt:
            for qb in range(tm // TQ):
                xt = x[qb * TQ:(qb + 1) * TQ].T.astype(BF16)
                iqt_ref[qb, :, (2 * h) * TQ:(2 * h + 1) * TQ] = xt[:IDX_DIM]
                iqt_ref[qb, :, (2 * h + 1) * TQ:(2 * h + 2) * TQ] = xt[IDX_DIM:]
        else:
            iq_ref[:, h * LANES:(h + 1) * LANES] = x.astype(BF16)
    j0 = i0 + N_IDX_HEADS * IDX_DIM
    xk = _rms(p_ref[:, j0:j0 + IDX_DIM], ikg_ref[...])
    xk2 = jnp.concatenate([xk, xk], axis=1)
    ik = _rope(xk2, c2, u2, d2, IDX_DIM // 8)[:, :IDX_DIM]
    ik_ref[...] = ik
    if prompt:
        ikb_ref[...] = ik.astype(BF16)
    iw_ref[...] = p_ref[:, j0 + IDX_DIM:j0 + IDX_DIM + N_IDX_HEADS] * (N_IDX_HEADS ** -0.5 * IDX_DIM ** -0.5)


def dsa_post(proj, pos, q_gain, k_gain, ik_gain, tm, prompt):
    m, n = proj.shape
    npos = pos.shape[0]
    t1 = _rope_tables(pos, HEAD_DIM, HEAD_DIM)
    t2 = _rope_tables(pos, LANES, IDX_DIM)
    nb = npos // tm
    row = lambda i: (i, 0)
    tab = lambda i: (i % nb, 0)
    const = lambda i: (0, 0)
    nkv = N_KV * HEAD_DIM
    niq = N_IDX_HEADS * IDX_DIM
    outs = [((m, N_HEADS * HEAD_DIM), BF16, pl.BlockSpec((tm, N_HEADS * HEAD_DIM), row)),
            ((m, nkv), F32, pl.BlockSpec((tm, nkv), row)),
            ((m, nkv), F32, pl.BlockSpec((tm, nkv), row)),
            ((m, IDX_DIM), F32, pl.BlockSpec((tm, IDX_DIM), row)),
            ((m, N_IDX_HEADS), F32, pl.BlockSpec((tm, N_IDX_HEADS), row))]
    if not prompt:
        outs += [((m, niq), BF16, pl.BlockSpec((tm, niq), row))]
    if prompt:
        assert tm == KC and tm % TQ == 0
        outs += [((N_KV, m, HEAD_DIM), BF16, pl.BlockSpec((N_KV, tm, HEAD_DIM), lambda i: (0, i, 0))),
                 ((N_KV, m // KC, HEAD_DIM, KC), BF16, pl.BlockSpec((N_KV, 1, HEAD_DIM, KC), lambda i: (0, i, 0, 0))),
                 ((m, IDX_DIM), BF16, pl.BlockSpec((tm, IDX_DIM), row)),
                 ((m // TQ, IDX_DIM, N_IDX_HEADS * TQ), BF16,
                  pl.BlockSpec((tm // TQ, IDX_DIM, N_IDX_HEADS * TQ), lambda i: (i, 0, 0)))]
    return pl.pallas_call(
        functools.partial(_dsa_post_kernel, prompt=prompt),
        grid=(m // tm,),
        in_specs=[pl.BlockSpec((tm, n), row),
                  pl.BlockSpec((1, HEAD_DIM), const), pl.BlockSpec((1, HEAD_DIM), const),
                  pl.BlockSpec((1, IDX_DIM), const)]
                 + [pl.BlockSpec((tm, LANES), tab)] * 6,
        out_specs=[o[2] for o in outs],
        out_shape=[jax.ShapeDtypeStruct(o[0], o[1]) for o in outs],
        compiler_params=_params(("parallel",)),
        name="dsa_post",
    )(proj, q_gain.reshape(1, -1), k_gain.reshape(1, -1), ik_gain.reshape(1, -1), *t1, *t2)


def _log_sigmoid(x):
    return jnp.minimum(x, 0.0) - jnp.log1p(jnp.exp(-jnp.abs(x)))


def _fox_post_kernel(p_ref, qg_ref, kg_ref, bf_ref, q_ref, k_ref, v_ref, lf_ref, *prompt_refs, prompt):
    qg, kg = qg_ref[...], kg_ref[...]
    scale = HEAD_DIM ** -0.5 * LOG2E
    for h in range(N_HEADS):
        x = p_ref[:, h * HEAD_DIM:(h + 1) * HEAD_DIM]
        q_ref[:, h * HEAD_DIM:(h + 1) * HEAD_DIM] = (_rms(x, qg) * scale).astype(BF16)
    k0 = N_HEADS * HEAD_DIM
    v0 = k0 + N_KV * HEAD_DIM
    kh_ref, vt_ref = prompt_refs if prompt else (None, None)
    _store_kv_heads(p_ref, k0, v0, lambda x: _rms(x, kg), k_ref, v_ref, kh_ref, vt_ref)
    f0 = v0 + N_KV * HEAD_DIM
    lf_ref[...] = _log_sigmoid(p_ref[:, f0:f0 + N_HEADS] + bf_ref[...])


def fox_post(proj, b_f, q_gain, k_gain, tm, prompt):
    m, n = proj.shape
    row = lambda i: (i, 0)
    const = lambda i: (0, 0)
    nkv = N_KV * HEAD_DIM
    outs = [((m, N_HEADS * HEAD_DIM), BF16, pl.BlockSpec((tm, N_HEADS * HEAD_DIM), row)),
            ((m, nkv), F32, pl.BlockSpec((tm, nkv), row)),
            ((m, nkv), F32, pl.BlockSpec((tm, nkv), row)),
            ((m, N_HEADS), F32, pl.BlockSpec((tm, N_HEADS), row))]
    if prompt:
        assert tm == KC
        outs += [((N_KV, m, HEAD_DIM), BF16, pl.BlockSpec((N_KV, tm, HEAD_DIM), lambda i: (0, i, 0))),
                 ((N_KV, m // KC, HEAD_DIM, KC), BF16, pl.BlockSpec((N_KV, 1, HEAD_DIM, KC), lambda i: (0, i, 0, 0)))]
    return pl.pallas_call(
        functools.partial(_fox_post_kernel, prompt=prompt),
        grid=(m // tm,),
        in_specs=[pl.BlockSpec((tm, n), row), pl.BlockSpec((1, HEAD_DIM), const),
                  pl.BlockSpec((1, HEAD_DIM), const), pl.BlockSpec((1, N_HEADS), const)],
        out_specs=[o[2] for o in outs],
        out_shape=[jax.ShapeDtypeStruct(o[0], o[1]) for o in outs],
        compiler_params=_params(("parallel",)),
        name="fox_post",
    )(proj, q_gain.reshape(1, -1), k_gain.reshape(1, -1), b_f.reshape(1, -1))


def _queries_t(q_ref, j):
    cols = []
    for g in range(GROUP):
        x = q_ref[0, :, (GROUP * j + g) * HEAD_DIM:(GROUP * j + g + 1) * HEAD_DIM]
        cols.append(x.astype(F32).T.astype(BF16))
    return jnp.concatenate(cols, axis=1)


def _attend_heads(s_first, logits_fn, values_fn, m_ref, l_ref, acc_ref, next_first_fn=None):
    s_next = s_first
    pending = None
    for j in range(N_KV):
        s_t = s_next
        if j + 1 < N_KV:
            s_next = logits_fn(j + 1)
        elif next_first_fn is not None:
            s_next = next_first_fn()
        m_run = m_ref[j]
        m_new = jnp.maximum(m_run, jnp.max(s_t, axis=0, keepdims=True))
        alpha = jnp.exp2(m_run - m_new)
        p = jnp.exp2(s_t - m_new)
        l_ref[j] = alpha * l_ref[j] + jnp.sum(p, axis=0, keepdims=True)
        m_ref[j] = m_new
        pv = jnp.dot(values_fn(j), p.astype(BF16), preferred_element_type=F32)
        if pending is not None:
            jp, alpha_p, pv_p = pending
            acc_ref[jp] = alpha_p * acc_ref[jp] + pv_p
        pending = (j, alpha, pv)
    jp, alpha_p, pv_p = pending
    acc_ref[jp] = alpha_p * acc_ref[jp] + pv_p
    return s_next


def _softmax_reset(m_ref, l_ref, acc_ref):
    m_ref[...] = jnp.full_like(m_ref, NEG)
    l_ref[...] = jnp.zeros_like(l_ref)
    acc_ref[...] = jnp.zeros_like(acc_ref)


def _softmax_scratch(qk_depth):
    rows = GROUP * TQ
    return [pltpu.VMEM((N_KV, qk_depth, rows), BF16), pltpu.VMEM((N_KV, 1, rows), F32),
            pltpu.VMEM((N_KV, 1, rows), F32), pltpu.VMEM((N_KV, HEAD_DIM, rows), F32)]


def _store_heads_t(o_ref, l_ref, acc_ref):
    for j in range(N_KV):
        o_t = acc_ref[j] / l_ref[j]
        for g in range(GROUP):
            o_ref[0, :, (GROUP * j + g) * HEAD_DIM:(GROUP * j + g + 1) * HEAD_DIM] = (
                o_t[:, g * TQ:(g + 1) * TQ].T.astype(o_ref.dtype))


def _cumsum_kernel(lf_ref, hi_ref, mid_ref, lo_ref):
    t = lf_ref.shape[1]
    r = lax.broadcasted_iota(I32, (LANES, LANES), 0)
    c = lax.broadcasted_iota(I32, (LANES, LANES), 1)
    lower = (c <= r).astype(F32)
    carry = jnp.zeros((1, lf_ref.shape[2]), F32)
    for b in range(t // LANES):
        rows = slice(b * LANES, (b + 1) * LANES)
        cs = jnp.dot(lower, lf_ref[0, rows, :], precision=HI, preferred_element_type=F32) + carry
        carry = cs[LANES - 1:LANES, :]
        x = cs * LOG2E
        hi = x.astype(BF16)
        r1 = x - hi.astype(F32)
        mid = r1.astype(BF16)
        hi_ref[0, rows, :] = hi
        mid_ref[0, rows, :] = mid
        lo_ref[0, rows, :] = (r1 - mid.astype(F32)).astype(BF16)


def cumsum_rows_split(lf):
    b, t, h = lf.shape
    spec = pl.BlockSpec((1, t, h), lambda i: (i, 0, 0))
    return pl.pallas_call(
        _cumsum_kernel,
        grid=(b,),
        in_specs=[spec],
        out_specs=[spec] * 3,
        out_shape=[jax.ShapeDtypeStruct((b, t, h), BF16)] * 3,
        compiler_params=_params(("parallel",)),
        name="cumsum_rows_split",
    )(lf)


def _fox_kernel(q_ref, k_ref, cp_ref, v_ref, o_ref, qt_ref, m_ref, l_ref, acc_ref):
    i = pl.program_id(1)
    rows = GROUP * TQ
    last = (i * TQ) // KC
    r_io = lax.broadcasted_iota(I32, (HEAD_DIM, rows), 0)
    g_io = lax.broadcasted_iota(I32, (HEAD_DIM, rows), 1) // TQ
    sel = jnp.where((r_io >= 3 * g_io) & (r_io < 3 * g_io + 3), -1.0, 0.0).astype(BF16)
    for j in range(N_KV):
        qt_ref[j] = jnp.concatenate([_queries_t(q_ref, j), sel], axis=0)
    _softmax_reset(m_ref, l_ref, acc_ref)

    def logits(j, c):
        start = pl.multiple_of(c * KC, KC)
        kc = jnp.concatenate([k_ref[j, 0, pl.ds(start, KC), :], cp_ref[0, j, pl.ds(start, KC), :]], axis=1)
        return jnp.dot(kc, qt_ref[j], preferred_element_type=F32)

    def body(c, s_first):
        return _attend_heads(s_first, lambda j: logits(j, c), lambda j: v_ref[j, 0, c], m_ref, l_ref, acc_ref,
                             next_first_fn=lambda: logits(0, c + 1))

    s_first = lax.fori_loop(0, last, body, logits(0, 0))
    key_io = lax.broadcasted_iota(I32, (KC, rows), 0)
    q_io = lax.broadcasted_iota(I32, (KC, rows), 1) % TQ + i * TQ
    causal = key_io + last * KC <= q_io
    _attend_heads(jnp.where(causal, s_first, NEG), lambda j: jnp.where(causal, logits(j, last), NEG),
                  lambda j: v_ref[j, 0, last], m_ref, l_ref, acc_ref)
    _store_heads_t(o_ref, l_ref, acc_ref)


def fox_attention(q, kh, cparts, vt):
    b, t, _ = q.shape
    return pl.pallas_call(
        _fox_kernel,
        grid=(b, t // TQ),
        in_specs=[pl.BlockSpec((1, TQ, N_HEADS * HEAD_DIM), lambda bi, i: (bi, i, 0)),
                  pl.BlockSpec((N_KV, 1, t, HEAD_DIM), lambda bi, i: (0, bi, 0, 0)),
                  pl.BlockSpec((1, N_KV, t, HEAD_DIM), lambda bi, i: (bi, 0, 0, 0)),
                  pl.BlockSpec((N_KV, 1, t // KC, HEAD_DIM, KC), lambda bi, i: (0, bi, 0, 0, 0))],
        out_specs=pl.BlockSpec((1, TQ, N_HEADS * HEAD_DIM), lambda bi, i: (bi, i, 0)),
        out_shape=jax.ShapeDtypeStruct((b, t, N_HEADS * HEAD_DIM), BF16),
        scratch_shapes=_softmax_scratch(2 * HEAD_DIM),
        compiler_params=_params(("parallel", "arbitrary")),
        name="fox_attention",
    )(q, kh, cparts, vt)


def _all_true(flags):
    return (jnp.min(flags.astype(F32)) > 0.5).astype(I32)


def _bisect_threshold(count_ge, lo, hi, cnt_lo, nsel):
    steps = 2

    def cond(state):
        it, _, _, _, settled, _ = state
        return (it < BISECT_MAX // steps) & (settled == 0)

    def body(state):
        it, lo, hi, cnt_lo, _, _ = state
        stuck = jnp.zeros(lo.shape, I32)
        for _ in range(steps):
            mid = lo + (hi - lo) * 0.5
            stuck = jnp.where((mid <= lo) | (mid >= hi), 1, 0)
            cnt = count_ge(mid)
            up = cnt >= nsel
            lo = jnp.where(up, mid, lo)
            cnt_lo = jnp.where(up, cnt, cnt_lo)
            hi = jnp.where(up, hi, mid)
        exact = jnp.where(cnt_lo == nsel, 1, 0)
        return it + 1, lo, hi, cnt_lo, _all_true(jnp.maximum(exact, stuck)), _all_true(exact)

    init = (jnp.int32(0), lo, hi, cnt_lo, jnp.int32(0), jnp.int32(0))
    _, lo, _, cnt_lo, _, all_exact = lax.while_loop(cond, body, init)
    return lo, cnt_lo, all_exact


def _exact_threshold(min_ge, count_gt_and_next, lo, nsel):
    def cond(state):
        it, _, _, _, _, finished = state
        return (it < BISECT_MAX) & (finished == 0)

    def body(state):
        it, v, tau, need, done, _ = state
        n_gt, v_next = count_gt_and_next(v)
        fin = (n_gt < nsel) & (done == 0)
        tau = jnp.where(fin, v, tau)
        need = jnp.where(fin, nsel - n_gt, need)
        done = jnp.where(fin, 1, done)
        v = jnp.where(done == 1, v, v_next)
        return it + 1, v, tau, need, done, _all_true(done)

    v0 = min_ge(lo)
    init = (jnp.int32(0), v0, v0, jnp.zeros(lo.shape, F32), jnp.zeros(lo.shape, I32), jnp.int32(0))
    _, _, tau, need, _, _ = lax.while_loop(cond, body, init)
    return tau, need


def _select_tie_index(count_eq_below, need, nbits, shape):
    def bit_body(bi, m):
        cand = m | jnp.left_shift(jnp.int32(1), nbits - 1 - bi)
        return jnp.where(count_eq_below(cand) < need, cand, m)

    return lax.fori_loop(0, nbits, bit_body, jnp.zeros(shape, I32))


def _dsa_kernel(iq_ref, iw_ref, ik_ref, q_ref, k_ref, v_ref, o_ref, sc_ref, bias_ref, qt_ref, m_ref, l_ref, acc_ref,
                *, nsel, t_total):
    i = pl.program_id(1)
    nch = (i * TQ + TQ + KC - 1) // KC
    w = iw_ref[0, 0]
    iq_t = iq_ref[0]
    key_io = lax.broadcasted_iota(I32, (KC, TQ), 0)
    qpos = i * TQ + lax.broadcasted_iota(I32, (KC, TQ), 1)
    inf = float("inf")

    def fold(x, op):
        part = op(x.reshape(4, KC // (4 * SUBLANES), SUBLANES, TQ), axis=1)
        return op(part, axis=0)

    def score_body(c, carry):
        lo, hi = carry
        ikc = ik_ref[0, pl.ds(pl.multiple_of(c * KC, KC), KC), :]
        y = jnp.dot(ikc, iq_t, preferred_element_type=F32)
        acc = jnp.zeros((KC, TQ), F32)
        for h in range(N_IDX_HEADS):
            acc = acc + w[h:h + 1, :] * jnp.maximum(y[:, h * TQ:(h + 1) * TQ], 0.0)
        causal = key_io + c * KC <= qpos
        sc_ref[c] = jnp.where(causal, acc, -inf)
        return (jnp.minimum(lo, fold(jnp.where(causal, acc, inf), jnp.min)),
                jnp.maximum(hi, fold(jnp.where(causal, acc, -inf), jnp.max)))

    lo8, hi8 = lax.fori_loop(0, nch, score_body,
                             (jnp.full((SUBLANES, TQ), inf, F32), jnp.full((SUBLANES, TQ), -inf, F32)))

    def count(pred):
        def body(c, acc):
            return acc + fold(jnp.where(pred(sc_ref[c], c), 1.0, 0.0), jnp.sum)
        acc = lax.fori_loop(0, nch, body, jnp.zeros((SUBLANES, TQ), F32))
        return jnp.sum(acc, axis=0, keepdims=True)

    def select():
        lo = jnp.min(lo8, axis=0, keepdims=True)
        hi = jnp.max(hi8, axis=0, keepdims=True)
        n_causal = (qpos[:1] + 1).astype(F32)
        lo, _, all_exact = _bisect_threshold(lambda thr: count(lambda s, c: s >= thr), lo, hi, n_causal, float(nsel))

        def ties():
            def min_ge(lo):
                def body(c, acc):
                    s = sc_ref[c]
                    return jnp.minimum(acc, fold(jnp.where(s >= lo, s, inf), jnp.min))
                acc = lax.fori_loop(0, nch, body, jnp.full((SUBLANES, TQ), inf, F32))
                return jnp.min(acc, axis=0, keepdims=True)

            def count_gt_and_next(v):
                def body(c, carry):
                    n, nxt = carry
                    s = sc_ref[c]
                    above = s > v
                    return (n + fold(jnp.where(above, 1.0, 0.0), jnp.sum),
                            jnp.minimum(nxt, fold(jnp.where(above, s, inf), jnp.min)))
                n, nxt = lax.fori_loop(0, nch, body, (jnp.zeros((SUBLANES, TQ), F32),
                                                      jnp.full((SUBLANES, TQ), inf, F32)))
                return jnp.sum(n, axis=0, keepdims=True), jnp.min(nxt, axis=0, keepdims=True)

            tau, need = _exact_threshold(min_ge, count_gt_and_next, lo, float(nsel))
            nbits = max(1, int(t_total - 1).bit_length())
            m_star = _select_tie_index(
                lambda cand: count(lambda s, c: (s == tau) & (key_io + c * KC < cand)), need, nbits, (1, TQ))
            return tau, m_star

        return lax.cond(all_exact == 1, lambda: (lo, jnp.full((1, TQ), t_total, I32)), ties)

    tau, m_star = lax.cond(i * TQ + TQ <= nsel,
                           lambda: (jnp.full((1, TQ), -inf, F32), jnp.full((1, TQ), t_total, I32)), select)

    def bias_body(c, _):
        s = sc_ref[c]
        idx = key_io + c * KC
        keep = ((s > tau) | ((s == tau) & (idx <= m_star))) & (idx <= qpos)
        bias_ref[c] = jnp.where(keep, 0.0, NEG)
        return 0

    lax.fori_loop(0, nch, bias_body, 0)

    for j in range(N_KV):
        qt_ref[j] = _queries_t(q_ref, j)
    _softmax_reset(m_ref, l_ref, acc_ref)

    def logits(j, c):
        kc = k_ref[j, 0, pl.ds(pl.multiple_of(c * KC, KC), KC), :]
        bias = jnp.concatenate([bias_ref[c]] * GROUP, axis=1)
        return jnp.dot(kc, qt_ref[j], preferred_element_type=F32) + bias

    def body(c, s_first):
        return _attend_heads(s_first, lambda j: logits(j, c), lambda j: v_ref[j, 0, c], m_ref, l_ref, acc_ref,
                             next_first_fn=lambda: logits(0, jnp.minimum(c + 1, nch - 1)))

    lax.fori_loop(0, nch, body, logits(0, 0))
    _store_heads_t(o_ref, l_ref, acc_ref)


def dsa_attention(iq_t, iw_t, ikb, q, kh, vt):
    b, t, _ = q.shape
    nsel = min(TOPK_MAX, t // 4)
    nch = t // KC
    kern = functools.partial(_dsa_kernel, nsel=nsel, t_total=t)
    return pl.pallas_call(
        kern,
        grid=(b, t // TQ),
        in_specs=[pl.BlockSpec((1, IDX_DIM, N_IDX_HEADS * TQ), lambda bi, i: (bi, i, 0)),
                  pl.BlockSpec((1, 1, N_IDX_HEADS, TQ), lambda bi, i: (bi, i, 0, 0)),
                  pl.BlockSpec((1, t, IDX_DIM), lambda bi, i: (bi, 0, 0)),
                  pl.BlockSpec((1, TQ, N_HEADS * HEAD_DIM), lambda bi, i: (bi, i, 0)),
                  pl.BlockSpec((N_KV, 1, t, HEAD_DIM), lambda bi, i: (0, bi, 0, 0)),
                  pl.BlockSpec((N_KV, 1, nch, HEAD_DIM, KC), lambda bi, i: (0, bi, 0, 0, 0))],
        out_specs=pl.BlockSpec((1, TQ, N_HEADS * HEAD_DIM), lambda bi, i: (bi, i, 0)),
        out_shape=jax.ShapeDtypeStruct((b, t, N_HEADS * HEAD_DIM), BF16),
        scratch_shapes=[pltpu.VMEM((nch, KC, TQ), F32), pltpu.VMEM((nch, KC, TQ), F32)] + _softmax_scratch(HEAD_DIM),
        compiler_params=_params(("parallel", "arbitrary")),
        name="dsa_attention",
    )(iq_t, iw_t, ikb, q, kh, vt)


def _page_copy(pool_ref, layer, pt_ref, b, p, buf, slot, sem):
    return pltpu.make_async_copy(pool_ref.at[layer, pt_ref[b, p]], buf.at[slot, p], sem.at[slot])


def _fetch_sequence_pages(pool_ref, layer, pt_ref, buf, sem):
    b = pl.program_id(0)
    nb = pl.num_programs(0)
    npages = buf.shape[1]
    slot = b % 2

    def start_all(bb, sl):
        lax.fori_loop(0, npages, lambda p, _: (_page_copy(pool_ref, layer, pt_ref, bb, p, buf, sl, sem).start(), 0)[1], 0)

    @pl.when(b == 0)
    def _():
        start_all(0, 0)

    @pl.when(b + 1 < nb)
    def _():
        start_all(b + 1, 1 - slot)

    lax.fori_loop(0, npages, lambda p, _: (_page_copy(pool_ref, layer, pt_ref, b, p, buf, slot, sem).wait(), 0)[1], 0)
    return slot


def _dsa_sample_index_kernel(pt_ref, iq_ref, iw_ref, pool_ref, ikn_ref, bias_ref, biasn_ref, buf, sem, sc_ref,
                             *, layer, nsel):
    slot = _fetch_sequence_pages(pool_ref, layer, pt_ref, buf, sem)
    npages = buf.shape[1]
    iq = iq_ref[0]
    w = iw_ref[0]

    unroll = math.gcd(npages, 8)

    def score_body(pb, _):
        for k in range(unroll):
            p = pb * unroll + k
            y = jnp.dot(iq, buf[slot, p].astype(BF16), preferred_element_type=F32)
            sc_ref[pl.ds(p, 1), :] = jnp.sum(w * jnp.maximum(y, 0.0), axis=0, keepdims=True)
        return 0

    lax.fori_loop(0, npages // unroll, score_body, 0)
    prod = iq.astype(F32) * ikn_ref[0].astype(BF16).astype(F32)
    yn = jnp.sum(prod, axis=1, keepdims=True)
    sn = jnp.sum(w * jnp.maximum(yn, 0.0), axis=0, keepdims=True)
    sc = sc_ref[...]
    idx = lax.broadcasted_iota(I32, sc.shape, 0) * PAGE + lax.broadcasted_iota(I32, sc.shape, 1)
    inf = float("inf")

    def total(x, op=jnp.sum):
        return op(op(x, axis=0, keepdims=True), axis=1, keepdims=True)

    def count_ge(thr):
        return total(jnp.where(sc >= thr, 1.0, 0.0)) + jnp.where(sn >= thr, 1.0, 0.0)

    def min_ge(lo):
        return jnp.minimum(total(jnp.where(sc >= lo, sc, inf), jnp.min), jnp.where(sn >= lo, sn, inf))

    def count_gt_and_next(v):
        n = total(jnp.where(sc > v, 1.0, 0.0)) + jnp.where(sn > v, 1.0, 0.0)
        return n, jnp.minimum(total(jnp.where(sc > v, sc, inf), jnp.min), jnp.where(sn > v, sn, inf))

    lo = jnp.minimum(total(sc, jnp.min), sn)
    hi = jnp.maximum(total(sc, jnp.max), sn)
    lo, _, _ = _bisect_threshold(count_ge, lo, hi, jnp.full((1, 1), float(npages * PAGE + 1), F32), float(nsel))
    tau, need = _exact_threshold(min_ge, count_gt_and_next, lo, float(nsel))
    nbits = int(npages * PAGE).bit_length()
    m_star = _select_tie_index(
        lambda cand: total(jnp.where((sc == tau) & (idx < cand), 1.0, 0.0)), need, nbits, (1, 1))
    n_eq = total(jnp.where(sc == tau, 1.0, 0.0))
    keep = (sc > tau) | ((sc == tau) & (idx <= m_star))
    keep_rows = jnp.dot(jnp.where(keep, 1.0, 0.0).astype(BF16), _row_expansion().astype(BF16),
                        preferred_element_type=F32)
    bias_ref[0] = jnp.where(keep_rows > 0.5, 0.0, NEG)
    keep_n = (sn > tau) | ((sn == tau) & (n_eq < need))
    biasn_ref[0] = jnp.broadcast_to(jnp.where(keep_n, 0.0, NEG), (1, LANES))


def dsa_sample_index(page_table, iq, iw, pool_ik_t, layer, ik_new):
    b, npages = page_table.shape
    nsel = min(TOPK_MAX, (npages * PAGE + 1) // 4)
    per_b = lambda bi, pt: (bi, 0, 0)
    grid_spec = pltpu.PrefetchScalarGridSpec(
        num_scalar_prefetch=1,
        grid=(b,),
        in_specs=[pl.BlockSpec((1, N_IDX_HEADS, IDX_DIM), per_b),
                  pl.BlockSpec((1, N_IDX_HEADS, 1), per_b),
                  pl.BlockSpec(memory_space=pl.ANY),
                  pl.BlockSpec((1, 1, IDX_DIM), per_b)],
        out_specs=[pl.BlockSpec((1, npages, PAGE * N_KV), per_b), pl.BlockSpec((1, 1, LANES), per_b)],
        scratch_shapes=[pltpu.VMEM((2, npages, IDX_DIM, PAGE), F32), pltpu.SemaphoreType.DMA((2,)),
                        pltpu.VMEM((npages, PAGE), F32)])
    return pl.pallas_call(
        functools.partial(_dsa_sample_index_kernel, layer=layer, nsel=nsel),
        grid_spec=grid_spec,
        out_shape=[jax.ShapeDtypeStruct((b, npages, PAGE * N_KV), F32), jax.ShapeDtypeStruct((b, 1, LANES), F32)],
        compiler_params=_params(("arbitrary",)),
        name="dsa_sample_index",
    )(page_table, iq, iw, pool_ik_t, ik_new)


def _fox_sample_bias_kernel(pt_ref, pool_ref, lfn_ref, bias_ref, buf, sem, *, layer):
    slot = _fetch_sequence_pages(pool_ref, layer, pt_ref, buf, sem)
    npages, h = buf.shape[1], buf.shape[2]
    rows = npages * h
    x = buf[slot].reshape(rows, PAGE)
    r = lax.broadcasted_iota(I32, (PAGE, PAGE), 0)
    c = lax.broadcasted_iota(I32, (PAGE, PAGE), 1)
    within = jnp.dot(x, (r > c).astype(F32), precision=HI, preferred_element_type=F32)
    tot = jnp.dot(x, jnp.ones((PAGE, PAGE), F32), precision=HI, preferred_element_type=F32)
    row = lax.broadcasted_iota(I32, (rows, PAGE), 0)
    suffix = tot
    shift = h
    while shift < rows:
        suffix = suffix + jnp.where(row + shift < rows, pltpu.roll(suffix, rows - shift, 0), 0.0)
        shift *= 2
    per_tok = within + (suffix - tot)
    per_row = jnp.dot(per_tok, _row_expansion().astype(F32), precision=HI, preferred_element_type=F32)
    bias_ref[0] = (per_row.reshape(npages, h, PAGE * N_KV) + lfn_ref[0][None]) * LOG2E


def fox_sample_bias(page_table, pool_lf_t, layer, lf_new):
    b, npages = page_table.shape
    h = pool_lf_t.shape[2]
    grid_spec = pltpu.PrefetchScalarGridSpec(
        num_scalar_prefetch=1,
        grid=(b,),
        in_specs=[pl.BlockSpec(memory_space=pl.ANY),
                  pl.BlockSpec((1, h, 1), lambda bi, pt: (bi, 0, 0))],
        out_specs=pl.BlockSpec((1, npages, h, PAGE * N_KV), lambda bi, pt: (bi, 0, 0, 0)),
        scratch_shapes=[pltpu.VMEM((2, npages, h, PAGE), F32), pltpu.SemaphoreType.DMA((2,))])
    return pl.pallas_call(
        functools.partial(_fox_sample_bias_kernel, layer=layer),
        grid_spec=grid_spec,
        out_shape=jax.ShapeDtypeStruct((b, npages, h, PAGE * N_KV), F32),
        compiler_params=_params(("arbitrary",)),
        name="fox_sample_bias",
    )(page_table, pool_lf_t, lf_new)


def _decode_kernel(pt_ref, q_ref, pk_ref, pv_ref, b_ref, kn_ref, vn_ref, bn_ref, o_ref, kbuf, vbuf, sem,
                   *, layer, per_head_bias):
    b = pl.program_id(0)
    npages = b_ref.shape[1]
    pg = kbuf.shape[1]
    ngroups = npages // pg
    rows = PAGE * N_KV
    q = q_ref[0]
    row_kv = lax.broadcasted_iota(I32, (N_HEADS, rows), 1) % N_KV
    head_mask = jnp.where(row_kv == lax.broadcasted_iota(I32, (N_HEADS, rows), 0) // GROUP, 0.0, NEG)

    def copies(g, slot):
        out = []
        for i in range(pg):
            page = pt_ref[b, g * pg + i]
            out.append(pltpu.make_async_copy(pk_ref.at[layer, page], kbuf.at[slot, i], sem.at[0, slot]))
            out.append(pltpu.make_async_copy(pv_ref.at[layer, page], vbuf.at[slot, i], sem.at[1, slot]))
        return out

    for cp in copies(0, 0):
        cp.start()

    def body(g, carry):
        m_run, l_run, acc = carry
        slot = g % 2

        @pl.when(g + 1 < ngroups)
        def _():
            for cp in copies(g + 1, 1 - slot):
                cp.start()

        for cp in copies(g, slot):
            cp.wait()
        logits = []
        for i in range(pg):
            s = lax.dot_general(q, kbuf[slot, i].astype(BF16), NT, preferred_element_type=F32)
            page_bias = b_ref[0, g * pg + i] if per_head_bias else b_ref[0, pl.ds(g * pg + i, 1), :]
            logits.append(s + page_bias + head_mask)
        s_all = jnp.concatenate(logits, axis=1)
        m_new = jnp.maximum(m_run, jnp.max(s_all, axis=1, keepdims=True))
        alpha = jnp.exp2(m_run - m_new)
        p = jnp.exp2(s_all - m_new)
        l_new = alpha * l_run + jnp.sum(p, axis=1, keepdims=True)
        pv = jnp.zeros((N_HEADS, HEAD_DIM), F32)
        for i in range(pg):
            pb = p[:, i * rows:(i + 1) * rows].astype(BF16)
            pv = pv + jnp.dot(pb, vbuf[slot, i].astype(BF16), preferred_element_type=F32)
        return m_new, l_new, alpha * acc + pv

    init = (jnp.full((N_HEADS, 1), NEG, F32), jnp.zeros((N_HEADS, 1), F32), jnp.zeros((N_HEADS, HEAD_DIM), F32))
    m_run, l_run, acc = lax.fori_loop(0, ngroups, body, init)

    kn = kn_ref[0].astype(BF16).astype(F32)
    s_n = jnp.sum(q.astype(F32) * kn, axis=1, keepdims=True) + bn_ref[0]
    m_fin = jnp.maximum(m_run, s_n)
    alpha = jnp.exp2(m_run - m_fin)
    p_n = jnp.exp2(s_n - m_fin)
    l_fin = alpha * l_run + p_n
    vn = vn_ref[0].astype(BF16).astype(F32)
    o_ref[0] = (alpha * acc + p_n.astype(BF16).astype(F32) * vn) / l_fin


def paged_decode_attention(page_table, q, cache_k, cache_v, layer, bias, k_new, v_new, bias_new):
    b, npages = page_table.shape
    rows = PAGE * N_KV
    nl, npool = cache_k.shape[:2]
    per_head_bias = bias.ndim == 4
    bias_spec = (pl.BlockSpec((1, npages, N_HEADS, rows), lambda bi, pt: (bi, 0, 0, 0)) if per_head_bias
                 else pl.BlockSpec((1, npages, rows), lambda bi, pt: (bi, 0, 0)))
    pg = math.gcd(DECODE_PAGES, npages)
    per_b = lambda bi, pt: (bi, 0, 0)
    per_head = lambda x: jnp.repeat(x.reshape(b, N_KV, HEAD_DIM), GROUP, axis=1)
    grid_spec = pltpu.PrefetchScalarGridSpec(
        num_scalar_prefetch=1,
        grid=(b,),
        in_specs=[pl.BlockSpec((1, N_HEADS, HEAD_DIM), per_b),
                  pl.BlockSpec(memory_space=pl.ANY),
                  pl.BlockSpec(memory_space=pl.ANY),
                  bias_spec,
                  pl.BlockSpec((1, N_HEADS, HEAD_DIM), per_b),
                  pl.BlockSpec((1, N_HEADS, HEAD_DIM), per_b),
                  pl.BlockSpec((1, N_HEADS, 1), per_b)],
        out_specs=pl.BlockSpec((1, N_HEADS, HEAD_DIM), per_b),
        scratch_shapes=[pltpu.VMEM((2, pg, rows, HEAD_DIM), F32),
                        pltpu.VMEM((2, pg, rows, HEAD_DIM), F32),
                        pltpu.SemaphoreType.DMA((2, 2))])
    return pl.pallas_call(
        functools.partial(_decode_kernel, layer=layer, per_head_bias=per_head_bias),
        grid_spec=grid_spec,
        out_shape=jax.ShapeDtypeStruct((b, N_HEADS, HEAD_DIM), F32),
        compiler_params=_params(("arbitrary",)),
        name="paged_decode_attention",
    )(page_table, q, cache_k.reshape(nl, npool, rows, HEAD_DIM), cache_v.reshape(nl, npool, rows, HEAD_DIM),
      bias, per_head(k_new), per_head(v_new), bias_new)


def _row_expansion():
    t = lax.broadcasted_iota(I32, (PAGE, PAGE * N_KV), 0)
    r = lax.broadcasted_iota(I32, (PAGE, PAGE * N_KV), 1)
    return r // N_KV == t


def _cmul(ar, ai, br, bi):
    return ar * br - ai * bi, ar * bi + ai * br


def _ssm_tables(a_re, a_im, log_dt, b_re, b_im, c_re, c_im, d_skip):
    L, cdim = SSM_CHUNK, SSM_GROUP
    g = a_re.shape[0]
    dt = jnp.exp(log_dt)[:, None]
    mag = jnp.exp(a_re * dt)
    ab = (mag * jnp.cos(a_im * dt), mag * jnp.sin(a_im * dt))
    den = a_re * a_re + a_im * a_im
    num = _cmul(ab[0] - 1.0, ab[1], a_re, -a_im)
    sc = (num[0] / den, num[1] / den)
    bs = _cmul(b_re, b_im, sc[0][:, :, None], sc[1][:, :, None])
    pows = [(jnp.ones_like(ab[0]), jnp.zeros_like(ab[0]))]
    for _ in range(L):
        pows.append(_cmul(*pows[-1], *ab))
    pw = (jnp.stack([p[0] for p in pows]), jnp.stack([p[1] for p in pows]))
    ca = _cmul(c_re[None], c_im[None], pw[0][:L, :, None, :], pw[1][:L, :, None, :])
    def contract_n(x, y):
        return jnp.sum(x.transpose(0, 1, 3, 2)[..., None] * y[None, :, :, None, :], axis=2)

    kd = contract_n(ca[0], bs[0]) - contract_n(ca[1], bs[1])
    dmat = jnp.eye(cdim, dtype=F32)[None] * d_skip.reshape(g, cdim)[:, :, None]
    kd = kd.at[0].add(dmat)
    pos = _ssm_slot_positions(g)
    pw_g = (pw[0].transpose(1, 0, 2), pw[1].transpose(1, 0, 2))

    def powers_at(idx):
        return tuple(jnp.take_along_axis(x, idx[:, :, None], axis=1) for x in pw_g)

    diff = pos[:, None, :] - pos[:, :, None]
    lag = jnp.clip(diff, 0, L - 1).reshape(g, L * L)
    kt = jnp.take_along_axis(kd.transpose(1, 0, 2, 3), lag[:, :, None, None], axis=1)
    kt = jnp.where((diff >= 0)[:, :, :, None, None], kt.reshape(g, L, L, cdim, cdim), 0.0)
    m = kt.transpose(0, 1, 4, 2, 3).reshape(g, L * cdim, L * cdim)
    pp = powers_at(L - 1 - pos)
    bst = (bs[0].transpose(0, 2, 1)[:, None], bs[1].transpose(0, 2, 1)[:, None])
    pj = _cmul(pp[0][:, :, None, :], pp[1][:, :, None, :], *bst)
    p = jnp.concatenate([x.reshape(g, L * cdim, -1) for x in pj], axis=-1)
    pq = powers_at(pos + 1)
    cp = _cmul(c_re[:, None], c_im[:, None], pq[0][:, :, None, :], pq[1][:, :, None, :])
    cp = [x.transpose(0, 3, 1, 2).reshape(g, -1, L * cdim) for x in cp]
    q = jnp.concatenate([cp[0], -cp[1]], axis=1)
    return m.astype(BF16), p, q.astype(BF16), pw[0][L], pw[1][L], ab, bs


def _ssm_slot_positions(g):
    gpb = LANES // SSM_GROUP
    slot = jnp.arange(SSM_CHUNK, dtype=I32)[None, :]
    gg = (jnp.arange(g, dtype=I32) % gpb)[:, None]
    return gpb * (slot // gpb) + (slot % gpb - gg) % gpb


def _ssm_kernel(u_ref, m_ref, p_ref, q_ref, ar_ref, ai_ref, y_ref, hf_ref, ug_ref, yg_ref, *, nb):
    L, cdim, n = SSM_CHUNK, SSM_GROUP, SSM_STATE
    gpb = LANES // cdim
    r = ug_ref.shape[1]
    nk = r // nb
    for j in range(L):
        xj = u_ref[pl.ds(j, r, stride=L), :]
        if j % gpb:
            xj = pltpu.roll(xj, (j % gpb) * cdim, 1)
        for gg in range(gpb):
            b = (gg + j) % gpb
            lanes = slice((j // gpb) * LANES + b * cdim, (j // gpb) * LANES + (b + 1) * cdim)
            ug_ref[gg, :, lanes] = xj[:, b * cdim:(b + 1) * cdim]
    row = lax.broadcasted_iota(I32, (r, 2 * n), 0) % nk
    lane = lax.broadcasted_iota(I32, (1, 2 * n), 1)

    pair = 2

    def group_body(gi, _):
        ggs = [gi * pair + s for s in range(pair)]
        us = [ug_ref[gg] for gg in ggs]
        hs = [jnp.dot(u, p_ref[gg], precision=HI, preferred_element_type=F32) for u, gg in zip(us, ggs)]
        drs = [jnp.concatenate([ar_ref[gg], ar_ref[gg]], axis=1) for gg in ggs]
        dis = [jnp.concatenate([-ai_ref[gg], ai_ref[gg]], axis=1) for gg in ggs]
        shift = 1
        while shift < nk:
            for s in range(pair):
                prev = jnp.where(row >= shift, pltpu.roll(hs[s], shift, 0), 0.0)
                hs[s] = hs[s] + drs[s] * prev + dis[s] * pltpu.roll(prev, n, 1)
                re, im = drs[s], jnp.where(lane < n, -dis[s], dis[s])
                re2, im2 = re * re - im * im, 2.0 * re * im
                drs[s], dis[s] = re2, jnp.where(lane < n, -im2, im2)
            shift *= 2
        for s, gg in enumerate(ggs):
            hprev = jnp.where(row >= 1, pltpu.roll(hs[s], 1, 0), 0.0)
            yg_ref[gg] = (jnp.dot(us[s].astype(BF16), m_ref[gg], preferred_element_type=F32)
                          + jnp.dot(hprev.astype(BF16), q_ref[gg], preferred_element_type=F32))
            for b in range(nb):
                hf_ref[gg, b:b + 1, :] = hs[s][(b + 1) * nk - 1:(b + 1) * nk, :]
        return 0

    lax.fori_loop(0, gpb // pair, group_body, 0)
    slot_io = lax.broadcasted_iota(I32, (r, LANES), 1) // cdim
    for j in range(L):
        half = slice((j // gpb) * LANES, (j // gpb + 1) * LANES)
        z = yg_ref[0, :, half]
        for gg in range(1, gpb):
            z = jnp.where(slot_io == (gg + j) % gpb, yg_ref[gg, :, half], z)
        if j % gpb:
            z = pltpu.roll(z, LANES - (j % gpb) * cdim, 1)
        y_ref[pl.ds(j, r, stride=L), :] = z


def ssm_prompt(u, m, p, q, al_re, al_im, nb):
    rows, d = u.shape
    g, w, n2 = p.shape
    gpb = LANES // SSM_GROUP
    r = rows // SSM_CHUNK
    gmap = lambda i: (i, 0, 0)
    return pl.pallas_call(
        functools.partial(_ssm_kernel, nb=nb),
        grid=(d // LANES,),
        in_specs=[pl.BlockSpec((rows, LANES), lambda i: (0, i)),
                  pl.BlockSpec((gpb, w, w), gmap), pl.BlockSpec((gpb, w, n2), gmap), pl.BlockSpec((gpb, n2, w), gmap),
                  pl.BlockSpec((gpb, 1, n2 // 2), gmap), pl.BlockSpec((gpb, 1, n2 // 2), gmap)],
        out_specs=[pl.BlockSpec((rows, LANES), lambda i: (0, i)), pl.BlockSpec((gpb, nb, n2), gmap)],
        out_shape=[jax.ShapeDtypeStruct((rows, d), F32), jax.ShapeDtypeStruct((g, nb, n2), F32)],
        scratch_shapes=[pltpu.VMEM((gpb, r, w), F32), pltpu.VMEM((gpb, r, w), F32)],
        compiler_params=_params(("parallel",)),
        name="ssm_prompt",
    )(u, m, p, q, al_re.reshape(g, 1, -1), al_im.reshape(g, 1, -1))


def _ssm_step_kernel(u_ref, hr_ref, hi_ref, ar_ref, ai_ref, br_ref, bi_ref, cr_ref, ci_ref, d_ref,
                     y_ref, or_ref, oi_ref):
    u = u_ref[0]
    bur = jnp.zeros(hr_ref.shape[1:], F32)
    bui = jnp.zeros(hr_ref.shape[1:], F32)
    for c in range(SSM_GROUP):
        uc = u[:, c:c + 1]
        bur = bur + br_ref[c] * uc
        bui = bui + bi_ref[c] * uc
    ar, ai = ar_ref[...], ai_ref[...]
    h0r, h0i = hr_ref[0], hi_ref[0]
    hr = ar * h0r - ai * h0i + bur
    hi = ar * h0i + ai * h0r + bui
    or_ref[0] = hr
    oi_ref[0] = hi
    lane = lax.broadcasted_iota(I32, u.shape, 1)
    y = d_ref[...] * u
    for c in range(SSM_GROUP):
        col = jnp.sum(cr_ref[c] * hr - ci_ref[c] * hi, axis=1, keepdims=True)
        y = y + jnp.where(lane == c, col, 0.0)
    y_ref[0] = y


def ssm_step(u, h_re, h_im, a_bar, bs, c_re, c_im, d_skip):
    b, g, c = u.shape
    n = h_re.shape[2]
    per_b = lambda i: (i, 0, 0)
    c3 = lambda i: (0, 0, 0)
    c2 = lambda i: (0, 0)
    return pl.pallas_call(
        _ssm_step_kernel,
        grid=(b,),
        in_specs=[pl.BlockSpec((1, g, c), per_b), pl.BlockSpec((1, g, n), per_b), pl.BlockSpec((1, g, n), per_b),
                  pl.BlockSpec((g, n), c2), pl.BlockSpec((g, n), c2),
                  pl.BlockSpec((c, g, n), c3), pl.BlockSpec((c, g, n), c3),
                  pl.BlockSpec((c, g, n), c3), pl.BlockSpec((c, g, n), c3),
                  pl.BlockSpec((g, c), c2)],
        out_specs=[pl.BlockSpec((1, g, c), per_b), pl.BlockSpec((1, g, n), per_b), pl.BlockSpec((1, g, n), per_b)],
        out_shape=[jax.ShapeDtypeStruct((b, g, c), F32), jax.ShapeDtypeStruct((b, g, n), F32),
                   jax.ShapeDtypeStruct((b, g, n), F32)],
        compiler_params=_params(("parallel",)),
        name="ssm_step",
    )(u, h_re, h_im, a_bar[0], a_bar[1], bs[0].transpose(2, 0, 1), bs[1].transpose(2, 0, 1),
      c_re.transpose(1, 0, 2), c_im.transpose(1, 0, 2), d_skip.reshape(g, c))


def _dsa_layer(xp, xs, gain, cache_k, cache_v, cache_ik, layer, page_table, w_in, w_out, q_gain, k_gain, ik_gain,
               bp, t, past_len, tm):
    mp, ms = xp.shape[0], xs.shape[0]
    nkv = N_KV * HEAD_DIM
    nq = t // TQ
    proj = norm_proj(xp, gain, w_in, layer, tm)
    q, k, v, ik, iw, kh, vt, ikb, iq_t = dsa_post(proj, jnp.arange(t), q_gain, k_gain, ik_gain, KC, True)
    iw_t = iw.reshape(bp, nq, TQ, N_IDX_HEADS).transpose(0, 1, 3, 2)
    o = dsa_attention(iq_t.reshape(bp, nq * IDX_DIM, N_IDX_HEADS * TQ), iw_t, ikb.reshape(bp, t, IDX_DIM),
                      q.reshape(bp, t, -1), kh.reshape(N_KV, bp, t, HEAD_DIM),
                      vt.reshape(N_KV, bp, t // KC, HEAD_DIM, KC))
    xp = out_res(xp, o.reshape(mp, -1), w_out, layer, tm)
    projs = norm_proj(xs, gain, w_in, layer, ms)
    pos_s = jnp.full((ms,), past_len, I32)
    qs, ks, vs, iks, iws, iqs = dsa_post(projs, pos_s, q_gain, k_gain, ik_gain, ms, False)
    bias, bias_n = dsa_sample_index(page_table, iqs.reshape(ms, N_IDX_HEADS, IDX_DIM),
                                    iws.reshape(ms, N_IDX_HEADS, 1), cache_ik.transpose(0, 1, 3, 2), layer,
                                    iks.reshape(ms, 1, IDX_DIM))
    bias_nh = jnp.broadcast_to(bias_n[:, :, :1], (ms, N_HEADS, 1))
    os_ = paged_decode_attention(page_table, qs.reshape(ms, N_HEADS, HEAD_DIM), cache_k, cache_v, layer, bias,
                                 ks, vs, bias_nh)
    xs = out_res(xs, os_.reshape(ms, -1), w_out, layer, ms)
    return xp, xs, (k, v, ik), (ks, vs, iks)


def _key_bias_columns(parts, bp, t):
    cols = jnp.stack(parts, axis=-1).reshape(bp, t, N_KV, GROUP * 3).transpose(0, 2, 1, 3)
    return jnp.pad(cols, ((0, 0), (0, 0), (0, 0), (0, HEAD_DIM - GROUP * 3)))


def _fox_layer(xp, xs, gain, cache_k, cache_v, cache_lf, layer, page_table, w_in, b_f, w_out, q_gain, k_gain,
               bp, t, tm):
    mp, ms = xp.shape[0], xs.shape[0]
    nkv = N_KV * HEAD_DIM
    proj = norm_proj(xp, gain, w_in, layer, tm)
    q, k, v, lf, kh, vt = fox_post(proj, b_f, q_gain, k_gain, KC, True)
    c_parts = cumsum_rows_split(lf.reshape(bp, t, N_HEADS))
    o = fox_attention(q.reshape(bp, t, -1), kh.reshape(N_KV, bp, t, HEAD_DIM), _key_bias_columns(c_parts, bp, t),
                      vt.reshape(N_KV, bp, t // KC, HEAD_DIM, KC))
    xp = out_res(xp, o.reshape(mp, -1), w_out, layer, tm)
    projs = norm_proj(xs, gain, w_in, layer, ms)
    qs, ks, vs, lfs = fox_post(projs, b_f, q_gain, k_gain, ms, False)
    bias = fox_sample_bias(page_table, cache_lf.transpose(0, 1, 3, 2), layer, lfs.reshape(ms, N_HEADS, 1))
    os_ = paged_decode_attention(page_table, qs.reshape(ms, N_HEADS, HEAD_DIM), cache_k, cache_v, layer, bias,
                                 ks, vs, jnp.zeros((ms, N_HEADS, 1), F32))
    xs = out_res(xs, os_.reshape(ms, -1), w_out, layer, ms)
    return xp, xs, (k, v, lf), (ks, vs, lfs)


def _ssm_layer(xp, xs, gain, h0_re, h0_im, a_re, a_im, log_dt, b_re, b_im, c_re, c_im, d_skip, w_glu, layer,
               bp, t, tm):
    mp, ms = xp.shape[0], xs.shape[0]
    g, n, cdim, L = N_SSM_GROUPS, SSM_STATE, SSM_GROUP, SSM_CHUNK
    m, p, q, al_re, al_im, a_bar, bs = _ssm_tables(a_re, a_im, log_dt, b_re, b_im, c_re, c_im, d_skip)
    u = rmsnorm(xp, gain, tm)
    y, hf = ssm_prompt(u, m, p, q, al_re, al_im, bp)
    xp = glu_res(xp, y, w_glu, layer, tm)
    hf = hf.transpose(1, 0, 2)
    us = rmsnorm(xs, gain, ms)
    ys, hs_re, hs_im = ssm_step(us.reshape(ms, g, cdim), h0_re, h0_im, a_bar, bs, c_re, c_im, d_skip)
    xs = glu_res(xs, ys.reshape(ms, g * cdim), w_glu, layer, ms)
    return xp, xs, (hf[..., :n], hf[..., n:]), (hs_re, hs_im)


def kernel(x_prompt, x_sample, cache_dsa_k, cache_dsa_v, cache_dsa_idx_k, cache_fox_k, cache_fox_v, cache_fox_logf, state_ssm_re, state_ssm_im, page_table, norm_mix, norm_ffn, dsa_w_in, dsa_w_out, dsa_q_gain, dsa_k_gain, dsa_ik_gain, fox_w_in, fox_b_f, fox_w_out, fox_q_gain, fox_k_gain, ssm_a_re, ssm_a_im, ssm_log_dt, ssm_b_re, ssm_b_im, ssm_c_re, ssm_c_im, ssm_d, ssm_w_glu, ffn_w_gu, ffn_w_down):
    bp, t, d = x_prompt.shape
    bs_, ts, _ = x_sample.shape
    assert ts == 1, "the sample group carries one new position per sequence"
    assert t % KC == 0 and KC % TQ == 0
    depth = norm_mix.shape[0]
    past_len = page_table.shape[1] * PAGE
    tm = min(512, bp * t)
    xp = x_prompt.reshape(bp * t, d)
    xs = x_sample.reshape(bs_ * ts, d)
    dsa_w_in, dsa_w_out, fox_w_in, fox_w_out, ssm_w_glu, ffn_w_gu, ffn_w_down = (
        w.astype(BF16) for w in (dsa_w_in, dsa_w_out, fox_w_in, fox_w_out, ssm_w_glu, ffn_w_gu, ffn_w_down))
    dsa_p, dsa_s, fox_p, fox_s, ssm_p, ssm_s = [], [], [], [], [], []
    for i in range(depth):
        kind, j = i % N_MIXERS, i // N_MIXERS
        if kind == 0:
            xp, xs, outp, outs = _dsa_layer(
                xp, xs, norm_mix[i], cache_dsa_k, cache_dsa_v, cache_dsa_idx_k, j, page_table,
                dsa_w_in, dsa_w_out, dsa_q_gain[j], dsa_k_gain[j], dsa_ik_gain[j], bp, t, past_len, tm)
            dsa_p.append(outp); dsa_s.append(outs)
        elif kind == 1:
            xp, xs, outp, outs = _fox_layer(
                xp, xs, norm_mix[i], cache_fox_k, cache_fox_v, cache_fox_logf, j, page_table,
                fox_w_in, fox_b_f[j], fox_w_out, fox_q_gain[j], fox_k_gain[j], bp, t, tm)
            fox_p.append(outp); fox_s.append(outs)
        else:
            xp, xs, outp, outs = _ssm_layer(
                xp, xs, norm_mix[i], state_ssm_re[j], state_ssm_im[j], ssm_a_re[j], ssm_a_im[j], ssm_log_dt[j],
                ssm_b_re[j], ssm_b_im[j], ssm_c_re[j], ssm_c_im[j], ssm_d[j], ssm_w_glu, j, bp, t, tm)
            ssm_p.append(outp); ssm_s.append(outs)
        xp = ffn_res(xp, norm_ffn[i], ffn_w_gu, ffn_w_down, i, tm)
        xs = ffn_res(xs, norm_ffn[i], ffn_w_gu, ffn_w_down, i, bs_ * ts)

    def stack(items, idx, shape):
        return jnp.stack([it[idx].reshape(shape) for it in items])

    kv_p, kv_s = (bp, t, N_KV, HEAD_DIM), (bs_, ts, N_KV, HEAD_DIM)
    st_p, st_s = (bp, N_SSM_GROUPS, SSM_STATE), (bs_, N_SSM_GROUPS, SSM_STATE)
    return (xp.reshape(bp, t, d), xs.reshape(bs_, ts, d),
            stack(dsa_p, 0, kv_p), stack(dsa_p, 1, kv_p), stack(dsa_p, 2, (bp, t, IDX_DIM)),
            stack(dsa_s, 0, kv_s), stack(dsa_s, 1, kv_s), stack(dsa_s, 2, (bs_, ts, IDX_DIM)),
            stack(fox_p, 0, kv_p), stack(fox_p, 1, kv_p), stack(fox_p, 2, (bp, t, N_HEADS)),
            stack(fox_s, 0, kv_s), stack(fox_s, 1, kv_s), stack(fox_s, 2, (bs_, ts, N_HEADS)),
            stack(ssm_p, 0, st_p), stack(ssm_p, 1, st_p), stack(ssm_s, 0, st_s), stack(ssm_s, 1, st_s))
```

```python
import functools
import math

import jax
import jax.numpy as jnp
from jax import lax
from jax.experimental import pallas as pl
from jax.experimental.pallas import tpu as pltpu

F32 = jnp.float32
BF16 = jnp.bfloat16
I32 = jnp.int32

D_MODEL = 2048
N_HEADS = 16
HEAD_DIM = 128
N_KV = 4
GROUP = N_HEADS // N_KV
N_IDX_HEADS = 16
IDX_DIM = 64
TOPK_MAX = 256
ROPE_THETA = 500000.0
PAGE = 128
SSM_GROUP = 16
N_SSM_GROUPS = D_MODEL // SSM_GROUP
SSM_STATE = 64
SSM_CHUNK = 16
EPS = 1e-6
N_MIXERS = 3

LANES = 128
SUBLANES = 8
TQ = 128
KC = 256
DECODE_PAGES = 8
LOG2E = math.log2(math.e)
NEG = -1e30
BISECT_MAX = 64
HI = lax.Precision.HIGHEST
VMEM_LIMIT = 56 * 1024 * 1024

NT = (((1,), (1,)), ((), ()))


def _params(sem):
    return pltpu.CompilerParams(dimension_semantics=sem, vmem_limit_bytes=VMEM_LIMIT)


def _rms(x, gain):
    return x * lax.rsqrt(jnp.mean(x * x, axis=-1, keepdims=True) + EPS) * gain


def _proj_kernel(x_ref, g_ref, w_ref, o_ref, h_ref):
    @pl.when(pl.program_id(1) == 0)
    def _():
        h_ref[...] = _rms(x_ref[...], g_ref[...]).astype(BF16)

    o_ref[...] = jnp.dot(h_ref[...], w_ref[...], preferred_element_type=F32)


def _pick_tn(n, cap=1536):
    k = -(-n // cap)
    return -(-n // (k * LANES)) * LANES


def norm_proj(x, gain, w, layer, tm):
    m, d = x.shape
    n = w.shape[2]
    tn = _pick_tn(n)
    return pl.pallas_call(
        _proj_kernel,
        grid=(m // tm, pl.cdiv(n, tn)),
        in_specs=[pl.BlockSpec((tm, d), lambda i, j: (i, 0)),
                  pl.BlockSpec((1, d), lambda i, j: (0, 0)),
                  pl.BlockSpec((None, d, tn), lambda i, j: (layer, 0, j))],
        out_specs=pl.BlockSpec((tm, tn), lambda i, j: (i, j)),
        out_shape=jax.ShapeDtypeStruct((m, n), F32),
        scratch_shapes=[pltpu.VMEM((tm, d), BF16)],
        compiler_params=_params(("parallel", "arbitrary")),
        name="norm_proj",
    )(x, gain.reshape(1, d), w)


def _out_res_kernel(x_ref, o_ref, w_ref, y_ref):
    y_ref[...] = x_ref[...] + jnp.dot(o_ref[...].astype(BF16), w_ref[...], preferred_element_type=F32)


def out_res(x, o, w, layer, tm):
    m, d = x.shape
    k = w.shape[1]
    return pl.pallas_call(
        _out_res_kernel,
        grid=(m // tm,),
        in_specs=[pl.BlockSpec((tm, d), lambda i: (i, 0)),
                  pl.BlockSpec((tm, k), lambda i: (i, 0)),
                  pl.BlockSpec((None, k, d), lambda i: (layer, 0, 0))],
        out_specs=pl.BlockSpec((tm, d), lambda i: (i, 0)),
        out_shape=jax.ShapeDtypeStruct((m, d), F32),
        compiler_params=_params(("parallel",)),
        name="out_res",
    )(x, o, w)


def _ffn_kernel(x_ref, g_ref, wg_ref, wu_ref, wd_ref, y_ref, h_ref, acc_ref):
    f = pl.program_id(1)

    @pl.when(f == 0)
    def _():
        h_ref[...] = _rms(x_ref[...], g_ref[...]).astype(BF16)
        acc_ref[...] = jnp.zeros_like(acc_ref)

    h = h_ref[...]
    g = jnp.dot(h, wg_ref[...], preferred_element_type=F32)
    u = jnp.dot(h, wu_ref[...], preferred_element_type=F32)
    a = (g * jax.nn.sigmoid(g) * u).astype(BF16)
    acc_ref[...] += jnp.dot(a, wd_ref[...], preferred_element_type=F32)

    @pl.when(f == pl.num_programs(1) - 1)
    def _():
        y_ref[...] = x_ref[...] + acc_ref[...]


def ffn_res(x, gain, w_gu, w_down, layer, tm, tf=512):
    m, d = x.shape
    dff = w_down.shape[1]
    nf = dff // tf
    return pl.pallas_call(
        _ffn_kernel,
        grid=(m // tm, nf),
        in_specs=[pl.BlockSpec((tm, d), lambda i, f: (i, 0)),
                  pl.BlockSpec((1, d), lambda i, f: (0, 0)),
                  pl.BlockSpec((None, d, tf), lambda i, f: (layer, 0, f)),
                  pl.BlockSpec((None, d, tf), lambda i, f: (layer, 0, f + nf)),
                  pl.BlockSpec((None, tf, d), lambda i, f: (layer, f, 0))],
        out_specs=pl.BlockSpec((tm, d), lambda i, f: (i, 0)),
        out_shape=jax.ShapeDtypeStruct((m, d), F32),
        scratch_shapes=[pltpu.VMEM((tm, d), BF16), pltpu.VMEM((tm, d), F32)],
        compiler_params=_params(("parallel", "arbitrary")),
        name="ffn_res",
    )(x, gain.reshape(1, d), w_gu, w_gu, w_down)


def _gelu_tanh(y):
    return 0.5 * y * (1.0 + jnp.tanh(math.sqrt(2.0 / math.pi) * (y + 0.044715 * (y * y * y))))


def _glu_kernel(x_ref, y_ref, wa_ref, wb_ref, o_ref, g_ref):
    @pl.when(pl.program_id(1) == 0)
    def _():
        g_ref[...] = _gelu_tanh(y_ref[...]).astype(BF16)

    g = g_ref[...]
    a = jnp.dot(g, wa_ref[...], preferred_element_type=F32)
    b = jnp.dot(g, wb_ref[...], preferred_element_type=F32)
    o_ref[...] = x_ref[...] + a * jax.nn.sigmoid(b)


def glu_res(x, y, w_glu, layer, tm, tn=512):
    m, d = x.shape
    nj = d // tn
    return pl.pallas_call(
        _glu_kernel,
        grid=(m // tm, nj),
        in_specs=[pl.BlockSpec((tm, tn), lambda i, j: (i, j)),
                  pl.BlockSpec((tm, d), lambda i, j: (i, 0)),
                  pl.BlockSpec((None, d, tn), lambda i, j: (layer, 0, j)),
                  pl.BlockSpec((None, d, tn), lambda i, j: (layer, 0, j + nj))],
        out_specs=pl.BlockSpec((tm, tn), lambda i, j: (i, j)),
        out_shape=jax.ShapeDtypeStruct((m, d), F32),
        scratch_shapes=[pltpu.VMEM((tm, d), BF16)],
        compiler_params=_params(("parallel", "arbitrary")),
        name="glu_res",
    )(x, y, w_glu, w_glu)


def _rmsnorm_kernel(x_ref, g_ref, o_ref):
    o_ref[...] = _rms(x_ref[...], g_ref[...])


def rmsnorm(x, gain, tm):
    m, d = x.shape
    return pl.pallas_call(
        _rmsnorm_kernel,
        grid=(m // tm,),
        in_specs=[pl.BlockSpec((tm, d), lambda i: (i, 0)), pl.BlockSpec((1, d), lambda i: (0, 0))],
        out_specs=pl.BlockSpec((tm, d), lambda i: (i, 0)),
        out_shape=jax.ShapeDtypeStruct((m, d), F32),
        compiler_params=_params(("parallel",)),
        name="rmsnorm",
    )(x, gain.reshape(1, d))


def _rope_tables(pos, width, period):
    rot = period // 4
    half = rot // 2
    inv = ROPE_THETA ** (-jnp.arange(half, dtype=F32) / half)
    ang = pos.astype(F32)[:, None] * inv[None, :]
    cos, sin = jnp.cos(ang), jnp.sin(ang)
    n = pos.shape[0]
    ones = jnp.ones((n, period - rot), F32)
    zeros = jnp.zeros((n, period - rot), F32)
    zh = jnp.zeros((n, half), F32)
    c = jnp.concatenate([cos, cos, ones], axis=1)
    s_up = jnp.concatenate([zh, sin, zeros], axis=1)
    s_dn = jnp.concatenate([-sin, zh, zeros], axis=1)
    reps = width // period
    return tuple(jnp.tile(a, (1, reps)) for a in (c, s_up, s_dn))


def _rope(x, cos, s_up, s_dn, half):
    w = x.shape[-1]
    return x * cos + pltpu.roll(x, half, 1) * s_up + pltpu.roll(x, w - half, 1) * s_dn


def _store_kv_heads(p_ref, k0, v0, k_fn, k_ref, v_ref, kh_ref, vt_ref):
    for h in range(N_KV):
        y = k_fn(p_ref[:, k0 + h * HEAD_DIM:k0 + (h + 1) * HEAD_DIM])
        k_ref[:, h * HEAD_DIM:(h + 1) * HEAD_DIM] = y
        if kh_ref is not None:
            kh_ref[h] = y.astype(BF16)
            vh = p_ref[:, v0 + h * HEAD_DIM:v0 + (h + 1) * HEAD_DIM]
            vt_ref[h, 0] = vh.T.astype(BF16)
    v_ref[...] = p_ref[:, v0:v0 + N_KV * HEAD_DIM]


def _dsa_post_kernel(p_ref, qg_ref, kg_ref, ikg_ref, c1_ref, u1_ref, d1_ref, c2_ref, u2_ref, d2_ref,
                     q_ref, k_ref, v_ref, ik_ref, iw_ref, *mode_refs, prompt):
    c1, u1, d1 = c1_ref[...], u1_ref[...], d1_ref[...]
    c2, u2, d2 = c2_ref[...], u2_ref[...], d2_ref[...]
    qg, kg = qg_ref[...], kg_ref[...]
    scale = HEAD_DIM ** -0.5 * LOG2E
    for h in range(N_HEADS):
        x = p_ref[:, h * HEAD_DIM:(h + 1) * HEAD_DIM]
        y = _rope(_rms(x, qg), c1, u1, d1, HEAD_DIM // 8)
        q_ref[:, h * HEAD_DIM:(h + 1) * HEAD_DIM] = (y * scale).astype(BF16)
    k0 = N_HEADS * HEAD_DIM
    v0 = k0 + N_KV * HEAD_DIM
    kh_ref, vt_ref, ikb_ref, iqt_ref = mode_refs if prompt else (None, None, None, None)
    iq_ref = None if prompt else mode_refs[0]
    _store_kv_heads(p_ref, k0, v0, lambda x: _rope(_rms(x, kg), c1, u1, d1, HEAD_DIM // 8),
                    k_ref, v_ref, kh_ref, vt_ref)
    i0 = v0 + N_KV * HEAD_DIM
    tm = p_ref.shape[0]
    for h in range(N_IDX_HEADS * IDX_DIM // LANES):
        x = _rope(p_ref[:, i0 + h * LANES:i0 + (h + 1) * LANES], c2, u2, d2, IDX_DIM // 8)
        if prompt:
            for qb in range(tm // TQ):
                xt = x[qb * TQ:(qb + 1) * TQ].T.astype(BF16)
                iqt_ref[qb, :, (2 * h) * TQ:(2 * h + 1) * TQ] = xt[:IDX_DIM]
                iqt_ref[qb, :, (2 * h + 1) * TQ:(2 * h + 2) * TQ] = xt[IDX_DIM:]
        else:
            iq_ref[:, h * LANES:(h + 1) * LANES] = x.astype(BF16)
    j0 = i0 + N_IDX_HEADS * IDX_DIM
    xk = _rms(p_ref[:, j0:j0 + IDX_DIM], ikg_ref[...])
    xk2 = jnp.concatenate([xk, xk], axis=1)
    ik = _rope(xk2, c2, u2, d2, IDX_DIM // 8)[:, :IDX_DIM]
    ik_ref[...] = ik
    if prompt:
        ikb_ref[...] = ik.astype(BF16)
    iw_ref[...] = p_ref[:, j0 + IDX_DIM:j0 + IDX_DIM + N_IDX_HEADS] * (N_IDX_HEADS ** -0.5 * IDX_DIM ** -0.5)


def dsa_post(proj, pos, q_gain, k_gain, ik_gain, tm, prompt):
    m, n = proj.shape
    npos = pos.shape[0]
    t1 = _rope_tables(pos, HEAD_DIM, HEAD_DIM)
    t2 = _rope_tables(pos, LANES, IDX_DIM)
    nb = npos // tm
    row = lambda i: (i, 0)
    tab = lambda i: (i % nb, 0)
    const = lambda i: (0, 0)
    nkv = N_KV * HEAD_DIM
    niq = N_IDX_HEADS * IDX_DIM
    outs = [((m, N_HEADS * HEAD_DIM), BF16, pl.BlockSpec((tm, N_HEADS * HEAD_DIM), row)),
            ((m, nkv), F32, pl.BlockSpec((tm, nkv), row)),
            ((m, nkv), F32, pl.BlockSpec((tm, nkv), row)),
            ((m, IDX_DIM), F32, pl.BlockSpec((tm, IDX_DIM), row)),
            ((m, N_IDX_HEADS), F32, pl.BlockSpec((tm, N_IDX_HEADS), row))]
    if not prompt:
        outs += [((m, niq), BF16, pl.BlockSpec((tm, niq), row))]
    if prompt:
        assert tm == KC and tm % TQ == 0
        outs += [((N_KV, m, HEAD_DIM), BF16, pl.BlockSpec((N_KV, tm, HEAD_DIM), lambda i: (0, i, 0))),
                 ((N_KV, m // KC, HEAD_DIM, KC), BF16, pl.BlockSpec((N_KV, 1, HEAD_DIM, KC), lambda i: (0, i, 0, 0))),
                 ((m, IDX_DIM), BF16, pl.BlockSpec((tm, IDX_DIM), row)),
                 ((m // TQ, IDX_DIM, N_IDX_HEADS * TQ), BF16,
                  pl.BlockSpec((tm // TQ, IDX_DIM, N_IDX_HEADS * TQ), lambda i: (i, 0, 0)))]
    return pl.pallas_call(
        functools.partial(_dsa_post_kernel, prompt=prompt),
        grid=(m // tm,),
        in_specs=[pl.BlockSpec((tm, n), row),
                  pl.BlockSpec((1, HEAD_DIM), const), pl.BlockSpec((1, HEAD_DIM), const),
                  pl.BlockSpec((1, IDX_DIM), const)]
                 + [pl.BlockSpec((tm, LANES), tab)] * 6,
        out_specs=[o[2] for o in outs],
        out_shape=[jax.ShapeDtypeStruct(o[0], o[1]) for o in outs],
        compiler_params=_params(("parallel",)),
        name="dsa_post",
    )(proj, q_gain.reshape(1, -1), k_gain.reshape(1, -1), ik_gain.reshape(1, -1), *t1, *t2)


def _log_sigmoid(x):
    return jnp.minimum(x, 0.0) - jnp.log1p(jnp.exp(-jnp.abs(x)))


def _fox_post_kernel(p_ref, qg_ref, kg_ref, bf_ref, q_ref, k_ref, v_ref, lf_ref, *prompt_refs, prompt):
    qg, kg = qg_ref[...], kg_ref[...]
    scale = HEAD_DIM ** -0.5 * LOG2E
    for h in range(N_HEADS):
        x = p_ref[:, h * HEAD_DIM:(h + 1) * HEAD_DIM]
        q_ref[:, h * HEAD_DIM:(h + 1) * HEAD_DIM] = (_rms(x, qg) * scale).astype(BF16)
    k0 = N_HEADS * HEAD_DIM
    v0 = k0 + N_KV * HEAD_DIM
    kh_ref, vt_ref = prompt_refs if prompt else (None, None)
    _store_kv_heads(p_ref, k0, v0, lambda x: _rms(x, kg), k_ref, v_ref, kh_ref, vt_ref)
    f0 = v0 + N_KV * HEAD_DIM
    lf_ref[...] = _log_sigmoid(p_ref[:, f0:f0 + N_HEADS] + bf_ref[...])


def fox_post(proj, b_f, q_gain, k_gain, tm, prompt):
    m, n = proj.shape
    row = lambda i: (i, 0)
    const = lambda i: (0, 0)
    nkv = N_KV * HEAD_DIM
    outs = [((m, N_HEADS * HEAD_DIM), BF16, pl.BlockSpec((tm, N_HEADS * HEAD_DIM), row)),
            ((m, nkv), F32, pl.BlockSpec((tm, nkv), row)),
            ((m, nkv), F32, pl.BlockSpec((tm, nkv), row)),
            ((m, N_HEADS), F32, pl.BlockSpec((tm, N_HEADS), row))]
    if prompt:
        assert tm == KC
        outs += [((N_KV, m, HEAD_DIM), BF16, pl.BlockSpec((N_KV, tm, HEAD_DIM), lambda i: (0, i, 0))),
                 ((N_KV, m // KC, HEAD_DIM, KC), BF16, pl.BlockSpec((N_KV, 1, HEAD_DIM, KC), lambda i: (0, i, 0, 0)))]
    return pl.pallas_call(
        functools.partial(_fox_post_kernel, prompt=prompt),
        grid=(m // tm,),
        in_specs=[pl.BlockSpec((tm, n), row), pl.BlockSpec((1, HEAD_DIM), const),
                  pl.BlockSpec((1, HEAD_DIM), const), pl.BlockSpec((1, N_HEADS), const)],
        out_specs=[o[2] for o in outs],
        out_shape=[jax.ShapeDtypeStruct(o[0], o[1]) for o in outs],
        compiler_params=_params(("parallel",)),
        name="fox_post",
    )(proj, q_gain.reshape(1, -1), k_gain.reshape(1, -1), b_f.reshape(1, -1))


def _queries_t(q_ref, j):
    cols = []
    for g in range(GROUP):
        x = q_ref[0, :, (GROUP * j + g) * HEAD_DIM:(GROUP * j + g + 1) * HEAD_DIM]
        cols.append(x.astype(F32).T.astype(BF16))
    return jnp.concatenate(cols, axis=1)


def _attend_heads(s_first, logits_fn, values_fn, m_ref, l_ref, acc_ref, next_first_fn=None):
    s_next = s_first
    pending = None
    for j in range(N_KV):
        s_t = s_next
        if j + 1 < N_KV:
            s_next = logits_fn(j + 1)
        elif next_first_fn is not None:
            s_next = next_first_fn()
        m_run = m_ref[j]
        m_new = jnp.maximum(m_run, jnp.max(s_t, axis=0, keepdims=True))
        alpha = jnp.exp2(m_run - m_new)
        p = jnp.exp2(s_t - m_new)
        l_ref[j] = alpha * l_ref[j] + jnp.sum(p, axis=0, keepdims=True)
        m_ref[j] = m_new
        pv = jnp.dot(values_fn(j), p.astype(BF16), preferred_element_type=F32)
        if pending is not None:
            jp, alpha_p, pv_p = pending
            acc_ref[jp] = alpha_p * acc_ref[jp] + pv_p
        pending = (j, alpha, pv)
    jp, alpha_p, pv_p = pending
    acc_ref[jp] = alpha_p * acc_ref[jp] + pv_p
    return s_next


def _softmax_reset(m_ref, l_ref, acc_ref):
    m_ref[...] = jnp.full_like(m_ref, NEG)
    l_ref[...] = jnp.zeros_like(l_ref)
    acc_ref[...] = jnp.zeros_like(acc_ref)


def _softmax_scratch(qk_depth):
    rows = GROUP * TQ
    return [pltpu.VMEM((N_KV, qk_depth, rows), BF16), pltpu.VMEM((N_KV, 1, rows), F32),
            pltpu.VMEM((N_KV, 1, rows), F32), pltpu.VMEM((N_KV, HEAD_DIM, rows), F32)]


def _store_heads_t(o_ref, l_ref, acc_ref):
    for j in range(N_KV):
        o_t = acc_ref[j] / l_ref[j]
        for g in range(GROUP):
            o_ref[0, :, (GROUP * j + g) * HEAD_DIM:(GROUP * j + g + 1) * HEAD_DIM] = (
                o_t[:, g * TQ:(g + 1) * TQ].T.astype(o_ref.dtype))


def _cumsum_kernel(lf_ref, hi_ref, mid_ref, lo_ref):
    t = lf_ref.shape[1]
    r = lax.broadcasted_iota(I32, (LANES, LANES), 0)
    c = lax.broadcasted_iota(I32, (LANES, LANES), 1)
    lower = (c <= r).astype(F32)
    carry = jnp.zeros((1, lf_ref.shape[2]), F32)
    for b in range(t // LANES):
        rows = slice(b * LANES, (b + 1) * LANES)
        cs = jnp.dot(lower, lf_ref[0, rows, :], precision=HI, preferred_element_type=F32) + carry
        carry = cs[LANES - 1:LANES, :]
        x = cs * LOG2E
        hi = x.astype(BF16)
        r1 = x - hi.astype(F32)
        mid = r1.astype(BF16)
        hi_ref[0, rows, :] = hi
        mid_ref[0, rows, :] = mid
        lo_ref[0, rows, :] = (r1 - mid.astype(F32)).astype(BF16)


def cumsum_rows_split(lf):
    b, t, h = lf.shape
    spec = pl.BlockSpec((1, t, h), lambda i: (i, 0, 0))
    return pl.pallas_call(
        _cumsum_kernel,
        grid=(b,),
        in_specs=[spec],
        out_specs=[spec] * 3,
        out_shape=[jax.ShapeDtypeStruct((b, t, h), BF16)] * 3,
        compiler_params=_params(("parallel",)),
        name="cumsum_rows_split",
    )(lf)


def _fox_kernel(q_ref, k_ref, cp_ref, v_ref, o_ref, qt_ref, m_ref, l_ref, acc_ref):
    i = pl.program_id(1)
    rows = GROUP * TQ
    last = (i * TQ) // KC
    r_io = lax.broadcasted_iota(I32, (HEAD_DIM, rows), 0)
    g_io = lax.broadcasted_iota(I32, (HEAD_DIM, rows), 1) // TQ
    sel = jnp.where((r_io >= 3 * g_io) & (r_io < 3 * g_io + 3), -1.0, 0.0).astype(BF16)
    for j in range(N_KV):
        qt_ref[j] = jnp.concatenate([_queries_t(q_ref, j), sel], axis=0)
    _softmax_reset(m_ref, l_ref, acc_ref)

    def logits(j, c):
        start = pl.multiple_of(c * KC, KC)
        kc = jnp.concatenate([k_ref[j, 0, pl.ds(start, KC), :], cp_ref[0, j, pl.ds(start, KC), :]], axis=1)
        return jnp.dot(kc, qt_ref[j], preferred_element_type=F32)

    def body(c, s_first):
        return _attend_heads(s_first, lambda j: logits(j, c), lambda j: v_ref[j, 0, c], m_ref, l_ref, acc_ref,
                             next_first_fn=lambda: logits(0, c + 1))

    s_first = lax.fori_loop(0, last, body, logits(0, 0))
    key_io = lax.broadcasted_iota(I32, (KC, rows), 0)
    q_io = lax.broadcasted_iota(I32, (KC, rows), 1) % TQ + i * TQ
    causal = key_io + last * KC <= q_io
    _attend_heads(jnp.where(causal, s_first, NEG), lambda j: jnp.where(causal, logits(j, last), NEG),
                  lambda j: v_ref[j, 0, last], m_ref, l_ref, acc_ref)
    _store_heads_t(o_ref, l_ref, acc_ref)


def fox_attention(q, kh, cparts, vt):
    b, t, _ = q.shape
    return pl.pallas_call(
        _fox_kernel,
        grid=(b, t // TQ),
        in_specs=[pl.BlockSpec((1, TQ, N_HEADS * HEAD_DIM), lambda bi, i: (bi, i, 0)),
                  pl.BlockSpec((N_KV, 1, t, HEAD_DIM), lambda bi, i: (0, bi, 0, 0)),
                  pl.BlockSpec((1, N_KV, t, HEAD_DIM), lambda bi, i: (bi, 0, 0, 0)),
                  pl.BlockSpec((N_KV, 1, t // KC, HEAD_DIM, KC), lambda bi, i: (0, bi, 0, 0, 0))],
        out_specs=pl.BlockSpec((1, TQ, N_HEADS * HEAD_DIM), lambda bi, i: (bi, i, 0)),
        out_shape=jax.ShapeDtypeStruct((b, t, N_HEADS * HEAD_DIM), BF16),
        scratch_shapes=_softmax_scratch(2 * HEAD_DIM),
        compiler_params=_params(("parallel", "arbitrary")),
        name="fox_attention",
    )(q, kh, cparts, vt)


def _all_true(flags):
    return (jnp.min(flags.astype(F32)) > 0.5).astype(I32)


def _bisect_threshold(count_ge, lo, hi, cnt_lo, nsel):
    steps = 2

    def cond(state):
        it, _, _, _, settled, _ = state
        return (it < BISECT_MAX // steps) & (settled == 0)

    def body(state):
        it, lo, hi, cnt_lo, _, _ = state
        stuck = jnp.zeros(lo.shape, I32)
        for _ in range(steps):
            mid = lo + (hi - lo) * 0.5
            stuck = jnp.where((mid <= lo) | (mid >= hi), 1, 0)
            cnt = count_ge(mid)
            up = cnt >= nsel
            lo = jnp.where(up, mid, lo)
            cnt_lo = jnp.where(up, cnt, cnt_lo)
            hi = jnp.where(up, hi, mid)
        exact = jnp.where(cnt_lo == nsel, 1, 0)
        return it + 1, lo, hi, cnt_lo, _all_true(jnp.maximum(exact, stuck)), _all_true(exact)

    init = (jnp.int32(0), lo, hi, cnt_lo, jnp.int32(0), jnp.int32(0))
    _, lo, _, cnt_lo, _, all_exact = lax.while_loop(cond, body, init)
    return lo, cnt_lo, all_exact


def _exact_threshold(min_ge, count_gt_and_next, lo, nsel):
    def cond(state):
        it, _, _, _, _, finished = state
        return (it < BISECT_MAX) & (finished == 0)

    def body(state):
        it, v, tau, need, done, _ = state
        n_gt, v_next = count_gt_and_next(v)
        fin = (n_gt < nsel) & (done == 0)
        tau = jnp.where(fin, v, tau)
        need = jnp.where(fin, nsel - n_gt, need)
        done = jnp.where(fin, 1, done)
        v = jnp.where(done == 1, v, v_next)
        return it + 1, v, tau, need, done, _all_true(done)

    v0 = min_ge(lo)
    init = (jnp.int32(0), v0, v0, jnp.zeros(lo.shape, F32), jnp.zeros(lo.shape, I32), jnp.int32(0))
    _, _, tau, need, _, _ = lax.while_loop(cond, body, init)
    return tau, need


def _select_tie_index(count_eq_below, need, nbits, shape):
    def bit_body(bi, m):
        cand = m | jnp.left_shift(jnp.int32(1), nbits - 1 - bi)
        return jnp.where(count_eq_below(cand) < need, cand, m)

    return lax.fori_loop(0, nbits, bit_body, jnp.zeros(shape, I32))


def _dsa_kernel(iq_ref, iw_ref, ik_ref, q_ref, k_ref, v_ref, o_ref, sc_ref, bias_ref, qt_ref, m_ref, l_ref, acc_ref,
                *, nsel, t_total):
    i = pl.program_id(1)
    nch = (i * TQ + TQ + KC - 1) // KC
    w = iw_ref[0, 0]
    iq_t = iq_ref[0]
    key_io = lax.broadcasted_iota(I32, (KC, TQ), 0)
    qpos = i * TQ + lax.broadcasted_iota(I32, (KC, TQ), 1)
    inf = float("inf")

    def fold(x, op):
        part = op(x.reshape(4, KC // (4 * SUBLANES), SUBLANES, TQ), axis=1)
        return op(part, axis=0)

    def score_body(c, carry):
        lo, hi = carry
        ikc = ik_ref[0, pl.ds(pl.multiple_of(c * KC, KC), KC), :]
        y = jnp.dot(ikc, iq_t, preferred_element_type=F32)
        acc = jnp.zeros((KC, TQ), F32)
        for h in range(N_IDX_HEADS):
            acc = acc + w[h:h + 1, :] * jnp.maximum(y[:, h * TQ:(h + 1) * TQ], 0.0)
        causal = key_io + c * KC <= qpos
        sc_ref[c] = jnp.where(causal, acc, -inf)
        return (jnp.minimum(lo, fold(jnp.where(causal, acc, inf), jnp.min)),
                jnp.maximum(hi, fold(jnp.where(causal, acc, -inf), jnp.max)))

    lo8, hi8 = lax.fori_loop(0, nch, score_body,
                             (jnp.full((SUBLANES, TQ), inf, F32), jnp.full((SUBLANES, TQ), -inf, F32)))

    def count(pred):
        def body(c, acc):
            return acc + fold(jnp.where(pred(sc_ref[c], c), 1.0, 0.0), jnp.sum)
        acc = lax.fori_loop(0, nch, body, jnp.zeros((SUBLANES, TQ), F32))
        return jnp.sum(acc, axis=0, keepdims=True)

    def select():
        lo = jnp.min(lo8, axis=0, keepdims=True)
        hi = jnp.max(hi8, axis=0, keepdims=True)
        n_causal = (qpos[:1] + 1).astype(F32)
        lo, _, all_exact = _bisect_threshold(lambda thr: count(lambda s, c: s >= thr), lo, hi, n_causal, float(nsel))

        def ties():
            def min_ge(lo):
                def body(c, acc):
                    s = sc_ref[c]
                    return jnp.minimum(acc, fold(jnp.where(s >= lo, s, inf), jnp.min))
                acc = lax.fori_loop(0, nch, body, jnp.full((SUBLANES, TQ), inf, F32))
                return jnp.min(acc, axis=0, keepdims=True)

            def count_gt_and_next(v):
                def body(c, carry):
                    n, nxt = carry
                    s = sc_ref[c]
                    above = s > v
                    return (n + fold(jnp.where(above, 1.0, 0.0), jnp.sum),
                            jnp.minimum(nxt, fold(jnp.where(above, s, inf), jnp.min)))
                n, nxt = lax.fori_loop(0, nch, body, (jnp.zeros((SUBLANES, TQ), F32),
                                                      jnp.full((SUBLANES, TQ), inf, F32)))
                return jnp.sum(n, axis=0, keepdims=True), jnp.min(nxt, axis=0, keepdims=True)

            tau, need = _exact_threshold(min_ge, count_gt_and_next, lo, float(nsel))
            nbits = max(1, int(t_total - 1).bit_length())
            m_star = _select_tie_index(
                lambda cand: count(lambda s, c: (s == tau) & (key_io + c * KC < cand)), need, nbits, (1, TQ))
            return tau, m_star

        return lax.cond(all_exact == 1, lambda: (lo, jnp.full((1, TQ), t_total, I32)), ties)

    tau, m_star = lax.cond(i * TQ + TQ <= nsel,
                           lambda: (jnp.full((1, TQ), -inf, F32), jnp.full((1, TQ), t_total, I32)), select)

    def bias_body(c, _):
        s = sc_ref[c]
        idx = key_io + c * KC
        keep = ((s > tau) | ((s == tau) & (idx <= m_star))) & (idx <= qpos)
        bias_ref[c] = jnp.where(keep, 0.0, NEG)
        return 0

    lax.fori_loop(0, nch, bias_body, 0)

    for j in range(N_KV):
        qt_ref[j] = _queries_t(q_ref, j)
    _softmax_reset(m_ref, l_ref, acc_ref)

    def logits(j, c):
        kc = k_ref[j, 0, pl.ds(pl.multiple_of(c * KC, KC), KC), :]
        bias = jnp.concatenate([bias_ref[c]] * GROUP, axis=1)
        return jnp.dot(kc, qt_ref[j], preferred_element_type=F32) + bias

    def body(c, s_first):
        return _attend_heads(s_first, lambda j: logits(j, c), lambda j: v_ref[j, 0, c], m_ref, l_ref, acc_ref,
                             next_first_fn=lambda: logits(0, jnp.minimum(c + 1, nch - 1)))

    lax.fori_loop(0, nch, body, logits(0, 0))
    _store_heads_t(o_ref, l_ref, acc_ref)


def dsa_attention(iq_t, iw_t, ikb, q, kh, vt):
    b, t, _ = q.shape
    nsel = min(TOPK_MAX, t // 4)
    nch = t // KC
    kern = functools.partial(_dsa_kernel, nsel=nsel, t_total=t)
    return pl.pallas_call(
        kern,
        grid=(b, t // TQ),
        in_specs=[pl.BlockSpec((1, IDX_DIM, N_IDX_HEADS * TQ), lambda bi, i: (bi, i, 0)),
                  pl.BlockSpec((1, 1, N_IDX_HEADS, TQ), lambda bi, i: (bi, i, 0, 0)),
                  pl.BlockSpec((1, t, IDX_DIM), lambda bi, i: (bi, 0, 0)),
                  pl.BlockSpec((1, TQ, N_HEADS * HEAD_DIM), lambda bi, i: (bi, i, 0)),
                  pl.BlockSpec((N_KV, 1, t, HEAD_DIM), lambda bi, i: (0, bi, 0, 0)),
                  pl.BlockSpec((N_KV, 1, nch, HEAD_DIM, KC), lambda bi, i: (0, bi, 0, 0, 0))],
        out_specs=pl.BlockSpec((1, TQ, N_HEADS * HEAD_DIM), lambda bi, i: (bi, i, 0)),
        out_shape=jax.ShapeDtypeStruct((b, t, N_HEADS * HEAD_DIM), BF16),
        scratch_shapes=[pltpu.VMEM((nch, KC, TQ), F32), pltpu.VMEM((nch, KC, TQ), F32)] + _softmax_scratch(HEAD_DIM),
        compiler_params=_params(("parallel", "arbitrary")),
        name="dsa_attention",
    )(iq_t, iw_t, ikb, q, kh, vt)


def _page_copy(pool_ref, layer, pt_ref, b, p, buf, slot, sem):
    return pltpu.make_async_copy(pool_ref.at[layer, pt_ref[b, p]], buf.at[slot, p], sem.at[slot])


def _fetch_sequence_pages(pool_ref, layer, pt_ref, buf, sem):
    b = pl.program_id(0)
    nb = pl.num_programs(0)
    npages = buf.shape[1]
    slot = b % 2

    def start_all(bb, sl):
        lax.fori_loop(0, npages, lambda p, _: (_page_copy(pool_ref, layer, pt_ref, bb, p, buf, sl, sem).start(), 0)[1], 0)

    @pl.when(b == 0)
    def _():
        start_all(0, 0)

    @pl.when(b + 1 < nb)
    def _():
        start_all(b + 1, 1 - slot)

    lax.fori_loop(0, npages, lambda p, _: (_page_copy(pool_ref, layer, pt_ref, b, p, buf, slot, sem).wait(), 0)[1], 0)
    return slot


def _dsa_sample_index_kernel(pt_ref, iq_ref, iw_ref, pool_ref, ikn_ref, bias_ref, biasn_ref, buf, sem, sc_ref,
                             *, layer, nsel):
    slot = _fetch_sequence_pages(pool_ref, layer, pt_ref, buf, sem)
    npages = buf.shape[1]
    iq = iq_ref[0]
    w = iw_ref[0]

    unroll = math.gcd(npages, 8)

    def score_body(pb, _):
        for k in range(unroll):
            p = pb * unroll + k
            y = jnp.dot(iq, buf[slot, p].astype(BF16), preferred_element_type=F32)
            sc_ref[pl.ds(p, 1), :] = jnp.sum(w * jnp.maximum(y, 0.0), axis=0, keepdims=True)
        return 0

    lax.fori_loop(0, npages // unroll, score_body, 0)
    prod = iq.astype(F32) * ikn_ref[0].astype(BF16).astype(F32)
    yn = jnp.sum(prod, axis=1, keepdims=True)
    sn = jnp.sum(w * jnp.maximum(yn, 0.0), axis=0, keepdims=True)
    sc = sc_ref[...]
    idx = lax.broadcasted_iota(I32, sc.shape, 0) * PAGE + lax.broadcasted_iota(I32, sc.shape, 1)
    inf = float("inf")

    def total(x, op=jnp.sum):
        return op(op(x, axis=0, keepdims=True), axis=1, keepdims=True)

    def count_ge(thr):
        return total(jnp.where(sc >= thr, 1.0, 0.0)) + jnp.where(sn >= thr, 1.0, 0.0)

    def min_ge(lo):
        return jnp.minimum(total(jnp.where(sc >= lo, sc, inf), jnp.min), jnp.where(sn >= lo, sn, inf))

    def count_gt_and_next(v):
        n = total(jnp.where(sc > v, 1.0, 0.0)) + jnp.where(sn > v, 1.0, 0.0)
        return n, jnp.minimum(total(jnp.where(sc > v, sc, inf), jnp.min), jnp.where(sn > v, sn, inf))

    lo = jnp.minimum(total(sc, jnp.min), sn)
    hi = jnp.maximum(total(sc, jnp.max), sn)
    lo, _, _ = _bisect_threshold(count_ge, lo, hi, jnp.full((1, 1), float(npages * PAGE + 1), F32), float(nsel))
    tau, need = _exact_threshold(min_ge, count_gt_and_next, lo, float(nsel))
    nbits = int(npages * PAGE).bit_length()
    m_star = _select_tie_index(
        lambda cand: total(jnp.where((sc == tau) & (idx < cand), 1.0, 0.0)), need, nbits, (1, 1))
    n_eq = total(jnp.where(sc == tau, 1.0, 0.0))
    keep = (sc > tau) | ((sc == tau) & (idx <= m_star))
    keep_rows = jnp.dot(jnp.where(keep, 1.0, 0.0).astype(BF16), _row_expansion().astype(BF16),
                        preferred_element_type=F32)
    bias_ref[0] = jnp.where(keep_rows > 0.5, 0.0, NEG)
    keep_n = (sn > tau) | ((sn == tau) & (n_eq < need))
    biasn_ref[0] = jnp.broadcast_to(jnp.where(keep_n, 0.0, NEG), (1, LANES))


def dsa_sample_index(page_table, iq, iw, pool_ik_t, layer, ik_new):
    b, npages = page_table.shape
    nsel = min(TOPK_MAX, (npages * PAGE + 1) // 4)
    per_b = lambda bi, pt: (bi, 0, 0)
    grid_spec = pltpu.PrefetchScalarGridSpec(
        num_scalar_prefetch=1,
        grid=(b,),
        in_specs=[pl.BlockSpec((1, N_IDX_HEADS, IDX_DIM), per_b),
                  pl.BlockSpec((1, N_IDX_HEADS, 1), per_b),
                  pl.BlockSpec(memory_space=pl.ANY),
                  pl.BlockSpec((1, 1, IDX_DIM), per_b)],
        out_specs=[pl.BlockSpec((1, npages, PAGE * N_KV), per_b), pl.BlockSpec((1, 1, LANES), per_b)],
        scratch_shapes=[pltpu.VMEM((2, npages, IDX_DIM, PAGE), F32), pltpu.SemaphoreType.DMA((2,)),
                        pltpu.VMEM((npages, PAGE), F32)])
    return pl.pallas_call(
        functools.partial(_dsa_sample_index_kernel, layer=layer, nsel=nsel),
        grid_spec=grid_spec,
        out_shape=[jax.ShapeDtypeStruct((b, npages, PAGE * N_KV), F32), jax.ShapeDtypeStruct((b, 1, LANES), F32)],
        compiler_params=_params(("arbitrary",)),
        name="dsa_sample_index",
    )(page_table, iq, iw, pool_ik_t, ik_new)


def _fox_sample_bias_kernel(pt_ref, pool_ref, lfn_ref, bias_ref, buf, sem, *, layer):
    slot = _fetch_sequence_pages(pool_ref, layer, pt_ref, buf, sem)
    npages, h = buf.shape[1], buf.shape[2]
    rows = npages * h
    x = buf[slot].reshape(rows, PAGE)
    r = lax.broadcasted_iota(I32, (PAGE, PAGE), 0)
    c = lax.broadcasted_iota(I32, (PAGE, PAGE), 1)
    within = jnp.dot(x, (r > c).astype(F32), precision=HI, preferred_element_type=F32)
    tot = jnp.dot(x, jnp.ones((PAGE, PAGE), F32), precision=HI, preferred_element_type=F32)
    row = lax.broadcasted_iota(I32, (rows, PAGE), 0)
    suffix = tot
    shift = h
    while shift < rows:
        suffix = suffix + jnp.where(row + shift < rows, pltpu.roll(suffix, rows - shift, 0), 0.0)
        shift *= 2
    per_tok = within + (suffix - tot)
    per_row = jnp.dot(per_tok, _row_expansion().astype(F32), precision=HI, preferred_element_type=F32)
    bias_ref[0] = (per_row.reshape(npages, h, PAGE * N_KV) + lfn_ref[0][None]) * LOG2E


def fox_sample_bias(page_table, pool_lf_t, layer, lf_new):
    b, npages = page_table.shape
    h = pool_lf_t.shape[2]
    grid_spec = pltpu.PrefetchScalarGridSpec(
        num_scalar_prefetch=1,
        grid=(b,),
        in_specs=[pl.BlockSpec(memory_space=pl.ANY),
                  pl.BlockSpec((1, h, 1), lambda bi, pt: (bi, 0, 0))],
        out_specs=pl.BlockSpec((1, npages, h, PAGE * N_KV), lambda bi, pt: (bi, 0, 0, 0)),
        scratch_shapes=[pltpu.VMEM((2, npages, h, PAGE), F32), pltpu.SemaphoreType.DMA((2,))])
    return pl.pallas_call(
        functools.partial(_fox_sample_bias_kernel, layer=layer),
        grid_spec=grid_spec,
        out_shape=jax.ShapeDtypeStruct((b, npages, h, PAGE * N_KV), F32),
        compiler_params=_params(("arbitrary",)),
        name="fox_sample_bias",
    )(page_table, pool_lf_t, lf_new)


def _decode_kernel(pt_ref, q_ref, pk_ref, pv_ref, b_ref, kn_ref, vn_ref, bn_ref, o_ref, kbuf, vbuf, sem,
                   *, layer, per_head_bias):
    b = pl.program_id(0)
    npages = b_ref.shape[1]
    pg = kbuf.shape[1]
    ngroups = npages // pg
    rows = PAGE * N_KV
    q = q_ref[0]
    row_kv = lax.broadcasted_iota(I32, (N_HEADS, rows), 1) % N_KV
    head_mask = jnp.where(row_kv == lax.broadcasted_iota(I32, (N_HEADS, rows), 0) // GROUP, 0.0, NEG)

    def copies(g, slot):
        out = []
        for i in range(pg):
            page = pt_ref[b, g * pg + i]
            out.append(pltpu.make_async_copy(pk_ref.at[layer, page], kbuf.at[slot, i], sem.at[0, slot]))
            out.append(pltpu.make_async_copy(pv_ref.at[layer, page], vbuf.at[slot, i], sem.at[1, slot]))
        return out

    for cp in copies(0, 0):
        cp.start()

    def body(g, carry):
        m_run, l_run, acc = carry
        slot = g % 2

        @pl.when(g + 1 < ngroups)
        def _():
            for cp in copies(g + 1, 1 - slot):
                cp.start()

        for cp in copies(g, slot):
            cp.wait()
        logits = []
        for i in range(pg):
            s = lax.dot_general(q, kbuf[slot, i].astype(BF16), NT, preferred_element_type=F32)
            page_bias = b_ref[0, g * pg + i] if per_head_bias else b_ref[0, pl.ds(g * pg + i, 1), :]
            logits.append(s + page_bias + head_mask)
        s_all = jnp.concatenate(logits, axis=1)
        m_new = jnp.maximum(m_run, jnp.max(s_all, axis=1, keepdims=True))
        alpha = jnp.exp2(m_run - m_new)
        p = jnp.exp2(s_all - m_new)
        l_new = alpha * l_run + jnp.sum(p, axis=1, keepdims=True)
        pv = jnp.zeros((N_HEADS, HEAD_DIM), F32)
        for i in range(pg):
            pb = p[:, i * rows:(i + 1) * rows].astype(BF16)
            pv = pv + jnp.dot(pb, vbuf[slot, i].astype(BF16), preferred_element_type=F32)
        return m_new, l_new, alpha * acc + pv

    init = (jnp.full((N_HEADS, 1), NEG, F32), jnp.zeros((N_HEADS, 1), F32), jnp.zeros((N_HEADS, HEAD_DIM), F32))
    m_run, l_run, acc = lax.fori_loop(0, ngroups, body, init)

    kn = kn_ref[0].astype(BF16).astype(F32)
    s_n = jnp.sum(q.astype(F32) * kn, axis=1, keepdims=True) + bn_ref[0]
    m_fin = jnp.maximum(m_run, s_n)
    alpha = jnp.exp2(m_run - m_fin)
    p_n = jnp.exp2(s_n - m_fin)
    l_fin = alpha * l_run + p_n
    vn = vn_ref[0].astype(BF16).astype(F32)
    o_ref[0] = (alpha * acc + p_n.astype(BF16).astype(F32) * vn) / l_fin


def paged_decode_attention(page_table, q, cache_k, cache_v, layer, bias, k_new, v_new, bias_new):
    b, npages = page_table.shape
    rows = PAGE * N_KV
    nl, npool = cache_k.shape[:2]
    per_head_bias = bias.ndim == 4
    bias_spec = (pl.BlockSpec((1, npages, N_HEADS, rows), lambda bi, pt: (bi, 0, 0, 0)) if per_head_bias
                 else pl.BlockSpec((1, npages, rows), lambda bi, pt: (bi, 0, 0)))
    pg = math.gcd(DECODE_PAGES, npages)
    per_b = lambda bi, pt: (bi, 0, 0)
    per_head = lambda x: jnp.repeat(x.reshape(b, N_KV, HEAD_DIM), GROUP, axis=1)
    grid_spec = pltpu.PrefetchScalarGridSpec(
        num_scalar_prefetch=1,
        grid=(b,),
        in_specs=[pl.BlockSpec((1, N_HEADS, HEAD_DIM), per_b),
                  pl.BlockSpec(memory_space=pl.ANY),
                  pl.BlockSpec(memory_space=pl.ANY),
                  bias_spec,
                  pl.BlockSpec((1, N_HEADS, HEAD_DIM), per_b),
                  pl.BlockSpec((1, N_HEADS, HEAD_DIM), per_b),
                  pl.BlockSpec((1, N_HEADS, 1), per_b)],
        out_specs=pl.BlockSpec((1, N_HEADS, HEAD_DIM), per_b),
        scratch_shapes=[pltpu.VMEM((2, pg, rows, HEAD_DIM), F32),
                        pltpu.VMEM((2, pg, rows, HEAD_DIM), F32),
                        pltpu.SemaphoreType.DMA((2, 2))])
    return pl.pallas_call(
        functools.partial(_decode_kernel, layer=layer, per_head_bias=per_head_bias),
        grid_spec=grid_spec,
        out_shape=jax.ShapeDtypeStruct((b, N_HEADS, HEAD_DIM), F32),
        compiler_params=_params(("arbitrary",)),
        name="paged_decode_attention",
    )(page_table, q, cache_k.reshape(nl, npool, rows, HEAD_DIM), cache_v.reshape(nl, npool, rows, HEAD_DIM),
      bias, per_head(k_new), per_head(v_new), bias_new)


def _row_expansion():
    t = lax.broadcasted_iota(I32, (PAGE, PAGE * N_KV), 0)
    r = lax.broadcasted_iota(I32, (PAGE, PAGE * N_KV), 1)
    return r // N_KV == t


def _cmul(ar, ai, br, bi):
    return ar * br - ai * bi, ar * bi + ai * br


def _ssm_tables(a_re, a_im, log_dt, b_re, b_im, c_re, c_im, d_skip):
    L, cdim = SSM_CHUNK, SSM_GROUP
    g = a_re.shape[0]
    dt = jnp.exp(log_dt)[:, None]
    mag = jnp.exp(a_re * dt)
    ab = (mag * jnp.cos(a_im * dt), mag * jnp.sin(a_im * dt))
    den = a_re * a_re + a_im * a_im
    num = _cmul(ab[0] - 1.0, ab[1], a_re, -a_im)
    sc = (num[0] / den, num[1] / den)
    bs = _cmul(b_re, b_im, sc[0][:, :, None], sc[1][:, :, None])
    pows = [(jnp.ones_like(ab[0]), jnp.zeros_like(ab[0]))]
    for _ in range(L):
        pows.append(_cmul(*pows[-1], *ab))
    pw = (jnp.stack([p[0] for p in pows]), jnp.stack([p[1] for p in pows]))
    ca = _cmul(c_re[None], c_im[None], pw[0][:L, :, None, :], pw[1][:L, :, None, :])
    def contract_n(x, y):
        return jnp.sum(x.transpose(0, 1, 3, 2)[..., None] * y[None, :, :, None, :], axis=2)

    kd = contract_n(ca[0], bs[0]) - contract_n(ca[1], bs[1])
    dmat = jnp.eye(cdim, dtype=F32)[None] * d_skip.reshape(g, cdim)[:, :, None]
    kd = kd.at[0].add(dmat)
    ii = jnp.arange(L)
    diff = ii[None, :] - ii[:, None]
    kt = jnp.where((diff >= 0)[:, :, None, None, None], kd[jnp.clip(diff, 0, L - 1)], 0.0)
    m = kt.transpose(2, 0, 4, 1, 3).reshape(g, L * cdim, L * cdim)
    bst = (bs[0].transpose(0, 2, 1)[None], bs[1].transpose(0, 2, 1)[None])
    pj = _cmul(pw[0][L - 1 - ii][:, :, None, :], pw[1][L - 1 - ii][:, :, None, :], *bst)
    p = jnp.concatenate([x.transpose(1, 0, 2, 3).reshape(g, L * cdim, -1) for x in pj], axis=-1)
    cp = _cmul(c_re[None], c_im[None], pw[0][1:L + 1][:, :, None, :], pw[1][1:L + 1][:, :, None, :])
    cp = [x.transpose(1, 3, 0, 2).reshape(g, -1, L * cdim) for x in cp]
    q = jnp.concatenate([cp[0], -cp[1]], axis=1)
    m = _ssm_slot_order(_ssm_slot_order(m, 1), 2)
    return m.astype(BF16), _ssm_slot_order(p, 1), _ssm_slot_order(q, 2).astype(BF16), pw[0][L], pw[1][L], ab, bs


def _ssm_slot_order(x, axis):
    gpb = LANES // SSM_GROUP
    g = x.shape[0]
    split = x.shape[:axis] + (SSM_CHUNK // gpb, gpb, SSM_GROUP) + x.shape[axis + 1:]
    xs = x.reshape((g // gpb, gpb) + split[1:])
    rolled = [jnp.roll(xs[:, gg], gg, axis=axis + 1) for gg in range(gpb)]
    return jnp.stack(rolled, axis=1).reshape(x.shape)


def _ssm_kernel(u_ref, m_ref, p_ref, q_ref, ar_ref, ai_ref, y_ref, hf_ref, ug_ref, yg_ref, *, nb):
    L, cdim, n = SSM_CHUNK, SSM_GROUP, SSM_STATE
    gpb = LANES // cdim
    r = ug_ref.shape[1]
    nk = r // nb
    for j in range(L):
        xj = u_ref[pl.ds(j, r, stride=L), :]
        if j % gpb:
            xj = pltpu.roll(xj, (j % gpb) * cdim, 1)
        for gg in range(gpb):
            b = (gg + j) % gpb
            lanes = slice((j // gpb) * LANES + b * cdim, (j // gpb) * LANES + (b + 1) * cdim)
            ug_ref[gg, :, lanes] = xj[:, b * cdim:(b + 1) * cdim]
    row = lax.broadcasted_iota(I32, (r, 2 * n), 0) % nk
    lane = lax.broadcasted_iota(I32, (1, 2 * n), 1)

    pair = 2

    def group_body(gi, _):
        ggs = [gi * pair + s for s in range(pair)]
        us = [ug_ref[gg] for gg in ggs]
        hs = [jnp.dot(u, p_ref[gg], precision=HI, preferred_element_type=F32) for u, gg in zip(us, ggs)]
        drs = [jnp.concatenate([ar_ref[gg], ar_ref[gg]], axis=1) for gg in ggs]
        dis = [jnp.concatenate([-ai_ref[gg], ai_ref[gg]], axis=1) for gg in ggs]
        shift = 1
        while shift < nk:
            for s in range(pair):
                prev = jnp.where(row >= shift, pltpu.roll(hs[s], shift, 0), 0.0)
                hs[s] = hs[s] + drs[s] * prev + dis[s] * pltpu.roll(prev, n, 1)
                re, im = drs[s], jnp.where(lane < n, -dis[s], dis[s])
                re2, im2 = re * re - im * im, 2.0 * re * im
                drs[s], dis[s] = re2, jnp.where(lane < n, -im2, im2)
            shift *= 2
        for s, gg in enumerate(ggs):
            hprev = jnp.where(row >= 1, pltpu.roll(hs[s], 1, 0), 0.0)
            yg_ref[gg] = (jnp.dot(us[s].astype(BF16), m_ref[gg], preferred_element_type=F32)
                          + jnp.dot(hprev.astype(BF16), q_ref[gg], preferred_element_type=F32))
            for b in range(nb):
                hf_ref[gg, b:b + 1, :] = hs[s][(b + 1) * nk - 1:(b + 1) * nk, :]
        return 0

    lax.fori_loop(0, gpb // pair, group_body, 0)
    slot_io = lax.broadcasted_iota(I32, (r, LANES), 1) // cdim
    for j in range(L):
        half = slice((j // gpb) * LANES, (j // gpb + 1) * LANES)
        z = yg_ref[0, :, half]
        for gg in range(1, gpb):
            z = jnp.where(slot_io == (gg + j) % gpb, yg_ref[gg, :, half], z)
        if j % gpb:
            z = pltpu.roll(z, LANES - (j % gpb) * cdim, 1)
        y_ref[pl.ds(j, r, stride=L), :] = z


def ssm_prompt(u, m, p, q, al_re, al_im, nb):
    rows, d = u.shape
    g, w, n2 = p.shape
    gpb = LANES // SSM_GROUP
    r = rows // SSM_CHUNK
    gmap = lambda i: (i, 0, 0)
    return pl.pallas_call(
        functools.partial(_ssm_kernel, nb=nb),
        grid=(d // LANES,),
        in_specs=[pl.BlockSpec((rows, LANES), lambda i: (0, i)),
                  pl.BlockSpec((gpb, w, w), gmap), pl.BlockSpec((gpb, w, n2), gmap), pl.BlockSpec((gpb, n2, w), gmap),
                  pl.BlockSpec((gpb, 1, n2 // 2), gmap), pl.BlockSpec((gpb, 1, n2 // 2), gmap)],
        out_specs=[pl.BlockSpec((rows, LANES), lambda i: (0, i)), pl.BlockSpec((gpb, nb, n2), gmap)],
        out_shape=[jax.ShapeDtypeStruct((rows, d), F32), jax.ShapeDtypeStruct((g, nb, n2), F32)],
        scratch_shapes=[pltpu.VMEM((gpb, r, w), F32), pltpu.VMEM((gpb, r, w), F32)],
        compiler_params=_params(("parallel",)),
        name="ssm_prompt",
    )(u, m, p, q, al_re.reshape(g, 1, -1), al_im.reshape(g, 1, -1))


def _ssm_step_kernel(u_ref, hr_ref, hi_ref, ar_ref, ai_ref, br_ref, bi_ref, cr_ref, ci_ref, d_ref,
                     y_ref, or_ref, oi_ref):
    u = u_ref[0]
    bur = jnp.zeros(hr_ref.shape[1:], F32)
    bui = jnp.zeros(hr_ref.shape[1:], F32)
    for c in range(SSM_GROUP):
        uc = u[:, c:c + 1]
        bur = bur + br_ref[c] * uc
        bui = bui + bi_ref[c] * uc
    ar, ai = ar_ref[...], ai_ref[...]
    h0r, h0i = hr_ref[0], hi_ref[0]
    hr = ar * h0r - ai * h0i + bur
    hi = ar * h0i + ai * h0r + bui
    or_ref[0] = hr
    oi_ref[0] = hi
    lane = lax.broadcasted_iota(I32, u.shape, 1)
    y = d_ref[...] * u
    for c in range(SSM_GROUP):
        col = jnp.sum(cr_ref[c] * hr - ci_ref[c] * hi, axis=1, keepdims=True)
        y = y + jnp.where(lane == c, col, 0.0)
    y_ref[0] = y


def ssm_step(u, h_re, h_im, a_bar, bs, c_re, c_im, d_skip):
    b, g, c = u.shape
    n = h_re.shape[2]
    per_b = lambda i: (i, 0, 0)
    c3 = lambda i: (0, 0, 0)
    c2 = lambda i: (0, 0)
    return pl.pallas_call(
        _ssm_step_kernel,
        grid=(b,),
        in_specs=[pl.BlockSpec((1, g, c), per_b), pl.BlockSpec((1, g, n), per_b), pl.BlockSpec((1, g, n), per_b),
                  pl.BlockSpec((g, n), c2), pl.BlockSpec((g, n), c2),
                  pl.BlockSpec((c, g, n), c3), pl.BlockSpec((c, g, n), c3),
                  pl.BlockSpec((c, g, n), c3), pl.BlockSpec((c, g, n), c3),
                  pl.BlockSpec((g, c), c2)],
        out_specs=[pl.BlockSpec((1, g, c), per_b), pl.BlockSpec((1, g, n), per_b), pl.BlockSpec((1, g, n), per_b)],
        out_shape=[jax.ShapeDtypeStruct((b, g, c), F32), jax.ShapeDtypeStruct((b, g, n), F32),
                   jax.ShapeDtypeStruct((b, g, n), F32)],
        compiler_params=_params(("parallel",)),
        name="ssm_step",
    )(u, h_re, h_im, a_bar[0], a_bar[1], bs[0].transpose(2, 0, 1), bs[1].transpose(2, 0, 1),
      c_re.transpose(1, 0, 2), c_im.transpose(1, 0, 2), d_skip.reshape(g, c))


def _dsa_layer(xp, xs, gain, cache_k, cache_v, cache_ik, layer, page_table, w_in, w_out, q_gain, k_gain, ik_gain,
               bp, t, past_len, tm):
    mp, ms = xp.shape[0], xs.shape[0]
    nkv = N_KV * HEAD_DIM
    nq = t // TQ
    proj = norm_proj(xp, gain, w_in, layer, tm)
    q, k, v, ik, iw, kh, vt, ikb, iq_t = dsa_post(proj, jnp.arange(t), q_gain, k_gain, ik_gain, KC, True)
    iw_t = iw.reshape(bp, nq, TQ, N_IDX_HEADS).transpose(0, 1, 3, 2)
    o = dsa_attention(iq_t.reshape(bp, nq * IDX_DIM, N_IDX_HEADS * TQ), iw_t, ikb.reshape(bp, t, IDX_DIM),
                      q.reshape(bp, t, -1), kh.reshape(N_KV, bp, t, HEAD_DIM),
                      vt.reshape(N_KV, bp, t // KC, HEAD_DIM, KC))
    xp = out_res(xp, o.reshape(mp, -1), w_out, layer, tm)
    projs = norm_proj(xs, gain, w_in, layer, ms)
    pos_s = jnp.full((ms,), past_len, I32)
    qs, ks, vs, iks, iws, iqs = dsa_post(projs, pos_s, q_gain, k_gain, ik_gain, ms, False)
    bias, bias_n = dsa_sample_index(page_table, iqs.reshape(ms, N_IDX_HEADS, IDX_DIM),
                                    iws.reshape(ms, N_IDX_HEADS, 1), cache_ik.transpose(0, 1, 3, 2), layer,
                                    iks.reshape(ms, 1, IDX_DIM))
    bias_nh = jnp.broadcast_to(bias_n[:, :, :1], (ms, N_HEADS, 1))
    os_ = paged_decode_attention(page_table, qs.reshape(ms, N_HEADS, HEAD_DIM), cache_k, cache_v, layer, bias,
                                 ks, vs, bias_nh)
    xs = out_res(xs, os_.reshape(ms, -1), w_out, layer, ms)
    return xp, xs, (k, v, ik), (ks, vs, iks)


def _key_bias_columns(parts, bp, t):
    cols = jnp.stack(parts, axis=-1).reshape(bp, t, N_KV, GROUP * 3).transpose(0, 2, 1, 3)
    return jnp.pad(cols, ((0, 0), (0, 0), (0, 0), (0, HEAD_DIM - GROUP * 3)))


def _fox_layer(xp, xs, gain, cache_k, cache_v, cache_lf, layer, page_table, w_in, b_f, w_out, q_gain, k_gain,
               bp, t, tm):
    mp, ms = xp.shape[0], xs.shape[0]
    nkv = N_KV * HEAD_DIM
    proj = norm_proj(xp, gain, w_in, layer, tm)
    q, k, v, lf, kh, vt = fox_post(proj, b_f, q_gain, k_gain, KC, True)
    c_parts = cumsum_rows_split(lf.reshape(bp, t, N_HEADS))
    o = fox_attention(q.reshape(bp, t, -1), kh.reshape(N_KV, bp, t, HEAD_DIM), _key_bias_columns(c_parts, bp, t),
                      vt.reshape(N_KV, bp, t // KC, HEAD_DIM, KC))
    xp = out_res(xp, o.reshape(mp, -1), w_out, layer, tm)
    projs = norm_proj(xs, gain, w_in, layer, ms)
    qs, ks, vs, lfs = fox_post(projs, b_f, q_gain, k_gain, ms, False)
    bias = fox_sample_bias(page_table, cache_lf.transpose(0, 1, 3, 2), layer, lfs.reshape(ms, N_HEADS, 1))
    os_ = paged_decode_attention(page_table, qs.reshape(ms, N_HEADS, HEAD_DIM), cache_k, cache_v, layer, bias,
                                 ks, vs, jnp.zeros((ms, N_HEADS, 1), F32))
    xs = out_res(xs, os_.reshape(ms, -1), w_out, layer, ms)
    return xp, xs, (k, v, lf), (ks, vs, lfs)


def _ssm_layer(xp, xs, gain, h0_re, h0_im, a_re, a_im, log_dt, b_re, b_im, c_re, c_im, d_skip, w_glu, layer,
               bp, t, tm):
    mp, ms = xp.shape[0], xs.shape[0]
    g, n, cdim, L = N_SSM_GROUPS, SSM_STATE, SSM_GROUP, SSM_CHUNK
    m, p, q, al_re, al_im, a_bar, bs = _ssm_tables(a_re, a_im, log_dt, b_re, b_im, c_re, c_im, d_skip)
    u = rmsnorm(xp, gain, tm)
    y, hf = ssm_prompt(u, m, p, q, al_re, al_im, bp)
    xp = glu_res(xp, y, w_glu, layer, tm)
    hf = hf.transpose(1, 0, 2)
    us = rmsnorm(xs, gain, ms)
    ys, hs_re, hs_im = ssm_step(us.reshape(ms, g, cdim), h0_re, h0_im, a_bar, bs, c_re, c_im, d_skip)
    xs = glu_res(xs, ys.reshape(ms, g * cdim), w_glu, layer, ms)
    return xp, xs, (hf[..., :n], hf[..., n:]), (hs_re, hs_im)


def kernel(x_prompt, x_sample, cache_dsa_k, cache_dsa_v, cache_dsa_idx_k, cache_fox_k, cache_fox_v, cache_fox_logf, state_ssm_re, state_ssm_im, page_table, norm_mix, norm_ffn, dsa_w_in, dsa_w_out, dsa_q_gain, dsa_k_gain, dsa_ik_gain, fox_w_in, fox_b_f, fox_w_out, fox_q_gain, fox_k_gain, ssm_a_re, ssm_a_im, ssm_log_dt, ssm_b_re, ssm_b_im, ssm_c_re, ssm_c_im, ssm_d, ssm_w_glu, ffn_w_gu, ffn_w_down):
    bp, t, d = x_prompt.shape
    bs_, ts, _ = x_sample.shape
    assert ts == 1, "the sample group carries one new position per sequence"
    assert t % KC == 0 and KC % TQ == 0
    depth = norm_mix.shape[0]
    past_len = page_table.shape[1] * PAGE
    tm = min(512, bp * t)
    xp = x_prompt.reshape(bp * t, d)
    xs = x_sample.reshape(bs_ * ts, d)
    dsa_w_in, dsa_w_out, fox_w_in, fox_w_out, ssm_w_glu, ffn_w_gu, ffn_w_down = (
        w.astype(BF16) for w in (dsa_w_in, dsa_w_out, fox_w_in, fox_w_out, ssm_w_glu, ffn_w_gu, ffn_w_down))
    dsa_p, dsa_s, fox_p, fox_s, ssm_p, ssm_s = [], [], [], [], [], []
    for i in range(depth):
        kind, j = i % N_MIXERS, i // N_MIXERS
        if kind == 0:
            xp, xs, outp, outs = _dsa_layer(
                xp, xs, norm_mix[i], cache_dsa_k, cache_dsa_v, cache_dsa_idx_k, j, page_table,
                dsa_w_in, dsa_w_out, dsa_q_gain[j], dsa_k_gain[j], dsa_ik_gain[j], bp, t, past_len, tm)
            dsa_p.append(outp); dsa_s.append(outs)
        elif kind == 1:
            xp, xs, outp, outs = _fox_layer(
                xp, xs, norm_mix[i], cache_fox_k, cache_fox_v, cache_fox_logf, j, page_table,
                fox_w_in, fox_b_f[j], fox_w_out, fox_q_gain[j], fox_k_gain[j], bp, t, tm)
            fox_p.append(outp); fox_s.append(outs)
        else:
            xp, xs, outp, outs = _ssm_layer(
                xp, xs, norm_mix[i], state_ssm_re[j], state_ssm_im[j], ssm_a_re[j], ssm_a_im[j], ssm_log_dt[j],
                ssm_b_re[j], ssm_b_im[j], ssm_c_re[j], ssm_c_im[j], ssm_d[j], ssm_w_glu, j, bp, t, tm)
            ssm_p.append(outp); ssm_s.append(outs)
        xp = ffn_res(xp, norm_ffn[i], ffn_w_gu, ffn_w_down, i, tm)
        xs = ffn_res(xs, norm_ffn[i], ffn_w_gu, ffn_w_down, i, bs_ * ts)

    def stack(items, idx, shape):
        return jnp.stack([it[idx].reshape(shape) for it in items])

    kv_p, kv_s = (bp, t, N_KV, HEAD_DIM), (bs_, ts, N_KV, HEAD_DIM)
    st_p, st_s = (bp, N_SSM_GROUPS, SSM_STATE), (bs_, N_SSM_GROUPS, SSM_STATE)
    return (xp.reshape(bp, t, d), xs.reshape(bs_, ts, d),
            stack(dsa_p, 0, kv_p), stack(dsa_p, 1, kv_p), stack(dsa_p, 2, (bp, t, IDX_DIM)),
            stack(dsa_s, 0, kv_s), stack(dsa_s, 1, kv_s), stack(dsa_s, 2, (bs_, ts, IDX_DIM)),
            stack(fox_p, 0, kv_p), stack(fox_p, 1, kv_p), stack(fox_p, 2, (bp, t, N_HEADS)),
            stack(fox_s, 0, kv_s), stack(fox_s, 1, kv_s), stack(fox_s, 2, (bs_, ts, N_HEADS)),
            stack(ssm_p, 0, st_p), stack(ssm_p, 1, st_p), stack(ssm_s, 0, st_s), stack(ssm_s, 1, st_s))
```

```python
import functools
import math

import jax
import jax.numpy as jnp
from jax import lax
from jax.experimental import pallas as pl
from jax.experimental.pallas import tpu as pltpu

F32 = jnp.float32
BF16 = jnp.bfloat16
I32 = jnp.int32

D_MODEL = 2048
N_HEADS = 16
HEAD_DIM = 128
N_KV = 4
GROUP = N_HEADS // N_KV
N_IDX_HEADS = 16
IDX_DIM = 64
TOPK_MAX = 256
ROPE_THETA = 500000.0
PAGE = 128
SSM_GROUP = 16
N_SSM_GROUPS = D_MODEL // SSM_GROUP
SSM_STATE = 64
SSM_CHUNK = 16
EPS = 1e-6
N_MIXERS = 3

LANES = 128
SUBLANES = 8
TQ = 128
KC = 256
DECODE_PAGES = 8
LOG2E = math.log2(math.e)
NEG = -1e30
BISECT_MAX = 64
HI = lax.Precision.HIGHEST
VMEM_LIMIT = 56 * 1024 * 1024

NT = (((1,), (1,)), ((), ()))


def _params(sem):
    return pltpu.CompilerParams(dimension_semantics=sem, vmem_limit_bytes=VMEM_LIMIT)


def _rms(x, gain):
    return x * lax.rsqrt(jnp.mean(x * x, axis=-1, keepdims=True) + EPS) * gain


def _proj_kernel(x_ref, g_ref, w_ref, o_ref, h_ref):
    @pl.when(pl.program_id(1) == 0)
    def _():
        h_ref[...] = _rms(x_ref[...], g_ref[...]).astype(BF16)

    o_ref[...] = jnp.dot(h_ref[...], w_ref[...], preferred_element_type=F32)


def _pick_tn(n, cap=1536):
    k = -(-n // cap)
    return -(-n // (k * LANES)) * LANES


def norm_proj(x, gain, w, layer, tm):
    m, d = x.shape
    n = w.shape[2]
    tn = _pick_tn(n)
    return pl.pallas_call(
        _proj_kernel,
        grid=(m // tm, pl.cdiv(n, tn)),
        in_specs=[pl.BlockSpec((tm, d), lambda i, j: (i, 0)),
                  pl.BlockSpec((1, d), lambda i, j: (0, 0)),
                  pl.BlockSpec((None, d, tn), lambda i, j: (layer, 0, j))],
        out_specs=pl.BlockSpec((tm, tn), lambda i, j: (i, j)),
        out_shape=jax.ShapeDtypeStruct((m, n), F32),
        scratch_shapes=[pltpu.VMEM((tm, d), BF16)],
        compiler_params=_params(("parallel", "arbitrary")),
        name="norm_proj",
    )(x, gain.reshape(1, d), w)


def _out_res_kernel(x_ref, o_ref, w_ref, y_ref):
    y_ref[...] = x_ref[...] + jnp.dot(o_ref[...].astype(BF16), w_ref[...], preferred_element_type=F32)


def out_res(x, o, w, layer, tm):
    m, d = x.shape
    k = w.shape[1]
    return pl.pallas_call(
        _out_res_kernel,
        grid=(m // tm,),
        in_specs=[pl.BlockSpec((tm, d), lambda i: (i, 0)),
                  pl.BlockSpec((tm, k), lambda i: (i, 0)),
                  pl.BlockSpec((None, k, d), lambda i: (layer, 0, 0))],
        out_specs=pl.BlockSpec((tm, d), lambda i: (i, 0)),
        out_shape=jax.ShapeDtypeStruct((m, d), F32),
        compiler_params=_params(("parallel",)),
        name="out_res",
    )(x, o, w)


def _ffn_kernel(x_ref, g_ref, wg_ref, wu_ref, wd_ref, y_ref, h_ref, acc_ref):
    f = pl.program_id(1)

    @pl.when(f == 0)
    def _():
        h_ref[...] = _rms(x_ref[...], g_ref[...]).astype(BF16)
        acc_ref[...] = jnp.zeros_like(acc_ref)

    h = h_ref[...]
    g = jnp.dot(h, wg_ref[...], preferred_element_type=F32)
    u = jnp.dot(h, wu_ref[...], preferred_element_type=F32)
    a = (g * jax.nn.sigmoid(g) * u).astype(BF16)
    acc_ref[...] += jnp.dot(a, wd_ref[...], preferred_element_type=F32)

    @pl.when(f == pl.num_programs(1) - 1)
    def _():
        y_ref[...] = x_ref[...] + acc_ref[...]


def ffn_res(x, gain, w_gu, w_down, layer, tm, tf=512):
    m, d = x.shape
    dff = w_down.shape[1]
    nf = dff // tf
    return pl.pallas_call(
        _ffn_kernel,
        grid=(m // tm, nf),
        in_specs=[pl.BlockSpec((tm, d), lambda i, f: (i, 0)),
                  pl.BlockSpec((1, d), lambda i, f: (0, 0)),
                  pl.BlockSpec((None, d, tf), lambda i, f: (layer, 0, f)),
                  pl.BlockSpec((None, d, tf), lambda i, f: (layer, 0, f + nf)),
                  pl.BlockSpec((None, tf, d), lambda i, f: (layer, f, 0))],
        out_specs=pl.BlockSpec((tm, d), lambda i, f: (i, 0)),
        out_shape=jax.ShapeDtypeStruct((m, d), F32),
        scratch_shapes=[pltpu.VMEM((tm, d), BF16), pltpu.VMEM((tm, d), F32)],
        compiler_params=_params(("parallel", "arbitrary")),
        name="ffn_res",
    )(x, gain.reshape(1, d), w_gu, w_gu, w_down)


def _gelu_tanh(y):
    return 0.5 * y * (1.0 + jnp.tanh(math.sqrt(2.0 / math.pi) * (y + 0.044715 * (y * y * y))))


def _glu_kernel(x_ref, y_ref, wa_ref, wb_ref, o_ref, g_ref):
    @pl.when(pl.program_id(1) == 0)
    def _():
        g_ref[...] = _gelu_tanh(y_ref[...]).astype(BF16)

    g = g_ref[...]
    a = jnp.dot(g, wa_ref[...], preferred_element_type=F32)
    b = jnp.dot(g, wb_ref[...], preferred_element_type=F32)
    o_ref[...] = x_ref[...] + a * jax.nn.sigmoid(b)


def glu_res(x, y, w_glu, layer, tm, tn=512):
    m, d = x.shape
    nj = d // tn
    return pl.pallas_call(
        _glu_kernel,
        grid=(m // tm, nj),
        in_specs=[pl.BlockSpec((tm, tn), lambda i, j: (i, j)),
                  pl.BlockSpec((tm, d), lambda i, j: (i, 0)),
                  pl.BlockSpec((None, d, tn), lambda i, j: (layer, 0, j)),
                  pl.BlockSpec((None, d, tn), lambda i, j: (layer, 0, j + nj))],
        out_specs=pl.BlockSpec((tm, tn), lambda i, j: (i, j)),
        out_shape=jax.ShapeDtypeStruct((m, d), F32),
        scratch_shapes=[pltpu.VMEM((tm, d), BF16)],
        compiler_params=_params(("parallel", "arbitrary")),
        name="glu_res",
    )(x, y, w_glu, w_glu)


def _rmsnorm_kernel(x_ref, g_ref, o_ref):
    o_ref[...] = _rms(x_ref[...], g_ref[...])


def rmsnorm(x, gain, tm):
    m, d = x.shape
    return pl.pallas_call(
        _rmsnorm_kernel,
        grid=(m // tm,),
        in_specs=[pl.BlockSpec((tm, d), lambda i: (i, 0)), pl.BlockSpec((1, d), lambda i: (0, 0))],
        out_specs=pl.BlockSpec((tm, d), lambda i: (i, 0)),
        out_shape=jax.ShapeDtypeStruct((m, d), F32),
        compiler_params=_params(("parallel",)),
        name="rmsnorm",
    )(x, gain.reshape(1, d))


def _rope_tables(pos, width, period):
    rot = period // 4
    half = rot // 2
    inv = ROPE_THETA ** (-jnp.arange(half, dtype=F32) / half)
    ang = pos.astype(F32)[:, None] * inv[None, :]
    cos, sin = jnp.cos(ang), jnp.sin(ang)
    n = pos.shape[0]
    ones = jnp.ones((n, period - rot), F32)
    zeros = jnp.zeros((n, period - rot), F32)
    zh = jnp.zeros((n, half), F32)
    c = jnp.concatenate([cos, cos, ones], axis=1)
    s_up = jnp.concatenate([zh, sin, zeros], axis=1)
    s_dn = jnp.concatenate([-sin, zh, zeros], axis=1)
    reps = width // period
    return tuple(jnp.tile(a, (1, reps)) for a in (c, s_up, s_dn))


def _rope(x, cos, s_up, s_dn, half):
    w = x.shape[-1]
    return x * cos + pltpu.roll(x, half, 1) * s_up + pltpu.roll(x, w - half, 1) * s_dn


def _store_kv_heads(p_ref, k0, v0, k_fn, k_ref, v_ref, kh_ref, vt_ref):
    for h in range(N_KV):
        y = k_fn(p_ref[:, k0 + h * HEAD_DIM:k0 + (h + 1) * HEAD_DIM])
        k_ref[:, h * HEAD_DIM:(h + 1) * HEAD_DIM] = y
        if kh_ref is not None:
            kh_ref[h] = y.astype(BF16)
            vh = p_ref[:, v0 + h * HEAD_DIM:v0 + (h + 1) * HEAD_DIM]
            vt_ref[h, 0] = vh.T.astype(BF16)
    v_ref[...] = p_ref[:, v0:v0 + N_KV * HEAD_DIM]


def _dsa_post_kernel(p_ref, qg_ref, kg_ref, ikg_ref, c1_ref, u1_ref, d1_ref, c2_ref, u2_ref, d2_ref,
                     q_ref, k_ref, v_ref, ik_ref, iw_ref, *mode_refs, prompt):
    c1, u1, d1 = c1_ref[...], u1_ref[...], d1_ref[...]
    c2, u2, d2 = c2_ref[...], u2_ref[...], d2_ref[...]
    qg, kg = qg_ref[...], kg_ref[...]
    scale = HEAD_DIM ** -0.5 * LOG2E
    for h in range(N_HEADS):
        x = p_ref[:, h * HEAD_DIM:(h + 1) * HEAD_DIM]
        y = _rope(_rms(x, qg), c1, u1, d1, HEAD_DIM // 8)
        q_ref[:, h * HEAD_DIM:(h + 1) * HEAD_DIM] = (y * scale).astype(BF16)
    k0 = N_HEADS * HEAD_DIM
    v0 = k0 + N_KV * HEAD_DIM
    kh_ref, vt_ref, ikb_ref, iqt_ref = mode_refs if prompt else (None, None, None, None)
    iq_ref = None if prompt else mode_refs[0]
    _store_kv_heads(p_ref, k0, v0, lambda x: _rope(_rms(x, kg), c1, u1, d1, HEAD_DIM // 8),
                    k_ref, v_ref, kh_ref, vt_ref)
    i0 = v0 + N_KV * HEAD_DIM
    tm = p_ref.shape[0]
    for h in range(N_IDX_HEADS * IDX_DIM // LANES):
        x = _rope(p_ref[:, i0 + h * LANES:i0 + (h + 1) * LANES], c2, u2, d2, IDX_DIM // 8)
        if prompt:
            for qb in range(tm // TQ):
                xt = x[qb * TQ:(qb + 1) * TQ].T.astype(BF16)
                iqt_ref[qb, :, (2 * h) * TQ:(2 * h + 1) * TQ] = xt[:IDX_DIM]
                iqt_ref[qb, :, (2 * h + 1) * TQ:(2 * h + 2) * TQ] = xt[IDX_DIM:]
        else:
            iq_ref[:, h * LANES:(h + 1) * LANES] = x.astype(BF16)
    j0 = i0 + N_IDX_HEADS * IDX_DIM
    xk = _rms(p_ref[:, j0:j0 + IDX_DIM], ikg_ref[...])
    xk2 = jnp.concatenate([xk, xk], axis=1)
    ik = _rope(xk2, c2, u2, d2, IDX_DIM // 8)[:, :IDX_DIM]
    ik_ref[...] = ik
    if prompt:
        ikb_ref[...] = ik.astype(BF16)
    iw_ref[...] = p_ref[:, j0 + IDX_DIM:j0 + IDX_DIM + N_IDX_HEADS] * (N_IDX_HEADS ** -0.5 * IDX_DIM ** -0.5)


def dsa_post(proj, pos, q_gain, k_gain, ik_gain, tm, prompt):
    m, n = proj.shape
    npos = pos.shape[0]
    t1 = _rope_tables(pos, HEAD_DIM, HEAD_DIM)
    t2 = _rope_tables(pos, LANES, IDX_DIM)
    nb = npos // tm
    row = lambda i: (i, 0)
    tab = lambda i: (i % nb, 0)
    const = lambda i: (0, 0)
    nkv = N_KV * HEAD_DIM
    niq = N_IDX_HEADS * IDX_DIM
    outs = [((m, N_HEADS * HEAD_DIM), BF16, pl.BlockSpec((tm, N_HEADS * HEAD_DIM), row)),
            ((m, nkv), F32, pl.BlockSpec((tm, nkv), row)),
            ((m, nkv), F32, pl.BlockSpec((tm, nkv), row)),
            ((m, IDX_DIM), F32, pl.BlockSpec((tm, IDX_DIM), row)),
            ((m, N_IDX_HEADS), F32, pl.BlockSpec((tm, N_IDX_HEADS), row))]
    if not prompt:
        outs += [((m, niq), BF16, pl.BlockSpec((tm, niq), row))]
    if prompt:
        assert tm == KC and tm % TQ == 0
        outs += [((N_KV, m, HEAD_DIM), BF16, pl.BlockSpec((N_KV, tm, HEAD_DIM), lambda i: (0, i, 0))),
                 ((N_KV, m // KC, HEAD_DIM, KC), BF16, pl.BlockSpec((N_KV, 1, HEAD_DIM, KC), lambda i: (0, i, 0, 0))),
                 ((m, IDX_DIM), BF16, pl.BlockSpec((tm, IDX_DIM), row)),
                 ((m // TQ, IDX_DIM, N_IDX_HEADS * TQ), BF16,
                  pl.BlockSpec((tm // TQ, IDX_DIM, N_IDX_HEADS * TQ), lambda i: (i, 0, 0)))]
    return pl.pallas_call(
        functools.partial(_dsa_post_kernel, prompt=prompt),
        grid=(m // tm,),
        in_specs=[pl.BlockSpec((tm, n), row),
                  pl.BlockSpec((1, HEAD_DIM), const), pl.BlockSpec((1, HEAD_DIM), const),
                  pl.BlockSpec((1, IDX_DIM), const)]
                 + [pl.BlockSpec((tm, LANES), tab)] * 6,
        out_specs=[o[2] for o in outs],
        out_shape=[jax.ShapeDtypeStruct(o[0], o[1]) for o in outs],
        compiler_params=_params(("parallel",)),
        name="dsa_post",
    )(proj, q_gain.reshape(1, -1), k_gain.reshape(1, -1), ik_gain.reshape(1, -1), *t1, *t2)


def _log_sigmoid(x):
    return jnp.minimum(x, 0.0) - jnp.log1p(jnp.exp(-jnp.abs(x)))


def _fox_post_kernel(p_ref, qg_ref, kg_ref, bf_ref, q_ref, k_ref, v_ref, lf_ref, *prompt_refs, prompt):
    qg, kg = qg_ref[...], kg_ref[...]
    scale = HEAD_DIM ** -0.5 * LOG2E
    for h in range(N_HEADS):
        x = p_ref[:, h * HEAD_DIM:(h + 1) * HEAD_DIM]
        q_ref[:, h * HEAD_DIM:(h + 1) * HEAD_DIM] = (_rms(x, qg) * scale).astype(BF16)
    k0 = N_HEADS * HEAD_DIM
    v0 = k0 + N_KV * HEAD_DIM
    kh_ref, vt_ref = prompt_refs if prompt else (None, None)
    _store_kv_heads(p_ref, k0, v0, lambda x: _rms(x, kg), k_ref, v_ref, kh_ref, vt_ref)
    f0 = v0 + N_KV * HEAD_DIM
    lf_ref[...] = _log_sigmoid(p_ref[:, f0:f0 + N_HEADS] + bf_ref[...])


def fox_post(proj, b_f, q_gain, k_gain, tm, prompt):
    m, n = proj.shape
    row = lambda i: (i, 0)
    const = lambda i: (0, 0)
    nkv = N_KV * HEAD_DIM
    outs = [((m, N_HEADS * HEAD_DIM), BF16, pl.BlockSpec((tm, N_HEADS * HEAD_DIM), row)),
            ((m, nkv), F32, pl.BlockSpec((tm, nkv), row)),
            ((m, nkv), F32, pl.BlockSpec((tm, nkv), row)),
            ((m, N_HEADS), F32, pl.BlockSpec((tm, N_HEADS), row))]
    if prompt:
        assert tm == KC
        outs += [((N_KV, m, HEAD_DIM), BF16, pl.BlockSpec((N_KV, tm, HEAD_DIM), lambda i: (0, i, 0))),
                 ((N_KV, m // KC, HEAD_DIM, KC), BF16, pl.BlockSpec((N_KV, 1, HEAD_DIM, KC), lambda i: (0, i, 0, 0)))]
    return pl.pallas_call(
        functools.partial(_fox_post_kernel, prompt=prompt),
        grid=(m // tm,),
        in_specs=[pl.BlockSpec((tm, n), row), pl.BlockSpec((1, HEAD_DIM), const),
                  pl.BlockSpec((1, HEAD_DIM), const), pl.BlockSpec((1, N_HEADS), const)],
        out_specs=[o[2] for o in outs],
        out_shape=[jax.ShapeDtypeStruct(o[0], o[1]) for o in outs],
        compiler_params=_params(("parallel",)),
        name="fox_post",
    )(proj, q_gain.reshape(1, -1), k_gain.reshape(1, -1), b_f.reshape(1, -1))


def _queries_t(q_ref, j):
    cols = []
    for g in range(GROUP):
        x = q_ref[0, :, (GROUP * j + g) * HEAD_DIM:(GROUP * j + g + 1) * HEAD_DIM]
        cols.append(x.astype(F32).T.astype(BF16))
    return jnp.concatenate(cols, axis=1)


def _attend_heads(s_first, logits_fn, values_fn, m_ref, l_ref, acc_ref, next_first_fn=None):
    s_next = s_first
    pending = None
    for j in range(N_KV):
        s_t = s_next
        if j + 1 < N_KV:
            s_next = logits_fn(j + 1)
        elif next_first_fn is not None:
            s_next = next_first_fn()
        m_run = m_ref[j]
        m_new = jnp.maximum(m_run, jnp.max(s_t, axis=0, keepdims=True))
        alpha = jnp.exp2(m_run - m_new)
        p = jnp.exp2(s_t - m_new)
        l_ref[j] = alpha * l_ref[j] + jnp.sum(p, axis=0, keepdims=True)
        m_ref[j] = m_new
        pv = jnp.dot(values_fn(j), p.astype(BF16), preferred_element_type=F32)
        if pending is not None:
            jp, alpha_p, pv_p = pending
            acc_ref[jp] = alpha_p * acc_ref[jp] + pv_p
        pending = (j, alpha, pv)
    jp, alpha_p, pv_p = pending
    acc_ref[jp] = alpha_p * acc_ref[jp] + pv_p
    return s_next


def _softmax_reset(m_ref, l_ref, acc_ref):
    m_ref[...] = jnp.full_like(m_ref, NEG)
    l_ref[...] = jnp.zeros_like(l_ref)
    acc_ref[...] = jnp.zeros_like(acc_ref)


def _softmax_scratch(qk_depth):
    rows = GROUP * TQ
    return [pltpu.VMEM((N_KV, qk_depth, rows), BF16), pltpu.VMEM((N_KV, 1, rows), F32),
            pltpu.VMEM((N_KV, 1, rows), F32), pltpu.VMEM((N_KV, HEAD_DIM, rows), F32)]


def _store_heads_t(o_ref, l_ref, acc_ref):
    for j in range(N_KV):
        o_t = acc_ref[j] / l_ref[j]
        for g in range(GROUP):
            o_ref[0, :, (GROUP * j + g) * HEAD_DIM:(GROUP * j + g + 1) * HEAD_DIM] = (
                o_t[:, g * TQ:(g + 1) * TQ].T.astype(o_ref.dtype))


def _cumsum_kernel(lf_ref, hi_ref, mid_ref, lo_ref):
    t = lf_ref.shape[1]
    r = lax.broadcasted_iota(I32, (LANES, LANES), 0)
    c = lax.broadcasted_iota(I32, (LANES, LANES), 1)
    lower = (c <= r).astype(F32)
    carry = jnp.zeros((1, lf_ref.shape[2]), F32)
    for b in range(t // LANES):
        rows = slice(b * LANES, (b + 1) * LANES)
        cs = jnp.dot(lower, lf_ref[0, rows, :], precision=HI, preferred_element_type=F32) + carry
        carry = cs[LANES - 1:LANES, :]
        x = cs * LOG2E
        hi = x.astype(BF16)
        r1 = x - hi.astype(F32)
        mid = r1.astype(BF16)
        hi_ref[0, rows, :] = hi
        mid_ref[0, rows, :] = mid
        lo_ref[0, rows, :] = (r1 - mid.astype(F32)).astype(BF16)


def cumsum_rows_split(lf):
    b, t, h = lf.shape
    spec = pl.BlockSpec((1, t, h), lambda i: (i, 0, 0))
    return pl.pallas_call(
        _cumsum_kernel,
        grid=(b,),
        in_specs=[spec],
        out_specs=[spec] * 3,
        out_shape=[jax.ShapeDtypeStruct((b, t, h), BF16)] * 3,
        compiler_params=_params(("parallel",)),
        name="cumsum_rows_split",
    )(lf)


def _fox_kernel(q_ref, k_ref, cp_ref, v_ref, o_ref, qt_ref, m_ref, l_ref, acc_ref):
    i = pl.program_id(1)
    rows = GROUP * TQ
    last = (i * TQ) // KC
    r_io = lax.broadcasted_iota(I32, (HEAD_DIM, rows), 0)
    g_io = lax.broadcasted_iota(I32, (HEAD_DIM, rows), 1) // TQ
    sel = jnp.where((r_io >= 3 * g_io) & (r_io < 3 * g_io + 3), -1.0, 0.0).astype(BF16)
    for j in range(N_KV):
        qt_ref[j] = jnp.concatenate([_queries_t(q_ref, j), sel], axis=0)
    _softmax_reset(m_ref, l_ref, acc_ref)

    def logits(j, c):
        start = pl.multiple_of(c * KC, KC)
        kc = jnp.concatenate([k_ref[j, 0, pl.ds(start, KC), :], cp_ref[0, j, pl.ds(start, KC), :]], axis=1)
        return jnp.dot(kc, qt_ref[j], preferred_element_type=F32)

    def body(c, s_first):
        return _attend_heads(s_first, lambda j: logits(j, c), lambda j: v_ref[j, 0, c], m_ref, l_ref, acc_ref,
                             next_first_fn=lambda: logits(0, c + 1))

    s_first = lax.fori_loop(0, last, body, logits(0, 0))
    key_io = lax.broadcasted_iota(I32, (KC, rows), 0)
    q_io = lax.broadcasted_iota(I32, (KC, rows), 1) % TQ + i * TQ
    causal = key_io + last * KC <= q_io
    _attend_heads(jnp.where(causal, s_first, NEG), lambda j: jnp.where(causal, logits(j, last), NEG),
                  lambda j: v_ref[j, 0, last], m_ref, l_ref, acc_ref)
    _store_heads_t(o_ref, l_ref, acc_ref)


def fox_attention(q, kh, cparts, vt):
    b, t, _ = q.shape
    return pl.pallas_call(
        _fox_kernel,
        grid=(b, t // TQ),
        in_specs=[pl.BlockSpec((1, TQ, N_HEADS * HEAD_DIM), lambda bi, i: (bi, i, 0)),
                  pl.BlockSpec((N_KV, 1, t, HEAD_DIM), lambda bi, i: (0, bi, 0, 0)),
                  pl.BlockSpec((1, N_KV, t, HEAD_DIM), lambda bi, i: (bi, 0, 0, 0)),
                  pl.BlockSpec((N_KV, 1, t // KC, HEAD_DIM, KC), lambda bi, i: (0, bi, 0, 0, 0))],
        out_specs=pl.BlockSpec((1, TQ, N_HEADS * HEAD_DIM), lambda bi, i: (bi, i, 0)),
        out_shape=jax.ShapeDtypeStruct((b, t, N_HEADS * HEAD_DIM), BF16),
        scratch_shapes=_softmax_scratch(2 * HEAD_DIM),
        compiler_params=_params(("parallel", "arbitrary")),
        name="fox_attention",
    )(q, kh, cparts, vt)


def _all_true(flags):
    return (jnp.min(flags.astype(F32)) > 0.5).astype(I32)


def _bisect_threshold(count_ge, lo, hi, cnt_lo, nsel):
    steps = 2

    def cond(state):
        it, _, _, _, settled, _ = state
        return (it < BISECT_MAX // steps) & (settled == 0)

    def body(state):
        it, lo, hi, cnt_lo, _, _ = state
        stuck = jnp.zeros(lo.shape, I32)
        for _ in range(steps):
            mid = lo + (hi - lo) * 0.5
            stuck = jnp.where((mid <= lo) | (mid >= hi), 1, 0)
            cnt = count_ge(mid)
            up = cnt >= nsel
            lo = jnp.where(up, mid, lo)
            cnt_lo = jnp.where(up, cnt, cnt_lo)
            hi = jnp.where(up, hi, mid)
        exact = jnp.where(cnt_lo == nsel, 1, 0)
        return it + 1, lo, hi, cnt_lo, _all_true(jnp.maximum(exact, stuck)), _all_true(exact)

    init = (jnp.int32(0), lo, hi, cnt_lo, jnp.int32(0), jnp.int32(0))
    _, lo, _, cnt_lo, _, all_exact = lax.while_loop(cond, body, init)
    return lo, cnt_lo, all_exact


def _exact_threshold(min_ge, count_gt_and_next, lo, nsel):
    def cond(state):
        it, _, _, _, _, finished = state
        return (it < BISECT_MAX) & (finished == 0)

    def body(state):
        it, v, tau, need, done, _ = state
        n_gt, v_next = count_gt_and_next(v)
        fin = (n_gt < nsel) & (done == 0)
        tau = jnp.where(fin, v, tau)
        need = jnp.where(fin, nsel - n_gt, need)
        done = jnp.where(fin, 1, done)
        v = jnp.where(done == 1, v, v_next)
        return it + 1, v, tau, need, done, _all_true(done)

    v0 = min_ge(lo)
    init = (jnp.int32(0), v0, v0, jnp.zeros(lo.shape, F32), jnp.zeros(lo.shape, I32), jnp.int32(0))
    _, _, tau, need, _, _ = lax.while_loop(cond, body, init)
    return tau, need


def _select_tie_index(count_eq_below, need, nbits, shape):
    def bit_body(bi, m):
        cand = m | jnp.left_shift(jnp.int32(1), nbits - 1 - bi)
        return jnp.where(count_eq_below(cand) < need, cand, m)

    return lax.fori_loop(0, nbits, bit_body, jnp.zeros(shape, I32))


def _dsa_kernel(iq_ref, iw_ref, ik_ref, q_ref, k_ref, v_ref, o_ref, sc_ref, bias_ref, qt_ref, m_ref, l_ref, acc_ref,
                *, nsel, t_total):
    i = pl.program_id(1)
    nch = (i * TQ + TQ + KC - 1) // KC
    w = iw_ref[0, 0]
    iq_t = iq_ref[0]
    key_io = lax.broadcasted_iota(I32, (KC, TQ), 0)
    qpos = i * TQ + lax.broadcasted_iota(I32, (KC, TQ), 1)
    inf = float("inf")

    def fold(x, op):
        part = op(x.reshape(4, KC // (4 * SUBLANES), SUBLANES, TQ), axis=1)
        return op(part, axis=0)

    def score_body(c, carry):
        lo, hi = carry
        ikc = ik_ref[0, pl.ds(pl.multiple_of(c * KC, KC), KC), :]
        y = jnp.dot(ikc, iq_t, preferred_element_type=F32)
        acc = jnp.zeros((KC, TQ), F32)
        for h in range(N_IDX_HEADS):
            acc = acc + w[h:h + 1, :] * jnp.maximum(y[:, h * TQ:(h + 1) * TQ], 0.0)
        causal = key_io + c * KC <= qpos
        sc_ref[c] = jnp.where(causal, acc, -inf)
        return (jnp.minimum(lo, fold(jnp.where(causal, acc, inf), jnp.min)),
                jnp.maximum(hi, fold(jnp.where(causal, acc, -inf), jnp.max)))

    lo8, hi8 = lax.fori_loop(0, nch, score_body,
                             (jnp.full((SUBLANES, TQ), inf, F32), jnp.full((SUBLANES, TQ), -inf, F32)))

    def count(pred):
        def body(c, acc):
            return acc + fold(jnp.where(pred(sc_ref[c], c), 1.0, 0.0), jnp.sum)
        acc = lax.fori_loop(0, nch, body, jnp.zeros((SUBLANES, TQ), F32))
        return jnp.sum(acc, axis=0, keepdims=True)

    def select():
        lo = jnp.min(lo8, axis=0, keepdims=True)
        hi = jnp.max(hi8, axis=0, keepdims=True)
        n_causal = (qpos[:1] + 1).astype(F32)
        lo, _, all_exact = _bisect_threshold(lambda thr: count(lambda s, c: s >= thr), lo, hi, n_causal, float(nsel))

        def ties():
            def min_ge(lo):
                def body(c, acc):
                    s = sc_ref[c]
                    return jnp.minimum(acc, fold(jnp.where(s >= lo, s, inf), jnp.min))
                acc = lax.fori_loop(0, nch, body, jnp.full((SUBLANES, TQ), inf, F32))
                return jnp.min(acc, axis=0, keepdims=True)

            def count_gt_and_next(v):
                def body(c, carry):
                    n, nxt = carry
                    s = sc_ref[c]
                    above = s > v
                    return (n + fold(jnp.where(above, 1.0, 0.0), jnp.sum),
                            jnp.minimum(nxt, fold(jnp.where(above, s, inf), jnp.min)))
                n, nxt = lax.fori_loop(0, nch, body, (jnp.zeros((SUBLANES, TQ), F32),
                                                      jnp.full((SUBLANES, TQ), inf, F32)))
                return jnp.sum(n, axis=0, keepdims=True), jnp.min(nxt, axis=0, keepdims=True)

            tau, need = _exact_threshold(min_ge, count_gt_and_next, lo, float(nsel))
            nbits = max(1, int(t_total - 1).bit_length())
            m_star = _select_tie_index(
                lambda cand: count(lambda s, c: (s == tau) & (key_io + c * KC < cand)), need, nbits, (1, TQ))
            return tau, m_star

        return lax.cond(all_exact == 1, lambda: (lo, jnp.full((1, TQ), t_total, I32)), ties)

    tau, m_star = lax.cond(i * TQ + TQ <= nsel,
                           lambda: (jnp.full((1, TQ), -inf, F32), jnp.full((1, TQ), t_total, I32)), select)

    def bias_body(c, _):
        s = sc_ref[c]
        idx = key_io + c * KC
        keep = ((s > tau) | ((s == tau) & (idx <= m_star))) & (idx <= qpos)
        bias_ref[c] = jnp.where(keep, 0.0, NEG)
        return 0

    lax.fori_loop(0, nch, bias_body, 0)

    for j in range(N_KV):
        qt_ref[j] = _queries_t(q_ref, j)
    _softmax_reset(m_ref, l_ref, acc_ref)

    def logits(j, c):
        kc = k_ref[j, 0, pl.ds(pl.multiple_of(c * KC, KC), KC), :]
        bias = jnp.concatenate([bias_ref[c]] * GROUP, axis=1)
        return jnp.dot(kc, qt_ref[j], preferred_element_type=F32) + bias

    def body(c, s_first):
        return _attend_heads(s_first, lambda j: logits(j, c), lambda j: v_ref[j, 0, c], m_ref, l_ref, acc_ref,
                             next_first_fn=lambda: logits(0, jnp.minimum(c + 1, nch - 1)))

    lax.fori_loop(0, nch, body, logits(0, 0))
    _store_heads_t(o_ref, l_ref, acc_ref)


def dsa_attention(iq_t, iw_t, ikb, q, kh, vt):
    b, t, _ = q.shape
    nsel = min(TOPK_MAX, t // 4)
    nch = t // KC
    kern = functools.partial(_dsa_kernel, nsel=nsel, t_total=t)
    return pl.pallas_call(
        kern,
        grid=(b, t // TQ),
        in_specs=[pl.BlockSpec((1, IDX_DIM, N_IDX_HEADS * TQ), lambda bi, i: (bi, i, 0)),
                  pl.BlockSpec((1, 1, N_IDX_HEADS, TQ), lambda bi, i: (bi, i, 0, 0)),
                  pl.BlockSpec((1, t, IDX_DIM), lambda bi, i: (bi, 0, 0)),
                  pl.BlockSpec((1, TQ, N_HEADS * HEAD_DIM), lambda bi, i: (bi, i, 0)),
                  pl.BlockSpec((N_KV, 1, t, HEAD_DIM), lambda bi, i: (0, bi, 0, 0)),
                  pl.BlockSpec((N_KV, 1, nch, HEAD_DIM, KC), lambda bi, i: (0, bi, 0, 0, 0))],
        out_specs=pl.BlockSpec((1, TQ, N_HEADS * HEAD_DIM), lambda bi, i: (bi, i, 0)),
        out_shape=jax.ShapeDtypeStruct((b, t, N_HEADS * HEAD_DIM), BF16),
        scratch_shapes=[pltpu.VMEM((nch, KC, TQ), F32), pltpu.VMEM((nch, KC, TQ), F32)] + _softmax_scratch(HEAD_DIM),
        compiler_params=_params(("parallel", "arbitrary")),
        name="dsa_attention",
    )(iq_t, iw_t, ikb, q, kh, vt)


def _page_copy(pool_ref, layer, pt_ref, b, p, buf, slot, sem):
    return pltpu.make_async_copy(pool_ref.at[layer, pt_ref[b, p]], buf.at[slot, p], sem.at[slot])


def _fetch_sequence_pages(pool_ref, layer, pt_ref, buf, sem):
    b = pl.program_id(0)
    nb = pl.num_programs(0)
    npages = buf.shape[1]
    slot = b % 2

    def start_all(bb, sl):
        lax.fori_loop(0, npages, lambda p, _: (_page_copy(pool_ref, layer, pt_ref, bb, p, buf, sl, sem).start(), 0)[1], 0)

    @pl.when(b == 0)
    def _():
        start_all(0, 0)

    @pl.when(b + 1 < nb)
    def _():
        start_all(b + 1, 1 - slot)

    lax.fori_loop(0, npages, lambda p, _: (_page_copy(pool_ref, layer, pt_ref, b, p, buf, slot, sem).wait(), 0)[1], 0)
    return slot


def _dsa_sample_index_kernel(pt_ref, iq_ref, iw_ref, pool_ref, ikn_ref, bias_ref, biasn_ref, buf, sem, sc_ref,
                             *, layer, nsel):
    slot = _fetch_sequence_pages(pool_ref, layer, pt_ref, buf, sem)
    npages = buf.shape[1]
    iq = iq_ref[0]
    w = iw_ref[0]

    unroll = math.gcd(npages, 8)

    def score_body(pb, _):
        for k in range(unroll):
            p = pb * unroll + k
            y = jnp.dot(iq, buf[slot, p].astype(BF16), preferred_element_type=F32)
            sc_ref[pl.ds(p, 1), :] = jnp.sum(w * jnp.maximum(y, 0.0), axis=0, keepdims=True)
        return 0

    lax.fori_loop(0, npages // unroll, score_body, 0)
    prod = iq.astype(F32) * ikn_ref[0].astype(BF16).astype(F32)
    yn = jnp.sum(prod, axis=1, keepdims=True)
    sn = jnp.sum(w * jnp.maximum(yn, 0.0), axis=0, keepdims=True)
    sc = sc_ref[...]
    idx = lax.broadcasted_iota(I32, sc.shape, 0) * PAGE + lax.broadcasted_iota(I32, sc.shape, 1)
    inf = float("inf")

    def total(x, op=jnp.sum):
        return op(op(x, axis=0, keepdims=True), axis=1, keepdims=True)

    def count_ge(thr):
        return total(jnp.where(sc >= thr, 1.0, 0.0)) + jnp.where(sn >= thr, 1.0, 0.0)

    def min_ge(lo):
        return jnp.minimum(total(jnp.where(sc >= lo, sc, inf), jnp.min), jnp.where(sn >= lo, sn, inf))

    def count_gt_and_next(v):
        n = total(jnp.where(sc > v, 1.0, 0.0)) + jnp.where(sn > v, 1.0, 0.0)
        return n, jnp.minimum(total(jnp.where(sc > v, sc, inf), jnp.min), jnp.where(sn > v, sn, inf))

    lo = jnp.minimum(total(sc, jnp.min), sn)
    hi = jnp.maximum(total(sc, jnp.max), sn)
    lo, _, _ = _bisect_threshold(count_ge, lo, hi, jnp.full((1, 1), float(npages * PAGE + 1), F32), float(nsel))
    tau, need = _exact_threshold(min_ge, count_gt_and_next, lo, float(nsel))
    nbits = int(npages * PAGE).bit_length()
    m_star = _select_tie_index(
        lambda cand: total(jnp.where((sc == tau) & (idx < cand), 1.0, 0.0)), need, nbits, (1, 1))
    n_eq = total(jnp.where(sc == tau, 1.0, 0.0))
    keep = (sc > tau) | ((sc == tau) & (idx <= m_star))
    keep_rows = jnp.dot(jnp.where(keep, 1.0, 0.0).astype(BF16), _row_expansion().astype(BF16),
                        preferred_element_type=F32)
    bias_ref[0] = jnp.where(keep_rows > 0.5, 0.0, NEG)
    keep_n = (sn > tau) | ((sn == tau) & (n_eq < need))
    biasn_ref[0] = jnp.broadcast_to(jnp.where(keep_n, 0.0, NEG), (1, LANES))


def dsa_sample_index(page_table, iq, iw, pool_ik_t, layer, ik_new):
    b, npages = page_table.shape
    nsel = min(TOPK_MAX, (npages * PAGE + 1) // 4)
    per_b = lambda bi, pt: (bi, 0, 0)
    grid_spec = pltpu.PrefetchScalarGridSpec(
        num_scalar_prefetch=1,
        grid=(b,),
        in_specs=[pl.BlockSpec((1, N_IDX_HEADS, IDX_DIM), per_b),
                  pl.BlockSpec((1, N_IDX_HEADS, 1), per_b),
                  pl.BlockSpec(memory_space=pl.ANY),
                  pl.BlockSpec((1, 1, IDX_DIM), per_b)],
        out_specs=[pl.BlockSpec((1, npages, PAGE * N_KV), per_b), pl.BlockSpec((1, 1, LANES), per_b)],
        scratch_shapes=[pltpu.VMEM((2, npages, IDX_DIM, PAGE), F32), pltpu.SemaphoreType.DMA((2,)),
                        pltpu.VMEM((npages, PAGE), F32)])
    return pl.pallas_call(
        functools.partial(_dsa_sample_index_kernel, layer=layer, nsel=nsel),
        grid_spec=grid_spec,
        out_shape=[jax.ShapeDtypeStruct((b, npages, PAGE * N_KV), F32), jax.ShapeDtypeStruct((b, 1, LANES), F32)],
        compiler_params=_params(("arbitrary",)),
        name="dsa_sample_index",
    )(page_table, iq, iw, pool_ik_t, ik_new)


def _fox_sample_bias_kernel(pt_ref, pool_ref, lfn_ref, bias_ref, buf, sem, *, layer):
    slot = _fetch_sequence_pages(pool_ref, layer, pt_ref, buf, sem)
    npages, h = buf.shape[1], buf.shape[2]
    rows = npages * h
    x = buf[slot].reshape(rows, PAGE)
    r = lax.broadcasted_iota(I32, (PAGE, PAGE), 0)
    c = lax.broadcasted_iota(I32, (PAGE, PAGE), 1)
    within = jnp.dot(x, (r > c).astype(F32), precision=HI, preferred_element_type=F32)
    tot = jnp.dot(x, jnp.ones((PAGE, PAGE), F32), precision=HI, preferred_element_type=F32)
    row = lax.broadcasted_iota(I32, (rows, PAGE), 0)
    suffix = tot
    shift = h
    while shift < rows:
        suffix = suffix + jnp.where(row + shift < rows, pltpu.roll(suffix, rows - shift, 0), 0.0)
        shift *= 2
    per_tok = within + (suffix - tot)
    per_row = jnp.dot(per_tok, _row_expansion().astype(F32), precision=HI, preferred_element_type=F32)
    bias_ref[0] = (per_row.reshape(npages, h, PAGE * N_KV) + lfn_ref[0][None]) * LOG2E


def fox_sample_bias(page_table, pool_lf_t, layer, lf_new):
    b, npages = page_table.shape
    h = pool_lf_t.shape[2]
    grid_spec = pltpu.PrefetchScalarGridSpec(
        num_scalar_prefetch=1,
        grid=(b,),
        in_specs=[pl.BlockSpec(memory_space=pl.ANY),
                  pl.BlockSpec((1, h, 1), lambda bi, pt: (bi, 0, 0))],
        out_specs=pl.BlockSpec((1, npages, h, PAGE * N_KV), lambda bi, pt: (bi, 0, 0, 0)),
        scratch_shapes=[pltpu.VMEM((2, npages, h, PAGE), F32), pltpu.SemaphoreType.DMA((2,))])
    return pl.pallas_call(
        functools.partial(_fox_sample_bias_kernel, layer=layer),
        grid_spec=grid_spec,
        out_shape=jax.ShapeDtypeStruct((b, npages, h, PAGE * N_KV), F32),
        compiler_params=_params(("arbitrary",)),
        name="fox_sample_bias",
    )(page_table, pool_lf_t, lf_new)


def _decode_kernel(pt_ref, q_ref, pk_ref, pv_ref, b_ref, kn_ref, vn_ref, bn_ref, o_ref, kbuf, vbuf, sem,
                   *, layer, per_head_bias):
    b = pl.program_id(0)
    npages = b_ref.shape[1]
    pg = kbuf.shape[1]
    ngroups = npages // pg
    rows = PAGE * N_KV
    q = q_ref[0]
    row_kv = lax.broadcasted_iota(I32, (N_HEADS, rows), 1) % N_KV
    head_mask = jnp.where(row_kv == lax.broadcasted_iota(I32, (N_HEADS, rows), 0) // GROUP, 0.0, NEG)

    def copies(g, slot):
        out = []
        for i in range(pg):
            page = pt_ref[b, g * pg + i]
            out.append(pltpu.make_async_copy(pk_ref.at[layer, page], kbuf.at[slot, i], sem.at[0, slot]))
            out.append(pltpu.make_async_copy(pv_ref.at[layer, page], vbuf.at[slot, i], sem.at[1, slot]))
        return out

    for cp in copies(0, 0):
        cp.start()

    def body(g, carry):
        m_run, l_run, acc = carry
        slot = g % 2

        @pl.when(g + 1 < ngroups)
        def _():
            for cp in copies(g + 1, 1 - slot):
                cp.start()

        for cp in copies(g, slot):
            cp.wait()
        logits = []
        for i in range(pg):
            s = lax.dot_general(q, kbuf[slot, i].astype(BF16), NT, preferred_element_type=F32)
            page_bias = b_ref[0, g * pg + i] if per_head_bias else b_ref[0, pl.ds(g * pg + i, 1), :]
            logits.append(s + page_bias + head_mask)
        s_all = jnp.concatenate(logits, axis=1)
        m_new = jnp.maximum(m_run, jnp.max(s_all, axis=1, keepdims=True))
        alpha = jnp.exp2(m_run - m_new)
        p = jnp.exp2(s_all - m_new)
        l_new = alpha * l_run + jnp.sum(p, axis=1, keepdims=True)
        pv = jnp.zeros((N_HEADS, HEAD_DIM), F32)
        for i in range(pg):
            pb = p[:, i * rows:(i + 1) * rows].astype(BF16)
            pv = pv + jnp.dot(pb, vbuf[slot, i].astype(BF16), preferred_element_type=F32)
        return m_new, l_new, alpha * acc + pv

    init = (jnp.full((N_HEADS, 1), NEG, F32), jnp.zeros((N_HEADS, 1), F32), jnp.zeros((N_HEADS, HEAD_DIM), F32))
    m_run, l_run, acc = lax.fori_loop(0, ngroups, body, init)

    kn = kn_ref[0].astype(BF16).astype(F32)
    s_n = jnp.sum(q.astype(F32) * kn, axis=1, keepdims=True) + bn_ref[0]
    m_fin = jnp.maximum(m_run, s_n)
    alpha = jnp.exp2(m_run - m_fin)
    p_n = jnp.exp2(s_n - m_fin)
    l_fin = alpha * l_run + p_n
    vn = vn_ref[0].astype(BF16).astype(F32)
    o_ref[0] = (alpha * acc + p_n.astype(BF16).astype(F32) * vn) / l_fin


def paged_decode_attention(page_table, q, cache_k, cache_v, layer, bias, k_new, v_new, bias_new):
    b, npages = page_table.shape
    rows = PAGE * N_KV
    nl, npool = cache_k.shape[:2]
    per_head_bias = bias.ndim == 4
    bias_spec = (pl.BlockSpec((1, npages, N_HEADS, rows), lambda bi, pt: (bi, 0, 0, 0)) if per_head_bias
                 else pl.BlockSpec((1, npages, rows), lambda bi, pt: (bi, 0, 0)))
    pg = math.gcd(DECODE_PAGES, npages)
    per_b = lambda bi, pt: (bi, 0, 0)
    per_head = lambda x: jnp.repeat(x.reshape(b, N_KV, HEAD_DIM), GROUP, axis=1)
    grid_spec = pltpu.PrefetchScalarGridSpec(
        num_scalar_prefetch=1,
        grid=(b,),
        in_specs=[pl.BlockSpec((1, N_HEADS, HEAD_DIM), per_b),
                  pl.BlockSpec(memory_space=pl.ANY),
                  pl.BlockSpec(memory_space=pl.ANY),
                  bias_spec,
                  pl.BlockSpec((1, N_HEADS, HEAD_DIM), per_b),
                  pl.BlockSpec((1, N_HEADS, HEAD_DIM), per_b),
                  pl.BlockSpec((1, N_HEADS, 1), per_b)],
        out_specs=pl.BlockSpec((1, N_HEADS, HEAD_DIM), per_b),
        scratch_shapes=[pltpu.VMEM((2, pg, rows, HEAD_DIM), F32),
                        pltpu.VMEM((2, pg, rows, HEAD_DIM), F32),
                        pltpu.SemaphoreType.DMA((2, 2))])
    return pl.pallas_call(
        functools.partial(_decode_kernel, layer=layer, per_head_bias=per_head_bias),
        grid_spec=grid_spec,
        out_shape=jax.ShapeDtypeStruct((b, N_HEADS, HEAD_DIM), F32),
        compiler_params=_params(("arbitrary",)),
        name="paged_decode_attention",
    )(page_table, q, cache_k.reshape(nl, npool, rows, HEAD_DIM), cache_v.reshape(nl, npool, rows, HEAD_DIM),
      bias, per_head(k_new), per_head(v_new), bias_new)


def _row_expansion():
    t = lax.broadcasted_iota(I32, (PAGE, PAGE * N_KV), 0)
    r = lax.broadcasted_iota(I32, (PAGE, PAGE * N_KV), 1)
    return r // N_KV == t


def _cmul(ar, ai, br, bi):
    return ar * br - ai * bi, ar * bi + ai * br


def _ssm_tables(a_re, a_im, log_dt, b_re, b_im, c_re, c_im, d_skip):
    L, cdim = SSM_CHUNK, SSM_GROUP
    g = a_re.shape[0]
    dt = jnp.exp(log_dt)[:, None]
    mag = jnp.exp(a_re * dt)
    ab = (mag * jnp.cos(a_im * dt), mag * jnp.sin(a_im * dt))
    den = a_re * a_re + a_im * a_im
    num = _cmul(ab[0] - 1.0, ab[1], a_re, -a_im)
    sc = (num[0] / den, num[1] / den)
    bs = _cmul(b_re, b_im, sc[0][:, :, None], sc[1][:, :, None])
    pows = [(jnp.ones_like(ab[0]), jnp.zeros_like(ab[0]))]
    for _ in range(L):
        pows.append(_cmul(*pows[-1], *ab))
    pw = (jnp.stack([p[0] for p in pows]), jnp.stack([p[1] for p in pows]))
    ca = _cmul(c_re[None], c_im[None], pw[0][:L, :, None, :], pw[1][:L, :, None, :])
    def contract_n(x, y):
        return jnp.sum(x.transpose(0, 1, 3, 2)[..., None] * y[None, :, :, None, :], axis=2)

    kd = contract_n(ca[0], bs[0]) - contract_n(ca[1], bs[1])
    dmat = jnp.eye(cdim, dtype=F32)[None] * d_skip.reshape(g, cdim)[:, :, None]
    kd = kd.at[0].add(dmat)
    ii = jnp.arange(L)
    diff = ii[None, :] - ii[:, None]
    kt = jnp.where((diff >= 0)[:, :, None, None, None], kd[jnp.clip(diff, 0, L - 1)], 0.0)
    m = kt.transpose(2, 0, 4, 1, 3).reshape(g, L * cdim, L * cdim)
    bst = (bs[0].transpose(0, 2, 1)[None], bs[1].transpose(0, 2, 1)[None])
    pj = _cmul(pw[0][L - 1 - ii][:, :, None, :], pw[1][L - 1 - ii][:, :, None, :], *bst)
    p = jnp.concatenate([x.transpose(1, 0, 2, 3).reshape(g, L * cdim, -1) for x in pj], axis=-1)
    cp = _cmul(c_re[None], c_im[None], pw[0][1:L + 1][:, :, None, :], pw[1][1:L + 1][:, :, None, :])
    cp = [x.transpose(1, 3, 0, 2).reshape(g, -1, L * cdim) for x in cp]
    q = jnp.concatenate([cp[0], -cp[1]], axis=1)
    return m.astype(BF16), p, q.astype(BF16), pw[0][L], pw[1][L], ab, bs


def _ssm_kernel(u_ref, m_ref, p_ref, q_ref, ar_ref, ai_ref, y_ref, hf_ref, ug_ref, yg_ref, *, nb):
    L, cdim, n = SSM_CHUNK, SSM_GROUP, SSM_STATE
    gpb = LANES // cdim
    r = ug_ref.shape[1]
    nk = r // nb
    for j in range(L):
        xj = u_ref[pl.ds(j, r, stride=L), :]
        for gg in range(gpb):
            ug_ref[gg, :, j * cdim:(j + 1) * cdim] = xj[:, gg * cdim:(gg + 1) * cdim]
    row = lax.broadcasted_iota(I32, (r, 2 * n), 0) % nk
    lane = lax.broadcasted_iota(I32, (1, 2 * n), 1)

    pair = 2

    def group_body(gi, _):
        ggs = [gi * pair + s for s in range(pair)]
        us = [ug_ref[gg] for gg in ggs]
        hs = [jnp.dot(u, p_ref[gg], precision=HI, preferred_element_type=F32) for u, gg in zip(us, ggs)]
        drs = [jnp.concatenate([ar_ref[gg], ar_ref[gg]], axis=1) for gg in ggs]
        dis = [jnp.concatenate([-ai_ref[gg], ai_ref[gg]], axis=1) for gg in ggs]
        shift = 1
        while shift < nk:
            for s in range(pair):
                prev = jnp.where(row >= shift, pltpu.roll(hs[s], shift, 0), 0.0)
                hs[s] = hs[s] + drs[s] * prev + dis[s] * pltpu.roll(prev, n, 1)
                re, im = drs[s], jnp.where(lane < n, -dis[s], dis[s])
                re2, im2 = re * re - im * im, 2.0 * re * im
                drs[s], dis[s] = re2, jnp.where(lane < n, -im2, im2)
            shift *= 2
        for s, gg in enumerate(ggs):
            hprev = jnp.where(row >= 1, pltpu.roll(hs[s], 1, 0), 0.0)
            yg_ref[gg] = (jnp.dot(us[s].astype(BF16), m_ref[gg], preferred_element_type=F32)
                          + jnp.dot(hprev.astype(BF16), q_ref[gg], preferred_element_type=F32))
            for b in range(nb):
                hf_ref[gg, b:b + 1, :] = hs[s][(b + 1) * nk - 1:(b + 1) * nk, :]
        return 0

    lax.fori_loop(0, gpb // pair, group_body, 0)
    for j in range(L):
        y_ref[pl.ds(j, r, stride=L), :] = jnp.concatenate(
            [yg_ref[gg, :, j * cdim:(j + 1) * cdim] for gg in range(gpb)], axis=1)


def ssm_prompt(u, m, p, q, al_re, al_im, nb):
    rows, d = u.shape
    g, w, n2 = p.shape
    gpb = LANES // SSM_GROUP
    r = rows // SSM_CHUNK
    gmap = lambda i: (i, 0, 0)
    return pl.pallas_call(
        functools.partial(_ssm_kernel, nb=nb),
        grid=(d // LANES,),
        in_specs=[pl.BlockSpec((rows, LANES), lambda i: (0, i)),
                  pl.BlockSpec((gpb, w, w), gmap), pl.BlockSpec((gpb, w, n2), gmap), pl.BlockSpec((gpb, n2, w), gmap),
                  pl.BlockSpec((gpb, 1, n2 // 2), gmap), pl.BlockSpec((gpb, 1, n2 // 2), gmap)],
        out_specs=[pl.BlockSpec((rows, LANES), lambda i: (0, i)), pl.BlockSpec((gpb, nb, n2), gmap)],
        out_shape=[jax.ShapeDtypeStruct((rows, d), F32), jax.ShapeDtypeStruct((g, nb, n2), F32)],
        scratch_shapes=[pltpu.VMEM((gpb, r, w), F32), pltpu.VMEM((gpb, r, w), F32)],
        compiler_params=_params(("parallel",)),
        name="ssm_prompt",
    )(u, m, p, q, al_re.reshape(g, 1, -1), al_im.reshape(g, 1, -1))


def _ssm_step_kernel(u_ref, hr_ref, hi_ref, ar_ref, ai_ref, br_ref, bi_ref, cr_ref, ci_ref, d_ref,
                     y_ref, or_ref, oi_ref):
    u = u_ref[0]
    bur = jnp.zeros(hr_ref.shape[1:], F32)
    bui = jnp.zeros(hr_ref.shape[1:], F32)
    for c in range(SSM_GROUP):
        uc = u[:, c:c + 1]
        bur = bur + br_ref[c] * uc
        bui = bui + bi_ref[c] * uc
    ar, ai = ar_ref[...], ai_ref[...]
    h0r, h0i = hr_ref[0], hi_ref[0]
    hr = ar * h0r - ai * h0i + bur
    hi = ar * h0i + ai * h0r + bui
    or_ref[0] = hr
    oi_ref[0] = hi
    lane = lax.broadcasted_iota(I32, u.shape, 1)
    y = d_ref[...] * u
    for c in range(SSM_GROUP):
        col = jnp.sum(cr_ref[c] * hr - ci_ref[c] * hi, axis=1, keepdims=True)
        y = y + jnp.where(lane == c, col, 0.0)
    y_ref[0] = y


def ssm_step(u, h_re, h_im, a_bar, bs, c_re, c_im, d_skip):
    b, g, c = u.shape
    n = h_re.shape[2]
    per_b = lambda i: (i, 0, 0)
    c3 = lambda i: (0, 0, 0)
    c2 = lambda i: (0, 0)
    return pl.pallas_call(
        _ssm_step_kernel,
        grid=(b,),
        in_specs=[pl.BlockSpec((1, g, c), per_b), pl.BlockSpec((1, g, n), per_b), pl.BlockSpec((1, g, n), per_b),
                  pl.BlockSpec((g, n), c2), pl.BlockSpec((g, n), c2),
                  pl.BlockSpec((c, g, n), c3), pl.BlockSpec((c, g, n), c3),
                  pl.BlockSpec((c, g, n), c3), pl.BlockSpec((c, g, n), c3),
                  pl.BlockSpec((g, c), c2)],
        out_specs=[pl.BlockSpec((1, g, c), per_b), pl.BlockSpec((1, g, n), per_b), pl.BlockSpec((1, g, n), per_b)],
        out_shape=[jax.ShapeDtypeStruct((b, g, c), F32), jax.ShapeDtypeStruct((b, g, n), F32),
                   jax.ShapeDtypeStruct((b, g, n), F32)],
        compiler_params=_params(("parallel",)),
        name="ssm_step",
    )(u, h_re, h_im, a_bar[0], a_bar[1], bs[0].transpose(2, 0, 1), bs[1].transpose(2, 0, 1),
      c_re.transpose(1, 0, 2), c_im.transpose(1, 0, 2), d_skip.reshape(g, c))


def _dsa_layer(xp, xs, gain, cache_k, cache_v, cache_ik, layer, page_table, w_in, w_out, q_gain, k_gain, ik_gain,
               bp, t, past_len, tm):
    mp, ms = xp.shape[0], xs.shape[0]
    nkv = N_KV * HEAD_DIM
    nq = t // TQ
    proj = norm_proj(xp, gain, w_in, layer, tm)
    q, k, v, ik, iw, kh, vt, ikb, iq_t = dsa_post(proj, jnp.arange(t), q_gain, k_gain, ik_gain, KC, True)
    iw_t = iw.reshape(bp, nq, TQ, N_IDX_HEADS).transpose(0, 1, 3, 2)
    o = dsa_attention(iq_t.reshape(bp, nq * IDX_DIM, N_IDX_HEADS * TQ), iw_t, ikb.reshape(bp, t, IDX_DIM),
                      q.reshape(bp, t, -1), kh.reshape(N_KV, bp, t, HEAD_DIM),
                      vt.reshape(N_KV, bp, t // KC, HEAD_DIM, KC))
    xp = out_res(xp, o.reshape(mp, -1), w_out, layer, tm)
    projs = norm_proj(xs, gain, w_in, layer, ms)
    pos_s = jnp.full((ms,), past_len, I32)
    qs, ks, vs, iks, iws, iqs = dsa_post(projs, pos_s, q_gain, k_gain, ik_gain, ms, False)
    bias, bias_n = dsa_sample_index(page_table, iqs.reshape(ms, N_IDX_HEADS, IDX_DIM),
                                    iws.reshape(ms, N_IDX_HEADS, 1), cache_ik.transpose(0, 1, 3, 2), layer,
                                    iks.reshape(ms, 1, IDX_DIM))
    bias_nh = jnp.broadcast_to(bias_n[:, :, :1], (ms, N_HEADS, 1))
    os_ = paged_decode_attention(page_table, qs.reshape(ms, N_HEADS, HEAD_DIM), cache_k, cache_v, layer, bias,
                                 ks, vs, bias_nh)
    xs = out_res(xs, os_.reshape(ms, -1), w_out, layer, ms)
    return xp, xs, (k, v, ik), (ks, vs, iks)


def _key_bias_columns(parts, bp, t):
    cols = jnp.stack(parts, axis=-1).reshape(bp, t, N_KV, GROUP * 3).transpose(0, 2, 1, 3)
    return jnp.pad(cols, ((0, 0), (0, 0), (0, 0), (0, HEAD_DIM - GROUP * 3)))


def _fox_layer(xp, xs, gain, cache_k, cache_v, cache_lf, layer, page_table, w_in, b_f, w_out, q_gain, k_gain,
               bp, t, tm):
    mp, ms = xp.shape[0], xs.shape[0]
    nkv = N_KV * HEAD_DIM
    proj = norm_proj(xp, gain, w_in, layer, tm)
    q, k, v, lf, kh, vt = fox_post(proj, b_f, q_gain, k_gain, KC, True)
    c_parts = cumsum_rows_split(lf.reshape(bp, t, N_HEADS))
    o = fox_attention(q.reshape(bp, t, -1), kh.reshape(N_KV, bp, t, HEAD_DIM), _key_bias_columns(c_parts, bp, t),
                      vt.reshape(N_KV, bp, t // KC, HEAD_DIM, KC))
    xp = out_res(xp, o.reshape(mp, -1), w_out, layer, tm)
    projs = norm_proj(xs, gain, w_in, layer, ms)
    qs, ks, vs, lfs = fox_post(projs, b_f, q_gain, k_gain, ms, False)
    bias = fox_sample_bias(page_table, cache_lf.transpose(0, 1, 3, 2), layer, lfs.reshape(ms, N_HEADS, 1))
    os_ = paged_decode_attention(page_table, qs.reshape(ms, N_HEADS, HEAD_DIM), cache_k, cache_v, layer, bias,
                                 ks, vs, jnp.zeros((ms, N_HEADS, 1), F32))
    xs = out_res(xs, os_.reshape(ms, -1), w_out, layer, ms)
    return xp, xs, (k, v, lf), (ks, vs, lfs)


def _ssm_layer(xp, xs, gain, h0_re, h0_im, a_re, a_im, log_dt, b_re, b_im, c_re, c_im, d_skip, w_glu, layer,
               bp, t, tm):
    mp, ms = xp.shape[0], xs.shape[0]
    g, n, cdim, L = N_SSM_GROUPS, SSM_STATE, SSM_GROUP, SSM_CHUNK
    m, p, q, al_re, al_im, a_bar, bs = _ssm_tables(a_re, a_im, log_dt, b_re, b_im, c_re, c_im, d_skip)
    u = rmsnorm(xp, gain, tm)
    y, hf = ssm_prompt(u, m, p, q, al_re, al_im, bp)
    xp = glu_res(xp, y, w_glu, layer, tm)
    hf = hf.transpose(1, 0, 2)
    us = rmsnorm(xs, gain, ms)
    ys, hs_re, hs_im = ssm_step(us.reshape(ms, g, cdim), h0_re, h0_im, a_bar, bs, c_re, c_im, d_skip)
    xs = glu_res(xs, ys.reshape(ms, g * cdim), w_glu, layer, ms)
    return xp, xs, (hf[..., :n], hf[..., n:]), (hs_re, hs_im)


def kernel(x_prompt, x_sample, cache_dsa_k, cache_dsa_v, cache_dsa_idx_k, cache_fox_k, cache_fox_v, cache_fox_logf, state_ssm_re, state_ssm_im, page_table, norm_mix, norm_ffn, dsa_w_in, dsa_w_out, dsa_q_gain, dsa_k_gain, dsa_ik_gain, fox_w_in, fox_b_f, fox_w_out, fox_q_gain, fox_k_gain, ssm_a_re, ssm_a_im, ssm_log_dt, ssm_b_re, ssm_b_im, ssm_c_re, ssm_c_im, ssm_d, ssm_w_glu, ffn_w_gu, ffn_w_down):
    bp, t, d = x_prompt.shape
    bs_, ts, _ = x_sample.shape
    assert ts == 1, "the sample group carries one new position per sequence"
    assert t % KC == 0 and KC % TQ == 0
    depth = norm_mix.shape[0]
    past_len = page_table.shape[1] * PAGE
    tm = min(512, bp * t)
    xp = x_prompt.reshape(bp * t, d)
    xs = x_sample.reshape(bs_ * ts, d)
    dsa_w_in, dsa_w_out, fox_w_in, fox_w_out, ssm_w_glu, ffn_w_gu, ffn_w_down = (
        w.astype(BF16) for w in (dsa_w_in, dsa_w_out, fox_w_in, fox_w_out, ssm_w_glu, ffn_w_gu, ffn_w_down))
    dsa_p, dsa_s, fox_p, fox_s, ssm_p, ssm_s = [], [], [], [], [], []
    for i in range(depth):
        kind, j = i % N_MIXERS, i // N_MIXERS
        if kind == 0:
            xp, xs, outp, outs = _dsa_layer(
                xp, xs, norm_mix[i], cache_dsa_k, cache_dsa_v, cache_dsa_idx_k, j, page_table,
                dsa_w_in, dsa_w_out, dsa_q_gain[j], dsa_k_gain[j], dsa_ik_gain[j], bp, t, past_len, tm)
            dsa_p.append(outp); dsa_s.append(outs)
        elif kind == 1:
            xp, xs, outp, outs = _fox_layer(
                xp, xs, norm_mix[i], cache_fox_k, cache_fox_v, cache_fox_logf, j, page_table,
                fox_w_in, fox_b_f[j], fox_w_out, fox_q_gain[j], fox_k_gain[j], bp, t, tm)
            fox_p.append(outp); fox_s.append(outs)
        else:
            xp, xs, outp, outs = _ssm_layer(
                xp, xs, norm_mix[i], state_ssm_re[j], state_ssm_im[j], ssm_a_re[j], ssm_a_im[j], ssm_log_dt[j],
                ssm_b_re[j], ssm_b_im[j], ssm_c_re[j], ssm_c_im[j], ssm_d[j], ssm_w_glu, j, bp, t, tm)
            ssm_p.append(outp); ssm_s.append(outs)
        xp = ffn_res(xp, norm_ffn[i], ffn_w_gu, ffn_w_down, i, tm)
        xs = ffn_res(xs, norm_ffn[i], ffn_w_gu, ffn_w_down, i, bs_ * ts)

    def stack(items, idx, shape):
        return jnp.stack([it[idx].reshape(shape) for it in items])

    kv_p, kv_s = (bp, t, N_KV, HEAD_DIM), (bs_, ts, N_KV, HEAD_DIM)
    st_p, st_s = (bp, N_SSM_GROUPS, SSM_STATE), (bs_, N_SSM_GROUPS, SSM_STATE)
    return (xp.reshape(bp, t, d), xs.reshape(bs_, ts, d),
            stack(dsa_p, 0, kv_p), stack(dsa_p, 1, kv_p), stack(dsa_p, 2, (bp, t, IDX_DIM)),
            stack(dsa_s, 0, kv_s), stack(dsa_s, 1, kv_s), stack(dsa_s, 2, (bs_, ts, IDX_DIM)),
            stack(fox_p, 0, kv_p), stack(fox_p, 1, kv_p), stack(fox_p, 2, (bp, t, N_HEADS)),
            stack(fox_s, 0, kv_s), stack(fox_s, 1, kv_s), stack(fox_s, 2, (bs_, ts, N_HEADS)),
            stack(ssm_p, 0, st_p), stack(ssm_p, 1, st_p), stack(ssm_s, 0, st_s), stack(ssm_s, 1, st_s))
```
